```python
import jax, jax.numpy as jnp
from jax import lax
import numpy as np

D_MODEL = 1024
BATCH = 8
SEQ = 2048
DEPTH = 4
DEC_BATCH = 128
DEC_SEQ = 8
PAST_LEN = 16384
PAGE_SIZE = 128

N_MIXERS = 2
D_FF = 2816
EXPAND = 2
D_INNER = EXPAND * D_MODEL
SSD_HEAD_DIM = 64
SSD_HEADS = D_INNER // SSD_HEAD_DIM
SSD_GROUPS = 4
HEADS_PER_GROUP = SSD_HEADS // SSD_GROUPS
D_STATE = 128
CONV_W = 4
CONV_DIM = D_INNER + 2 * SSD_GROUPS * D_STATE
IN_DIM = D_INNER + CONV_DIM + SSD_HEADS
SSD_CHUNK = 128
POOL_WINDOWS = (2, 4, 8, 16)
POOL_GROUPS = len(POOL_WINDOWS)
POOL_GW = D_MODEL // POOL_GROUPS
POOL_BUF = max(POOL_WINDOWS) - 1
N_MEM = 256
MEM_HEADS = 4
MEM_HEAD_DIM = D_MODEL // MEM_HEADS
N_SSD_LAYERS = (DEPTH + 1) // 2
N_POOL_LAYERS = DEPTH // 2
EPS = 1e-5

kernel_name = "hybrid_ssd_pool_macaron_memxattn_step"


def _rms(x, g):
    xf = x.astype(jnp.float32)
    r = lax.rsqrt(jnp.mean(xf * xf, axis=-1, keepdims=True) + EPS)
    return (xf * r * g.astype(jnp.float32)).astype(x.dtype)


def _swiglu(u, wg, wu, wd):
    return (jax.nn.silu(u @ wg) * (u @ wu)) @ wd


def _ssd_scan(x, dt, a, bm, cm, h0):
    Bsz, L = x.shape[0], x.shape[1]
    q = SSD_CHUNK if L % SSD_CHUNK == 0 else L
    nc = L // q
    x = x.reshape(Bsz, nc, q, SSD_GROUPS, HEADS_PER_GROUP, SSD_HEAD_DIM)
    dt = dt.reshape(Bsz, nc, q, SSD_GROUPS, HEADS_PER_GROUP)
    bm = bm.reshape(Bsz, nc, q, SSD_GROUPS, D_STATE)
    cm = cm.reshape(Bsz, nc, q, SSD_GROUPS, D_STATE)
    acs = jnp.cumsum(dt * a.reshape(SSD_GROUPS, HEADS_PER_GROUP), axis=2)
    xdt = x * dt[..., None]
    causal = jnp.tril(jnp.ones((q, q), dtype=bool))[:, :, None, None]
    seg = acs[:, :, :, None] - acs[:, :, None, :]
    decay = jnp.exp(jnp.where(causal, seg, -jnp.inf))
    cb = jnp.einsum('bcqgn,bckgn->bcqkg', cm, bm)
    y_diag = jnp.einsum('bcqkgh,bckghp->bcqghp', cb[..., None] * decay, xdt)
    decay_to_end = jnp.exp(acs[:, :, -1:] - acs)
    states = jnp.einsum('bcqgn,bcqgh,bcqghp->bcghpn', bm, decay_to_end, xdt)
    chunk_decay = jnp.exp(acs[:, :, -1])

    def step(h, inp):
        st, dec = inp
        return h * dec[..., None, None] + st, h

    h_init = h0.reshape(Bsz, SSD_GROUPS, HEADS_PER_GROUP, SSD_HEAD_DIM, D_STATE)
    h_final, h_prev = lax.scan(step, h_init, (jnp.moveaxis(states, 1, 0), jnp.moveaxis(chunk_decay, 1, 0)))
    h_prev = jnp.moveaxis(h_prev, 0, 1)
    y_off = jnp.einsum('bcqgn,bcghpn,bcqgh->bcqghp', cm, h_prev, jnp.exp(acs))
    y = (y_diag + y_off).reshape(Bsz, L, SSD_HEADS, SSD_HEAD_DIM)
    return y, h_final.reshape(Bsz, SSD_HEADS, SSD_HEAD_DIM, D_STATE)


def _ssd_mixer(u, conv_buf, ssm_state, in_w, conv_w, conv_b, dt_bias, a_log, d_skip, norm_w, out_w):
    Bsz, L, _ = u.shape
    zxbcdt = u @ in_w
    z = zxbcdt[..., :D_INNER]
    xbc = zxbcdt[..., D_INNER:D_INNER + CONV_DIM]
    dt_raw = zxbcdt[..., D_INNER + CONV_DIM:]
    xx = jnp.concatenate([conv_buf.astype(xbc.dtype), xbc], axis=1)
    conv = conv_b + sum(xx[:, k:k + L] * conv_w[k] for k in range(CONV_W))
    xbc = jax.nn.silu(conv)
    new_conv = xx[:, L:].astype(conv_buf.dtype)
    gn = SSD_GROUPS * D_STATE
    xs = xbc[..., :D_INNER].reshape(Bsz, L, SSD_HEADS, SSD_HEAD_DIM).astype(jnp.float32)
    bm = xbc[..., D_INNER:D_INNER + gn].reshape(Bsz, L, SSD_GROUPS, D_STATE).astype(jnp.float32)
    cm = xbc[..., D_INNER + gn:].reshape(Bsz, L, SSD_GROUPS, D_STATE).astype(jnp.float32)
    dt = jax.nn.softplus(dt_raw.astype(jnp.float32) + dt_bias.astype(jnp.float32))
    a = -jnp.exp(a_log.astype(jnp.float32))
    y, new_ssm = _ssd_scan(xs, dt, a, bm, cm, ssm_state.astype(jnp.float32))
    y = y + xs * d_skip.astype(jnp.float32)[:, None]
    y = y.reshape(Bsz, L, D_INNER) * jax.nn.silu(z.astype(jnp.float32))
    yg = y.reshape(Bsz, L, SSD_GROUPS, D_INNER // SSD_GROUPS)
    yg = yg * lax.rsqrt(jnp.mean(yg * yg, axis=-1, keepdims=True) + EPS)
    y = (yg.reshape(Bsz, L, D_INNER) * norm_w.astype(jnp.float32)).astype(u.dtype)
    return y @ out_w, new_conv, new_ssm.astype(ssm_state.dtype)


def _pool_mixer(u, buf, pos0, pool_w, pool_scale):
    Bsz, L, _ = u.shape
    xx = jnp.concatenate([buf.astype(jnp.float32), u.astype(jnp.float32)], axis=1)
    cs0 = jnp.concatenate([jnp.zeros((Bsz, 1, D_MODEL), jnp.float32), jnp.cumsum(xx, axis=1)], axis=1)
    end = cs0[:, POOL_BUF + 1:]
    pos = (pos0 + jnp.arange(L)).astype(jnp.float32)
    uf = u.astype(jnp.float32)
    outs = []
    for g, w in enumerate(POOL_WINDOWS):
        lo, hi = g * POOL_GW, (g + 1) * POOL_GW
        start = cs0[:, POOL_BUF + 1 - w:POOL_BUF + 1 - w + L, lo:hi]
        cnt = jnp.minimum(pos + 1.0, float(w))[None, :, None]
        mix = (end[..., lo:hi] - start) / cnt - uf[..., lo:hi]
        outs.append(jnp.einsum('bld,de->ble', mix, pool_w[g].astype(jnp.float32)))
    out = (jnp.concatenate(outs, axis=-1) * pool_scale.astype(jnp.float32)).astype(u.dtype)
    return out, xx[:, L:].astype(buf.dtype)


def _mem_kv(mem, g, wk, wv):
    Bsz = mem.shape[0]
    m = _rms(mem, g)
    k = (m @ wk).reshape(Bsz, N_MEM, MEM_HEADS, MEM_HEAD_DIM)
    v = (m @ wv).reshape(Bsz, N_MEM, MEM_HEADS, MEM_HEAD_DIM)
    return k, v


def _cross_attn(u, mk, mv, wq, wo):
    Bsz, L, _ = u.shape
    q = (u @ wq).reshape(Bsz, L, MEM_HEADS, MEM_HEAD_DIM)
    s = jnp.einsum('blhd,bmhd->bhlm', q.astype(jnp.float32), mk.astype(jnp.float32)) * (MEM_HEAD_DIM ** -0.5)
    p = jax.nn.softmax(s, axis=-1).astype(mv.dtype)
    o = jnp.einsum('bhlm,bmhd->blhd', p, mv).reshape(Bsz, L, D_MODEL)
    return o @ wo


def setup_inputs(seed: int = 0) -> dict:
    key = jax.random.key(seed)
    ks = iter(jax.random.split(key, 64))
    f32 = jnp.float32

    def nrm(shape, scale):
        return jax.random.normal(next(ks), shape, f32) * scale

    def gain(shape):
        return 1.0 + 0.05 * jax.random.normal(next(ks), shape, f32)

    dt0 = jnp.exp(jax.random.uniform(next(ks), (N_SSD_LAYERS, SSD_HEADS), f32)
                  * (np.log(0.1) - np.log(0.001)).astype(np.float32) + np.float32(np.log(0.001)))
    dt_bias = dt0 + jnp.log(-jnp.expm1(-dt0))
    a_log = jnp.log(jax.random.uniform(next(ks), (N_SSD_LAYERS, SSD_HEADS), f32, minval=1.0, maxval=16.0))
    return {
        "x_prompt": nrm((BATCH, SEQ, D_MODEL), 1.0),
        "x_sample": nrm((DEC_BATCH, DEC_SEQ, D_MODEL), 1.0),
        "mem_prompt": nrm((BATCH, N_MEM, D_MODEL), 1.0),
        "cache_mem_k": nrm((DEPTH, DEC_BATCH, N_MEM, MEM_HEADS, MEM_HEAD_DIM), 1.0),
        "cache_mem_v": nrm((DEPTH, DEC_BATCH, N_MEM, MEM_HEADS, MEM_HEAD_DIM), 1.0),
        "state_ssm": nrm((N_SSD_LAYERS, DEC_BATCH, SSD_HEADS, SSD_HEAD_DIM, D_STATE), 0.3),
        "state_conv": nrm((N_SSD_LAYERS, DEC_BATCH, CONV_W - 1, CONV_DIM), 1.0),
        "state_pool": nrm((N_POOL_LAYERS, DEC_BATCH, POOL_BUF, D_MODEL), 1.0),
        "norm_ffn1": gain((DEPTH, D_MODEL)),
        "ffn1_w_gate": nrm((DEPTH, D_MODEL, D_FF), D_MODEL ** -0.5),
        "ffn1_w_up": nrm((DEPTH, D_MODEL, D_FF), D_MODEL ** -0.5),
        "ffn1_w_down": nrm((DEPTH, D_FF, D_MODEL), D_FF ** -0.5),
        "norm_mix": gain((DEPTH, D_MODEL)),
        "ssd_in_w": nrm((N_SSD_LAYERS, D_MODEL, IN_DIM), D_MODEL ** -0.5),
        "ssd_conv_w": nrm((N_SSD_LAYERS, CONV_W, CONV_DIM), CONV_W ** -0.5),
        "ssd_conv_b": nrm((N_SSD_LAYERS, CONV_DIM), 0.02),
        "ssd_dt_bias": dt_bias,
        "ssd_a_log": a_log,
        "ssd_d": gain((N_SSD_LAYERS, SSD_HEADS)),
        "ssd_norm_w": gain((N_SSD_LAYERS, D_INNER)),
        "ssd_out_w": nrm((N_SSD_LAYERS, D_INNER, D_MODEL), D_INNER ** -0.5),
        "pool_w": nrm((N_POOL_LAYERS, POOL_GROUPS, POOL_GW, POOL_GW), POOL_GW ** -0.5),
        "pool_scale": gain((N_POOL_LAYERS, D_MODEL)),
        "norm_cross": gain((DEPTH, D_MODEL)),
        "norm_mem": gain((DEPTH, D_MODEL)),
        "xa_wq": nrm((DEPTH, D_MODEL, D_MODEL), D_MODEL ** -0.5),
        "xa_wk": nrm((DEPTH, D_MODEL, D_MODEL), D_MODEL ** -0.5),
        "xa_wv": nrm((DEPTH, D_MODEL, D_MODEL), D_MODEL ** -0.5),
        "xa_wo": nrm((DEPTH, D_MODEL, D_MODEL), D_MODEL ** -0.5),
        "norm_ffn2": gain((DEPTH, D_MODEL)),
        "ffn2_w_gate": nrm((DEPTH, D_MODEL, D_FF), D_MODEL ** -0.5),
        "ffn2_w_up": nrm((DEPTH, D_MODEL, D_FF), D_MODEL ** -0.5),
        "ffn2_w_down": nrm((DEPTH, D_FF, D_MODEL), D_FF ** -0.5),
        "final_norm": gain((D_MODEL,)),
    }


def reference(x_prompt, x_sample, mem_prompt, cache_mem_k, cache_mem_v, state_ssm, state_conv, state_pool,
              norm_ffn1, ffn1_w_gate, ffn1_w_up, ffn1_w_down, norm_mix,
              ssd_in_w, ssd_conv_w, ssd_conv_b, ssd_dt_bias, ssd_a_log, ssd_d, ssd_norm_w, ssd_out_w,
              pool_w, pool_scale, norm_cross, norm_mem, xa_wq, xa_wk, xa_wv, xa_wo,
              norm_ffn2, ffn2_w_gate, ffn2_w_up, ffn2_w_down, final_norm):

    def run_group(x, pos0, conv_in, ssm_in, pool_in, mem_k, mem_v):
        new_conv, new_ssm, new_pool = [], [], []
        for i in range(DEPTH):
            j = i // N_MIXERS
            x = x + 0.5 * _swiglu(_rms(x, norm_ffn1[i]), ffn1_w_gate[i], ffn1_w_up[i], ffn1_w_down[i])
            u = _rms(x, norm_mix[i])
            if i % N_MIXERS == 0:
                out, cs, ss = _ssd_mixer(u, conv_in[j], ssm_in[j], ssd_in_w[j], ssd_conv_w[j], ssd_conv_b[j],
                                         ssd_dt_bias[j], ssd_a_log[j], ssd_d[j], ssd_norm_w[j], ssd_out_w[j])
                new_conv.append(cs)
                new_ssm.append(ss)
            else:
                out, ps = _pool_mixer(u, pool_in[j], pos0, pool_w[j], pool_scale[j])
                new_pool.append(ps)
            x = x + out
            x = x + _cross_attn(_rms(x, norm_cross[i]), mem_k[i], mem_v[i], xa_wq[i], xa_wo[i])
            x = x + 0.5 * _swiglu(_rms(x, norm_ffn2[i]), ffn2_w_gate[i], ffn2_w_up[i], ffn2_w_down[i])
        return _rms(x, final_norm), jnp.stack(new_ssm), jnp.stack(new_conv), jnp.stack(new_pool)

    dt_p = x_prompt.dtype
    conv0 = [jnp.zeros((BATCH, CONV_W - 1, CONV_DIM), dt_p) for _ in range(N_SSD_LAYERS)]
    ssm0 = [jnp.zeros((BATCH, SSD_HEADS, SSD_HEAD_DIM, D_STATE), dt_p) for _ in range(N_SSD_LAYERS)]
    pool0 = [jnp.zeros((BATCH, POOL_BUF, D_MODEL), dt_p) for _ in range(N_POOL_LAYERS)]
    mk_p, mv_p = [], []
    for i in range(DEPTH):
        k_i, v_i = _mem_kv(mem_prompt, norm_mem[i], xa_wk[i], xa_wv[i])
        mk_p.append(k_i)
        mv_p.append(v_i)
    y_prompt, ssm_p, conv_p, pool_p = run_group(x_prompt, 0, conv0, ssm0, pool0, mk_p, mv_p)
    new_mem_k_prompt = jnp.stack(mk_p)
    new_mem_v_prompt = jnp.stack(mv_p)

    y_sample, ssm_s, conv_s, pool_s = run_group(x_sample, PAST_LEN, state_conv, state_ssm, state_pool,
                                                cache_mem_k, cache_mem_v)
    return (y_prompt, y_sample, ssm_p, conv_p, pool_p, new_mem_k_prompt, new_mem_v_prompt, ssm_s, conv_s, pool_s)
```

```python
import functools

import jax
import jax.numpy as jnp
from jax import lax
from jax.experimental import pallas as pl
from jax.experimental.pallas import tpu as pltpu

F32 = jnp.float32
BF16 = jnp.bfloat16

D_MODEL = 1024
BATCH = 8
SEQ = 2048
DEPTH = 4
DEC_BATCH = 128
DEC_SEQ = 8
PAST_LEN = 16384
D_FF = 2816
D_INNER = 2048
SSD_HEAD_DIM = 64
SSD_HEADS = 32
SSD_GROUPS = 4
HEADS_PER_GROUP = 8
D_STATE = 128
CONV_W = 4
GN = SSD_GROUPS * D_STATE
CONV_DIM = D_INNER + 2 * GN
SSD_CHUNK = 128
POOL_WINDOWS = (2, 4, 8, 16)
POOL_GW = 256
POOL_BUF = 15
N_MEM = 256
MEM_HEADS = 4
MEM_HEAD_DIM = 256
EPS = 1e-5

ROWS_P = BATCH * SEQ
ROWS_S = DEC_BATCH * DEC_SEQ
ROWS = ROWS_P + ROWS_S

LANES = 128
XBC_OFF = D_INNER
DT_OFF = D_INNER + CONV_DIM
IN_PAD_P = 5376
IN_PAD_S = DT_OFF + D_INNER
VMEM_LIMIT = 48 * 1024 * 1024


def _params(n_axes, vmem=VMEM_LIMIT):
    return pltpu.CompilerParams(dimension_semantics=("arbitrary",) * n_axes,
                                vmem_limit_bytes=vmem)


def _rms(x, g):
    r = lax.rsqrt(jnp.mean(x * x, axis=-1, keepdims=True) + EPS)
    return x * r * g


def _silu(x):
    return x * (1.0 / (1.0 + jnp.exp(-x)))


def _softplus(x):
    return jnp.maximum(x, 0.0) + jnp.log1p(jnp.exp(-jnp.abs(x)))


def _dot(a, b):
    return jnp.dot(a, b, preferred_element_type=F32)


def _ffn_kernel(x_ref, g_ref, wg_ref, wu_ref, wd_ref, o_ref, u_ref, acc_ref, *, nf):
    f = pl.program_id(1)

    @pl.when(f == 0)
    def _():
        u_ref[...] = _rms(x_ref[...], g_ref[...]).astype(BF16)
        acc_ref[...] = jnp.zeros_like(acc_ref)

    u = u_ref[...]
    a = _dot(u, wg_ref[...])
    b = _dot(u, wu_ref[...])
    h = (_silu(a) * b).astype(BF16)
    acc_ref[...] += _dot(h, wd_ref[...])

    @pl.when(f == nf - 1)
    def _():
        o_ref[...] = x_ref[...] + 0.5 * acc_ref[...]


def _ffn(x, g, wg, wu, wd, *, tm=512, tf=1408):
    rows = x.shape[0]
    nf = D_FF // tf
    return pl.pallas_call(
        functools.partial(_ffn_kernel, nf=nf),
        grid=(rows // tm, nf),
        in_specs=[
            pl.BlockSpec((tm, D_MODEL), lambda i, f: (i, 0)),
            pl.BlockSpec((1, D_MODEL), lambda i, f: (0, 0)),
            pl.BlockSpec((D_MODEL, tf), lambda i, f: (0, f)),
            pl.BlockSpec((D_MODEL, tf), lambda i, f: (0, f)),
            pl.BlockSpec((tf, D_MODEL), lambda i, f: (f, 0)),
        ],
        out_specs=pl.BlockSpec((tm, D_MODEL), lambda i, f: (i, 0)),
        out_shape=jax.ShapeDtypeStruct((rows, D_MODEL), F32),
        scratch_shapes=[pltpu.VMEM((tm, D_MODEL), BF16), pltpu.VMEM((tm, D_MODEL), F32)],
        compiler_params=_params(2),
        name="ffn",
    )(x, g, wg, wu, wd)


def _rms_mm_kernel(x_ref, g_ref, w_ref, o_ref, u_ref):
    @pl.when(pl.program_id(1) == 0)
    def _():
        u_ref[...] = _rms(x_ref[...], g_ref[...]).astype(BF16)

    o_ref[...] = _dot(u_ref[...], w_ref[...])


def _rms_matmul(x, g, w, *, row_start, n_rows, tm, tn, name):
    n = w.shape[1]
    r0 = row_start // tm
    return pl.pallas_call(
        _rms_mm_kernel,
        grid=(n_rows // tm, n // tn),
        in_specs=[
            pl.BlockSpec((tm, D_MODEL), lambda i, j: (r0 + i, 0)),
            pl.BlockSpec((1, D_MODEL), lambda i, j: (0, 0)),
            pl.BlockSpec((D_MODEL, tn), lambda i, j: (0, j)),
        ],
        out_specs=pl.BlockSpec((tm, tn), lambda i, j: (i, j)),
        out_shape=jax.ShapeDtypeStruct((n_rows, n), F32),
        scratch_shapes=[pltpu.VMEM((tm, D_MODEL), BF16)],
        compiler_params=_params(2),
        name=name,
    )(x, g, w)


def _mem_kv_kernel(m_ref, g_ref, wk_ref, wv_ref, k_ref, v_ref):
    u = _rms(m_ref[...], g_ref[...]).astype(BF16)
    k_ref[...] = _dot(u, wk_ref[...])
    v_ref[...] = _dot(u, wv_ref[...])


def _mem_kv(mem, g, wk, wv, *, tm=512):
    rows = mem.shape[0]
    w_spec = pl.BlockSpec((None, D_MODEL, D_MODEL), lambda l, r: (l, 0, 0))
    o_spec = pl.BlockSpec((None, tm, D_MODEL), lambda l, r: (l, r, 0))
    o_shape = jax.ShapeDtypeStruct((DEPTH, rows, D_MODEL), F32)
    return pl.pallas_call(
        _mem_kv_kernel,
        grid=(DEPTH, rows // tm),
        in_specs=[
            pl.BlockSpec((tm, D_MODEL), lambda l, r: (r, 0)),
            pl.BlockSpec((None, 1, D_MODEL), lambda l, r: (l, 0, 0)),
            w_spec, w_spec,
        ],
        out_specs=[o_spec, o_spec],
        out_shape=[o_shape, o_shape],
        compiler_params=_params(2),
        name="mem_kv",
    )(mem, g, wk, wv)


def _mm_res_kernel(x_ref, y_ref, w_ref, o_ref):
    o_ref[...] = x_ref[...] + _dot(y_ref[...].astype(BF16), w_ref[...])


def _matmul_residual(x, y, w, *, row_start, tm, name):
    n_rows, k = y.shape
    r0 = row_start // tm
    return pl.pallas_call(
        _mm_res_kernel,
        grid=(n_rows // tm,),
        in_specs=[
            pl.BlockSpec((tm, D_MODEL), lambda i: (r0 + i, 0)),
            pl.BlockSpec((tm, k), lambda i: (i, 0)),
            pl.BlockSpec((k, D_MODEL), lambda i: (0, 0)),
        ],
        out_specs=pl.BlockSpec((tm, D_MODEL), lambda i: (r0 + i, 0)),
        out_shape=jax.ShapeDtypeStruct(x.shape, F32),
        input_output_aliases={0: 0},
        compiler_params=_params(1),
        name=name,
    )(x, y, w)


def _ssd_prompt_kernel(zx_ref, cbuf_ref, st_ref, cw_ref, cb_ref, dtb_ref, alog_ref, dsk_ref, nw_ref,
                       y_ref, nconv_ref, nst_ref, xpad_ref, ht_ref, ybuf_ref, *, nc):
    c = pl.program_id(1)
    q = SSD_CHUNK
    lead = 8
    tail0 = lead - (CONV_W - 1)

    @pl.when(c == 0)
    def _():
        xpad_ref[tail0:lead, :] = cbuf_ref[...]
        ht_ref[...] = st_ref[...].T

    xpad_ref[lead:lead + q, :] = zx_ref[:, XBC_OFF:DT_OFF]
    conv = cw_ref[0:1, :] * xpad_ref[tail0:tail0 + q, :]
    for k in range(1, CONV_W):
        conv = conv + cw_ref[k:k + 1, :] * xpad_ref[tail0 + k:tail0 + k + q, :]
    conv = cb_ref[...] + conv
    tail = xpad_ref[q + tail0:q + lead, :]
    xpad_ref[tail0:lead, :] = tail

    @pl.when(c == nc - 1)
    def _():
        nconv_ref[...] = tail

    xbc = _silu(conv)
    xs = xbc[:, :D_INNER]
    bm = xbc[:, D_INNER:D_INNER + GN]
    cm = xbc[:, D_INNER + GN:]

    dt = _softplus(zx_ref[:, DT_OFF:DT_OFF + LANES] + dtb_ref[...])
    a = -jnp.exp(alog_ref[...])
    da = dt * a
    rows = lax.broadcasted_iota(jnp.int32, (q, q), 0)
    cols = lax.broadcasted_iota(jnp.int32, (q, q), 1)
    causal = rows >= cols
    tri = jnp.where(causal, 1.0, 0.0).astype(F32)
    acs = jnp.dot(tri, da, precision=lax.Precision.HIGHEST, preferred_element_type=F32)
    acs_t = acs.T
    dt_t = dt.T
    eacs = jnp.exp(acs)
    last = acs_t[:, q - 1:q]
    w_t = jnp.exp(last - acs_t) * dt_t
    cd_t = jnp.exp(last)
    low_half = lax.broadcasted_iota(jnp.int32, (1, LANES), 1) < SSD_HEAD_DIM

    for g in range(SSD_GROUPS):
        bm_g = bm[:, g * D_STATE:(g + 1) * D_STATE]
        cm_g = cm[:, g * D_STATE:(g + 1) * D_STATE]
        bm_gt = bm_g.T
        cb_g = _dot(cm_g.astype(BF16), bm_gt.astype(BF16))
        for hh in range(0, HEADS_PER_GROUP, 2):
            h0 = g * HEADS_PER_GROUP + hh
            lo = (h0 // 2) * LANES
            x_pair = xs[:, lo:lo + LANES]
            ht_pair = ht_ref[:, lo:lo + LANES]
            x_pair_b = x_pair.astype(BF16)
            rhs = jnp.concatenate([x_pair_b, ht_pair.astype(BF16)], axis=0)
            ys, sts = [], []
            for h in (h0, h0 + 1):
                seg = acs[:, h:h + 1] - acs_t[h:h + 1, :]
                decay = jnp.exp(jnp.where(causal, seg, -jnp.inf))
                m_h = cb_g * decay * dt_t[h:h + 1, :]
                e_h = eacs[:, h:h + 1] * cm_g
                lhs = jnp.concatenate([m_h.astype(BF16), e_h.astype(BF16)], axis=1)
                ys.append(_dot(lhs, rhs))
                b_h = bm_gt * w_t[h:h + 1, :]
                sts.append(_dot(b_h.astype(BF16), x_pair_b))
            y_pair = jnp.where(low_half, ys[0], ys[1]) + x_pair * dsk_ref[:, lo:lo + LANES]
            st_pair = jnp.where(low_half, sts[0], sts[1])
            cd_row = jnp.where(low_half, cd_t[h0:h0 + 1, :], cd_t[h0 + 1:h0 + 2, :])
            ht_ref[:, lo:lo + LANES] = ht_pair * cd_row + st_pair
            ybuf_ref[:, lo:lo + LANES] = y_pair

    y = ybuf_ref[...] * _silu(zx_ref[:, :D_INNER])
    gw = D_INNER // SSD_GROUPS
    for g in range(SSD_GROUPS):
        yg = y[:, g * gw:(g + 1) * gw]
        yg = yg * lax.rsqrt(jnp.mean(yg * yg, axis=-1, keepdims=True) + EPS)
        y_ref[:, g * gw:(g + 1) * gw] = (yg * nw_ref[:, g * gw:(g + 1) * gw]).astype(BF16)

    @pl.when(c == nc - 1)
    def _():
        nst_ref[...] = ht_ref[...].T


def _ssd_prompt(zx, cbuf, st, cw, cb, dtb, alog, dsk, nw):
    nb = cbuf.shape[0]
    nc = SEQ // SSD_CHUNK
    q = SSD_CHUNK
    vec = lambda n: pl.BlockSpec((1, n), lambda b, c: (0, 0))
    return pl.pallas_call(
        functools.partial(_ssd_prompt_kernel, nc=nc),
        grid=(nb, nc),
        in_specs=[
            pl.BlockSpec((q, IN_PAD_P), lambda b, c: (b * nc + c, 0)),
            pl.BlockSpec((None, CONV_W - 1, CONV_DIM), lambda b, c: (b, 0, 0)),
            pl.BlockSpec((None, D_INNER, D_STATE), lambda b, c: (b, 0, 0)),
            pl.BlockSpec((CONV_W, CONV_DIM), lambda b, c: (0, 0)),
            vec(CONV_DIM), vec(LANES), vec(LANES), vec(D_INNER), vec(D_INNER),
        ],
        out_specs=[
            pl.BlockSpec((q, D_INNER), lambda b, c: (b * nc + c, 0)),
            pl.BlockSpec((None, CONV_W - 1, CONV_DIM), lambda b, c: (b, 0, 0)),
            pl.BlockSpec((None, D_INNER, D_STATE), lambda b, c: (b, 0, 0)),
        ],
        out_shape=[
            jax.ShapeDtypeStruct((nb * SEQ, D_INNER), BF16),
            jax.ShapeDtypeStruct((nb, CONV_W - 1, CONV_DIM), F32),
            jax.ShapeDtypeStruct((nb, D_INNER, D_STATE), F32),
        ],
        scratch_shapes=[
            pltpu.VMEM((q + 8, CONV_DIM), F32),
            pltpu.VMEM((D_STATE, D_INNER), F32),
            pltpu.VMEM((q, D_INNER), F32),
        ],
        compiler_params=_params(2),
        name="ssd_prompt",
    )(zx, cbuf, st, cw, cb, dtb, alog, dsk, nw)


def _ssd_sample_kernel(zx_ref, cbuf_ref, st_ref, cw_ref, cb_ref, dtb_ref, alog_ref, dsk_ref, nw_ref,
                       y_ref, nconv_ref, nst_ref, xpad_ref):
    q = DEC_SEQ
    lead = 8
    tail0 = lead - (CONV_W - 1)
    xpad_ref[tail0:lead, :] = cbuf_ref[...]
    xpad_ref[lead:lead + q, :] = zx_ref[:, XBC_OFF:DT_OFF]
    conv = cw_ref[0:1, :] * xpad_ref[tail0:tail0 + q, :]
    for k in range(1, CONV_W):
        conv = conv + cw_ref[k:k + 1, :] * xpad_ref[tail0 + k:tail0 + k + q, :]
    conv = cb_ref[...] + conv
    nconv_ref[...] = xpad_ref[q + tail0:q + lead, :]

    xbc = _silu(conv)
    xs = xbc[:, :D_INNER]
    bm = xbc[:, D_INNER:D_INNER + GN]
    cm = xbc[:, D_INNER + GN:]

    dt = _softplus(zx_ref[:, DT_OFF:DT_OFF + D_INNER] + dtb_ref[...])
    a = -jnp.exp(alog_ref[...])
    da = dt * a
    rows = lax.broadcasted_iota(jnp.int32, (q, 1), 0)
    acs = jnp.where(rows >= 0, da[0:1, :], 0.0)
    for k in range(1, q):
        acs = acs + jnp.where(rows >= k, da[k:k + 1, :], 0.0)
    xdt = xs * dt
    gw = D_INNER // SSD_GROUPS

    y = jnp.zeros((q, D_INNER), F32)
    for k in range(q):
        decay = jnp.exp(jnp.where(rows >= k, acs - acs[k:k + 1, :], -jnp.inf))
        cb_k = jnp.concatenate(
            [jnp.broadcast_to(
                jnp.sum(cm[:, g * D_STATE:(g + 1) * D_STATE] * bm[k:k + 1, g * D_STATE:(g + 1) * D_STATE],
                        axis=-1, keepdims=True), (q, gw)) for g in range(SSD_GROUPS)], axis=1)
        y = y + cb_k * decay * xdt[k:k + 1, :]

    h0 = st_ref[...]
    cm_rows = jnp.concatenate([cm[:, g * D_STATE:(g + 1) * D_STATE] for g in range(SSD_GROUPS)], axis=0)
    r = lax.dot_general(cm_rows.astype(BF16), h0.astype(BF16), (((1,), (1,)), ((), ())),
                        preferred_element_type=F32)
    y_off = jnp.concatenate([r[g * q:(g + 1) * q, g * gw:(g + 1) * gw] for g in range(SSD_GROUPS)], axis=1)
    y = y + jnp.exp(acs) * y_off
    y = y + xs * dsk_ref[...]

    y = y * _silu(zx_ref[:, :D_INNER])
    for g in range(SSD_GROUPS):
        yg = y[:, g * gw:(g + 1) * gw]
        yg = yg * lax.rsqrt(jnp.mean(yg * yg, axis=-1, keepdims=True) + EPS)
        y_ref[:, g * gw:(g + 1) * gw] = yg * nw_ref[:, g * gw:(g + 1) * gw]

    last = acs[q - 1:q, :]
    xw = xdt * jnp.exp(last - acs)
    cd = jnp.broadcast_to(jnp.exp(last), (q, D_INNER))
    padded = jnp.concatenate([xw, cd, jnp.zeros((LANES - 2 * q, D_INNER), F32)], axis=0)
    padded_t = padded.T
    cd_col = padded_t[:, q:q + 1]
    zeros_k = jnp.zeros((LANES - q, D_STATE), F32)
    for g in range(SSD_GROUPS):
        bm_pad = jnp.concatenate([bm[:, g * D_STATE:(g + 1) * D_STATE], zeros_k], axis=0)
        st_g = _dot(padded_t[g * gw:(g + 1) * gw, :].astype(BF16), bm_pad.astype(BF16))
        nst_ref[g * gw:(g + 1) * gw, :] = h0[g * gw:(g + 1) * gw, :] * cd_col[g * gw:(g + 1) * gw, :] + st_g


def _ssd_sample(zx, cbuf, st, layer, cw, cb, dtb, alog, dsk, nw):
    q = DEC_SEQ
    vec = lambda n: pl.BlockSpec((1, n), lambda b: (0, 0))
    return pl.pallas_call(
        _ssd_sample_kernel,
        grid=(DEC_BATCH,),
        in_specs=[
            pl.BlockSpec((q, IN_PAD_S), lambda b: (b, 0)),
            pl.BlockSpec((None, None, CONV_W - 1, CONV_DIM), lambda b: (layer, b, 0, 0)),
            pl.BlockSpec((None, None, D_INNER, D_STATE), lambda b: (layer, b, 0, 0)),
            pl.BlockSpec((CONV_W, CONV_DIM), lambda b: (0, 0)),
            vec(CONV_DIM), vec(D_INNER), vec(D_INNER), vec(D_INNER), vec(D_INNER),
        ],
        out_specs=[
            pl.BlockSpec((q, D_INNER), lambda b: (b, 0)),
            pl.BlockSpec((None, CONV_W - 1, CONV_DIM), lambda b: (b, 0, 0)),
            pl.BlockSpec((None, D_INNER, D_STATE), lambda b: (b, 0, 0)),
        ],
        out_shape=[
            jax.ShapeDtypeStruct((ROWS_S, D_INNER), F32),
            jax.ShapeDtypeStruct((DEC_BATCH, CONV_W - 1, CONV_DIM), F32),
            jax.ShapeDtypeStruct((DEC_BATCH, D_INNER, D_STATE), F32),
        ],
        scratch_shapes=[pltpu.VMEM((q + 8, CONV_DIM), F32)],
        compiler_params=_params(1),
        name="ssd_sample",
    )(zx, cbuf, st, cw, cb, dtb, alog, dsk, nw)


def _pool_kernel(x_ref, g_ref, buf_ref, pw_ref, ps_ref, o_ref, np_ref, xx_ref, *, tb, tl, nl, pos0):
    l = pl.program_id(1)
    x = x_ref[...]
    u = _rms(x, g_ref[...]).reshape(tb, tl, D_MODEL)
    lead = POOL_BUF + 1

    @pl.when(l == 0)
    def _():
        xx_ref[:, 1:lead, :] = buf_ref[...]

    xx_ref[:, lead:lead + tl, :] = u
    pos = (pos0 + l * tl + lax.broadcasted_iota(jnp.int32, (1, tl, 1), 1)).astype(F32)
    outs = []
    for gi, w in enumerate(POOL_WINDOWS):
        lo = gi * POOL_GW
        s = xx_ref[:, lead:lead + tl, lo:lo + POOL_GW]
        for j in range(1, w):
            s = s + xx_ref[:, lead - j:lead - j + tl, lo:lo + POOL_GW]
        cnt = jnp.minimum(pos + 1.0, float(w))
        mix = s / cnt - u[:, :, lo:lo + POOL_GW]
        outs.append(_dot(mix.reshape(tb * tl, POOL_GW).astype(BF16), pw_ref[gi]))
    o_ref[...] = x + jnp.concatenate(outs, axis=-1) * ps_ref[...]

    tail = xx_ref[:, tl + 1:tl + lead, :]
    if nl > 1:
        xx_ref[:, 1:lead, :] = tail

    @pl.when(l == nl - 1)
    def _():
        np_ref[...] = tail


def _pool(x, g, buf_arr, buf_index, pw, ps, *, row_start, nb, seq, tb, tl, pos0, name):
    nl = seq // tl
    r0 = row_start // (tb * tl)
    x_spec = pl.BlockSpec((tb * tl, D_MODEL), lambda b, l: (r0 + b * nl + l, 0))
    return pl.pallas_call(
        functools.partial(_pool_kernel, tb=tb, tl=tl, nl=nl, pos0=pos0),
        grid=(nb // tb, nl),
        in_specs=[
            x_spec,
            pl.BlockSpec((1, D_MODEL), lambda b, l: (0, 0)),
            buf_index(tb),
            pl.BlockSpec((len(POOL_WINDOWS), POOL_GW, POOL_GW), lambda b, l: (0, 0, 0)),
            pl.BlockSpec((1, D_MODEL), lambda b, l: (0, 0)),
        ],
        out_specs=[x_spec, pl.BlockSpec((tb, POOL_BUF, D_MODEL), lambda b, l: (b, 0, 0))],
        out_shape=[jax.ShapeDtypeStruct(x.shape, F32),
                   jax.ShapeDtypeStruct((nb, POOL_BUF, D_MODEL), F32)],
        scratch_shapes=[pltpu.VMEM((tb, tl + POOL_BUF + 1, D_MODEL), F32)],
        input_output_aliases={0: 0},
        compiler_params=_params(2),
        name=name,
    )(x, g, buf_arr, pw, ps)


def _attend(q_b, k_b, v_b):
    outs = []
    for h in range(MEM_HEADS):
        sl = slice(h * MEM_HEAD_DIM, (h + 1) * MEM_HEAD_DIM)
        s = lax.dot_general(q_b[:, sl], k_b[:, sl], (((1,), (1,)), ((), ())), preferred_element_type=F32)
        e = jnp.exp(s - jnp.max(s, axis=-1, keepdims=True))
        den = jnp.sum(e, axis=-1, keepdims=True)
        outs.append(_dot(e.astype(BF16), v_b[:, sl]) / den)
    return jnp.concatenate(outs, axis=-1)


def _xattn_prompt_kernel(x_ref, g_ref, wq_ref, k_ref, v_ref, wo_ref, o_ref):
    x = x_ref[...]
    u = _rms(x, g_ref[...]).astype(BF16)
    q = (_dot(u, wq_ref[...]) * (MEM_HEAD_DIM ** -0.5)).astype(BF16)
    o = _attend(q, k_ref[...].astype(BF16), v_ref[...].astype(BF16))
    o_ref[...] = x + _dot(o.astype(BF16), wo_ref[...])


def _xattn_prompt(x, g, wq, mk, mv, wo, layer, *, tl=512):
    nl = SEQ // tl
    x_spec = pl.BlockSpec((tl, D_MODEL), lambda b, l: (b * nl + l, 0))
    w_spec = pl.BlockSpec((D_MODEL, D_MODEL), lambda b, l: (0, 0))
    kv_spec = pl.BlockSpec((None, None, N_MEM, D_MODEL), lambda b, l: (layer, b, 0, 0))
    return pl.pallas_call(
        _xattn_prompt_kernel,
        grid=(BATCH, nl),
        in_specs=[x_spec, pl.BlockSpec((1, D_MODEL), lambda b, l: (0, 0)), w_spec, kv_spec, kv_spec, w_spec],
        out_specs=x_spec,
        out_shape=jax.ShapeDtypeStruct(x.shape, F32),
        input_output_aliases={0: 0},
        compiler_params=_params(2),
        name="xattn_prompt",
    )(x, g, wq, mk, mv, wo)


def _attn_sample_kernel(q_ref, k_ref, v_ref, o_ref, *, tb):
    n = DEC_SEQ
    rows = lax.broadcasted_iota(jnp.int32, (MEM_HEADS * n, D_MODEL), 0) // n
    cols = lax.broadcasted_iota(jnp.int32, (MEM_HEADS * n, D_MODEL), 1) // MEM_HEAD_DIM
    own = rows == cols
    for b in range(tb):
        q_b = q_ref[b * n:(b + 1) * n, :] * (MEM_HEAD_DIM ** -0.5)
        q_blk = jnp.where(own, jnp.concatenate([q_b] * MEM_HEADS, axis=0), 0.0).astype(BF16)
        k_b = k_ref[b].astype(BF16)
        v_b = v_ref[b].astype(BF16)
        s = lax.dot_general(q_blk, k_b, (((1,), (1,)), ((), ())), preferred_element_type=F32)
        e = jnp.exp(s - jnp.max(s, axis=-1, keepdims=True))
        den = jnp.sum(e, axis=-1, keepdims=True)
        o = _dot(e.astype(BF16), v_b) / den
        o_ref[b * n:(b + 1) * n, :] = jnp.concatenate(
            [o[h * n:(h + 1) * n, h * MEM_HEAD_DIM:(h + 1) * MEM_HEAD_DIM] for h in range(MEM_HEADS)], axis=1)


def _attn_sample(q, ck, cv, layer, *, tb=4):
    n = DEC_SEQ
    kv_spec = pl.BlockSpec((None, tb, N_MEM, D_MODEL), lambda i: (layer, i, 0, 0))
    row_spec = pl.BlockSpec((tb * n, D_MODEL), lambda i: (i, 0))
    return pl.pallas_call(
        functools.partial(_attn_sample_kernel, tb=tb),
        grid=(DEC_BATCH // tb,),
        in_specs=[row_spec, kv_spec, kv_spec],
        out_specs=row_spec,
        out_shape=jax.ShapeDtypeStruct((ROWS_S, D_MODEL), F32),
        compiler_params=_params(1),
        name="attn_sample",
    )(q, ck, cv)


def _norm_kernel(x_ref, g_ref, o_ref):
    o_ref[...] = _rms(x_ref[...], g_ref[...])


def _final_norm(x, g, *, row_start, n_rows, tm=512):
    r0 = row_start // tm
    return pl.pallas_call(
        _norm_kernel,
        grid=(n_rows // tm,),
        in_specs=[pl.BlockSpec((tm, D_MODEL), lambda i: (r0 + i, 0)),
                  pl.BlockSpec((1, D_MODEL), lambda i: (0, 0))],
        out_specs=pl.BlockSpec((tm, D_MODEL), lambda i: (i, 0)),
        out_shape=jax.ShapeDtypeStruct((n_rows, D_MODEL), F32),
        compiler_params=_params(1),
        name="final_norm",
    )(x, g)


def _row(v):
    return v.reshape(1, -1)


def _per_channel(v):
    return jnp.repeat(v, SSD_HEAD_DIM).reshape(1, D_INNER)


def _pad_lanes(v):
    return jnp.pad(v, (0, LANES - v.shape[0])).reshape(1, LANES)


def kernel(x_prompt, x_sample, mem_prompt, cache_mem_k, cache_mem_v, state_ssm, state_conv, state_pool, norm_ffn1, ffn1_w_gate, ffn1_w_up, ffn1_w_down, norm_mix, ssd_in_w, ssd_conv_w, ssd_conv_b, ssd_dt_bias, ssd_a_log, ssd_d, ssd_norm_w, ssd_out_w, pool_w, pool_scale, norm_cross, norm_mem, xa_wq, xa_wk, xa_wv, xa_wo, norm_ffn2, ffn2_w_gate, ffn2_w_up, ffn2_w_down, final_norm):
    bf = lambda w: w.astype(BF16)
    x = jnp.concatenate([x_prompt.reshape(ROWS_P, D_MODEL), x_sample.reshape(ROWS_S, D_MODEL)], axis=0)

    mk_p, mv_p = _mem_kv(mem_prompt.reshape(BATCH * N_MEM, D_MODEL), norm_mem.reshape(DEPTH, 1, D_MODEL),
                         bf(xa_wk), bf(xa_wv))
    mk_p = mk_p.reshape(DEPTH, BATCH, N_MEM, D_MODEL)
    mv_p = mv_p.reshape(DEPTH, BATCH, N_MEM, D_MODEL)
    ck = cache_mem_k.reshape(DEPTH, DEC_BATCH, N_MEM, D_MODEL)
    cv = cache_mem_v.reshape(DEPTH, DEC_BATCH, N_MEM, D_MODEL)
    st_s = state_ssm.reshape(-1, DEC_BATCH, D_INNER, D_STATE)
    conv0 = jnp.zeros((BATCH, CONV_W - 1, CONV_DIM), F32)
    ssm0 = jnp.zeros((BATCH, D_INNER, D_STATE), F32)
    pool0 = jnp.zeros((BATCH, POOL_BUF, D_MODEL), F32)

    ssm_p, conv_p, pool_p, ssm_s, conv_s, pool_s = [], [], [], [], [], []
    for i in range(DEPTH):
        j = i // 2
        x = _ffn(x, _row(norm_ffn1[i]), bf(ffn1_w_gate[i]), bf(ffn1_w_up[i]), bf(ffn1_w_down[i]))
        g_mix = _row(norm_mix[i])
        if i % 2 == 0:
            w_in = ssd_in_w[j]
            w_dt = w_in[:, DT_OFF:]
            w_p = bf(jnp.concatenate(
                [w_in[:, :DT_OFF], jnp.pad(w_dt, ((0, 0), (0, IN_PAD_P - DT_OFF - SSD_HEADS)))], axis=1))
            w_s = bf(jnp.concatenate([w_in[:, :DT_OFF], jnp.repeat(w_dt, SSD_HEAD_DIM, axis=1)], axis=1))
            cw, cb = ssd_conv_w[j], _row(ssd_conv_b[j])
            dsk, nw, w_out = _per_channel(ssd_d[j]), _row(ssd_norm_w[j]), bf(ssd_out_w[j])

            zx_p = _rms_matmul(x, g_mix, w_p, row_start=0, n_rows=ROWS_P, tm=1024, tn=768, name="in_proj_p")
            y_p, nc_p, ns_p = _ssd_prompt(zx_p, conv0, ssm0, cw, cb, _pad_lanes(ssd_dt_bias[j]),
                                          _pad_lanes(ssd_a_log[j]), dsk, nw)
            zx_s = _rms_matmul(x, g_mix, w_s, row_start=ROWS_P, n_rows=ROWS_S, tm=1024, tn=1792,
                               name="in_proj_s")
            y_s, nc_s, ns_s = _ssd_sample(zx_s, state_conv, st_s, j, cw, cb, _per_channel(ssd_dt_bias[j]),
                                          _per_channel(ssd_a_log[j]), dsk, nw)
            x = _matmul_residual(x, y_p, w_out, row_start=0, tm=1024, name="out_proj_p")
            x = _matmul_residual(x, y_s, w_out, row_start=ROWS_P, tm=1024, name="out_proj_s")
            conv_p.append(nc_p)
            conv_s.append(nc_s)
            ssm_p.append(ns_p.reshape(BATCH, SSD_HEADS, SSD_HEAD_DIM, D_STATE))
            ssm_s.append(ns_s.reshape(DEC_BATCH, SSD_HEADS, SSD_HEAD_DIM, D_STATE))
        else:
            pw, ps = bf(pool_w[j]), _row(pool_scale[j])
            x, np_p = _pool(x, g_mix, pool0, lambda tb: pl.BlockSpec((tb, POOL_BUF, D_MODEL), lambda b, l: (b, 0, 0)),
                            pw, ps, row_start=0, nb=BATCH, seq=SEQ, tb=1, tl=512, pos0=0, name="pool_p")
            x, np_s = _pool(x, g_mix, state_pool,
                            lambda tb: pl.BlockSpec((None, tb, POOL_BUF, D_MODEL), lambda b, l: (j, b, 0, 0)),
                            pw, ps, row_start=ROWS_P, nb=DEC_BATCH, seq=DEC_SEQ, tb=8, tl=DEC_SEQ,
                            pos0=PAST_LEN, name="pool_s")
            pool_p.append(np_p)
            pool_s.append(np_s)

        g_x = _row(norm_cross[i])
        wq, wo = bf(xa_wq[i]), bf(xa_wo[i])
        x = _xattn_prompt(x, g_x, wq, mk_p, mv_p, wo, i)
        q_s = _rms_matmul(x, g_x, wq, row_start=ROWS_P, n_rows=ROWS_S, tm=1024, tn=1024, name="q_proj_s")
        o_s = _attn_sample(q_s, ck, cv, i)
        x = _matmul_residual(x, o_s, wo, row_start=ROWS_P, tm=1024, name="wo_s")

        x = _ffn(x, _row(norm_ffn2[i]), bf(ffn2_w_gate[i]), bf(ffn2_w_up[i]), bf(ffn2_w_down[i]))

    g_f = _row(final_norm)
    y_prompt = _final_norm(x, g_f, row_start=0, n_rows=ROWS_P).reshape(BATCH, SEQ, D_MODEL)
    y_sample = _final_norm(x, g_f, row_start=ROWS_P, n_rows=ROWS_S).reshape(DEC_BATCH, DEC_SEQ, D_MODEL)
    kv_shape = (DEPTH, BATCH, N_MEM, MEM_HEADS, MEM_HEAD_DIM)
    return (y_prompt, y_sample, jnp.stack(ssm_p), jnp.stack(conv_p), jnp.stack(pool_p),
            mk_p.reshape(kv_shape), mv_p.reshape(kv_shape),
            jnp.stack(ssm_s), jnp.stack(conv_s), jnp.stack(pool_s))
```

```python
import functools

import jax
import jax.numpy as jnp
from jax import lax
from jax.experimental import pallas as pl
from jax.experimental.pallas import tpu as pltpu

F32 = jnp.float32
BF16 = jnp.bfloat16

D_MODEL = 1024
BATCH = 8
SEQ = 2048
DEPTH = 4
DEC_BATCH = 128
DEC_SEQ = 8
PAST_LEN = 16384
D_FF = 2816
D_INNER = 2048
SSD_HEAD_DIM = 64
SSD_HEADS = 32
SSD_GROUPS = 4
HEADS_PER_GROUP = 8
D_STATE = 128
CONV_W = 4
GN = SSD_GROUPS * D_STATE
CONV_DIM = D_INNER + 2 * GN
SSD_CHUNK = 128
POOL_WINDOWS = (2, 4, 8, 16)
POOL_GW = 256
POOL_BUF = 15
N_MEM = 256
MEM_HEADS = 4
MEM_HEAD_DIM = 256
EPS = 1e-5

ROWS_P = BATCH * SEQ
ROWS_S = DEC_BATCH * DEC_SEQ
ROWS = ROWS_P + ROWS_S

LANES = 128
XBC_OFF = D_INNER
DT_OFF = D_INNER + CONV_DIM
IN_PAD_P = 5376
IN_PAD_S = DT_OFF + D_INNER
VMEM_LIMIT = 48 * 1024 * 1024
MXU_TILE = 256
FFN_CHUNK = MXU_TILE
MEM_CHUNKS = D_MODEL // LANES


def _params(n_axes, vmem=VMEM_LIMIT):
    return pltpu.CompilerParams(dimension_semantics=("arbitrary",) * n_axes,
                                vmem_limit_bytes=vmem)


def _rms(x, g):
    r = lax.rsqrt(jnp.mean(x * x, axis=-1, keepdims=True) + EPS)
    return x * r * g


def _silu(x):
    return x * (1.0 / (1.0 + jnp.exp(-x)))


def _softplus(x):
    return jnp.maximum(x, 0.0) + jnp.log1p(jnp.exp(-jnp.abs(x)))


def _dot(a, b):
    return jnp.dot(a, b, preferred_element_type=F32)


def _ffn_kernel(x_ref, g_ref, wg_ref, wu_ref, wd_ref, o_ref, h_ref):
    x = x_ref[...]
    u = _rms(x, g_ref[...]).astype(BF16)
    for lo in range(0, D_FF, FFN_CHUNK):
        a = _dot(u, wg_ref[:, lo:lo + FFN_CHUNK])
        b = _dot(u, wu_ref[:, lo:lo + FFN_CHUNK])
        h_ref[:, lo:lo + FFN_CHUNK] = (_silu(a) * b).astype(BF16)
    o_ref[...] = x + 0.5 * _dot(h_ref[...], wd_ref[...])


def _resident(shape):
    return pl.BlockSpec(shape, lambda *_: (0,) * len(shape), pipeline_mode=pl.Buffered(1))


def _ffn(x, g, wg, wu, wd, *, tm=1024):
    rows = x.shape[0]
    row_spec = pl.BlockSpec((tm, D_MODEL), lambda i: (i, 0))
    return pl.pallas_call(
        _ffn_kernel,
        grid=(rows // tm,),
        in_specs=[row_spec, _resident((1, D_MODEL)), _resident((D_MODEL, D_FF)), _resident((D_MODEL, D_FF)),
                  _resident((D_FF, D_MODEL))],
        out_specs=row_spec,
        out_shape=jax.ShapeDtypeStruct((rows, D_MODEL), F32),
        scratch_shapes=[pltpu.VMEM((tm, D_FF), BF16)],
        compiler_params=_params(1),
        name="ffn",
    )(x, g, wg, wu, wd)


def _rms_mm_kernel(x_ref, g_ref, w_ref, o_ref, u_ref):
    @pl.when(pl.program_id(1) == 0)
    def _():
        u_ref[...] = _rms(x_ref[...], g_ref[...]).astype(BF16)

    o_ref[...] = _dot(u_ref[...], w_ref[...])


def _rms_matmul(x, g, w, *, row_start, n_rows, tm, tn, name):
    n = w.shape[1]
    r0 = row_start // tm
    return pl.pallas_call(
        _rms_mm_kernel,
        grid=(n_rows // tm, n // tn),
        in_specs=[
            pl.BlockSpec((tm, D_MODEL), lambda i, j: (r0 + i, 0)),
            pl.BlockSpec((1, D_MODEL), lambda i, j: (0, 0)),
            pl.BlockSpec((D_MODEL, tn), lambda i, j: (0, j)),
        ],
        out_specs=pl.BlockSpec((tm, tn), lambda i, j: (i, j)),
        out_shape=jax.ShapeDtypeStruct((n_rows, n), F32),
        scratch_shapes=[pltpu.VMEM((tm, D_MODEL), BF16)],
        compiler_params=_params(2),
        name=name,
    )(x, g, w)


def _mem_kv_kernel(m_ref, g_ref, wk_ref, wv_ref, k_ref, v_ref, *, tm):
    u = _rms(m_ref[...], g_ref[...]).astype(BF16)
    for w_ref, o_ref in ((wk_ref, k_ref), (wv_ref, v_ref)):
        kv = _dot(u, w_ref[...])
        for j in range(MEM_CHUNKS):
            o_ref[pl.ds(j, tm, stride=MEM_CHUNKS), :] = kv[:, _chunk_to_col(j):_chunk_to_col(j) + LANES]


def _mem_kv(mem, g, wk, wv, *, tm=512):
    rows = mem.shape[0]
    w_spec = pl.BlockSpec((None, D_MODEL, D_MODEL), lambda l, r: (l, 0, 0))
    o_spec = pl.BlockSpec((None, tm * MEM_CHUNKS, LANES), lambda l, r: (l, r, 0))
    o_shape = jax.ShapeDtypeStruct((DEPTH, rows * MEM_CHUNKS, LANES), F32)
    return pl.pallas_call(
        functools.partial(_mem_kv_kernel, tm=tm),
        grid=(DEPTH, rows // tm),
        in_specs=[
            pl.BlockSpec((tm, D_MODEL), lambda l, r: (r, 0)),
            pl.BlockSpec((None, 1, D_MODEL), lambda l, r: (l, 0, 0)),
            w_spec, w_spec,
        ],
        out_specs=[o_spec, o_spec],
        out_shape=[o_shape, o_shape],
        compiler_params=_params(2),
        name="mem_kv",
    )(mem, g, wk, wv)


def _mm_res_kernel(x_ref, y_ref, w_ref, o_ref):
    o_ref[...] = x_ref[...] + _dot(y_ref[...].astype(BF16), w_ref[...])


def _matmul_residual(x, y, w, *, row_start, tm, name):
    n_rows, k = y.shape
    r0 = row_start // tm
    return pl.pallas_call(
        _mm_res_kernel,
        grid=(n_rows // tm,),
        in_specs=[
            pl.BlockSpec((tm, D_MODEL), lambda i: (r0 + i, 0)),
            pl.BlockSpec((tm, k), lambda i: (i, 0)),
            pl.BlockSpec((k, D_MODEL), lambda i: (0, 0)),
        ],
        out_specs=pl.BlockSpec((tm, D_MODEL), lambda i: (r0 + i, 0)),
        out_shape=jax.ShapeDtypeStruct(x.shape, F32),
        input_output_aliases={0: 0},
        compiler_params=_params(1),
        name=name,
    )(x, y, w)


def _ssd_prompt_kernel(zx_ref, cbuf_ref, st_ref, cw_ref, cb_ref, dtb_ref, alog_ref, dsk_ref, nw_ref,
                       y_ref, nconv_ref, nst_ref, xpad_ref, ht_ref, ybuf_ref, *, nc):
    c = pl.program_id(1)
    q = SSD_CHUNK
    lead = 8
    tail0 = lead - (CONV_W - 1)

    @pl.when(c == 0)
    def _():
        xpad_ref[tail0:lead, :] = cbuf_ref[...]
        ht_ref[...] = st_ref[...].T

    xpad_ref[lead:lead + q, :] = zx_ref[:, XBC_OFF:DT_OFF]
    conv = cw_ref[0:1, :] * xpad_ref[tail0:tail0 + q, :]
    for k in range(1, CONV_W):
        conv = conv + cw_ref[k:k + 1, :] * xpad_ref[tail0 + k:tail0 + k + q, :]
    conv = cb_ref[...] + conv
    tail = xpad_ref[q + tail0:q + lead, :]
    xpad_ref[tail0:lead, :] = tail

    @pl.when(c == nc - 1)
    def _():
        nconv_ref[...] = tail

    xbc = _silu(conv)
    xs = xbc[:, :D_INNER]
    bm = xbc[:, D_INNER:D_INNER + GN]
    cm = xbc[:, D_INNER + GN:]

    dt = _softplus(zx_ref[:, DT_OFF:DT_OFF + LANES] + dtb_ref[...])
    a = -jnp.exp(alog_ref[...])
    da = dt * a
    rows = lax.broadcasted_iota(jnp.int32, (q, q), 0)
    cols = lax.broadcasted_iota(jnp.int32, (q, q), 1)
    causal = rows >= cols
    tri = jnp.where(causal, 1.0, 0.0).astype(F32)
    acs = jnp.dot(tri, da, precision=lax.Precision.HIGHEST, preferred_element_type=F32)
    acs_t = acs.T
    dt_t = dt.T
    eacs = jnp.exp(acs)
    last = acs_t[:, q - 1:q]
    w_t = jnp.exp(last - acs_t) * dt_t
    cd_t = jnp.exp(last)
    low_half = lax.broadcasted_iota(jnp.int32, (1, LANES), 1) < SSD_HEAD_DIM

    for g in range(SSD_GROUPS):
        bm_g = bm[:, g * D_STATE:(g + 1) * D_STATE]
        cm_g = cm[:, g * D_STATE:(g + 1) * D_STATE]
        bm_gt = bm_g.T
        cb_g = _dot(cm_g.astype(BF16), bm_gt.astype(BF16))
        for hh in range(0, HEADS_PER_GROUP, 2):
            h0 = g * HEADS_PER_GROUP + hh
            lo = (h0 // 2) * LANES
            x_pair = xs[:, lo:lo + LANES]
            ht_pair = ht_ref[:, lo:lo + LANES]
            x_pair_b = x_pair.astype(BF16)
            rhs = jnp.concatenate([x_pair_b, ht_pair.astype(BF16)], axis=0)
            ys, sts = [], []
            for h in (h0, h0 + 1):
                seg = acs[:, h:h + 1] - acs_t[h:h + 1, :]
                decay = jnp.exp(jnp.where(causal, seg, -jnp.inf))
                m_h = cb_g * decay * dt_t[h:h + 1, :]
                e_h = eacs[:, h:h + 1] * cm_g
                lhs = jnp.concatenate([m_h.astype(BF16), e_h.astype(BF16)], axis=1)
                ys.append(_dot(lhs, rhs))
                b_h = bm_gt * w_t[h:h + 1, :]
                sts.append(_dot(b_h.astype(BF16), x_pair_b))
            y_pair = jnp.where(low_half, ys[0], ys[1]) + x_pair * dsk_ref[:, lo:lo + LANES]
            st_pair = jnp.where(low_half, sts[0], sts[1])
            cd_row = jnp.where(low_half, cd_t[h0:h0 + 1, :], cd_t[h0 + 1:h0 + 2, :])
            ht_ref[:, lo:lo + LANES] = ht_pair * cd_row + st_pair
            ybuf_ref[:, lo:lo + LANES] = y_pair

    y = ybuf_ref[...] * _silu(zx_ref[:, :D_INNER])
    gw = D_INNER // SSD_GROUPS
    for g in range(SSD_GROUPS):
        yg = y[:, g * gw:(g + 1) * gw]
        yg = yg * lax.rsqrt(jnp.mean(yg * yg, axis=-1, keepdims=True) + EPS)
        y_ref[:, g * gw:(g + 1) * gw] = (yg * nw_ref[:, g * gw:(g + 1) * gw]).astype(BF16)

    @pl.when(c == nc - 1)
    def _():
        nst_ref[...] = ht_ref[...].T


def _ssd_prompt(zx, cbuf, st, cw, cb, dtb, alog, dsk, nw):
    nb = cbuf.shape[0]
    nc = SEQ // SSD_CHUNK
    q = SSD_CHUNK
    vec = lambda n: pl.BlockSpec((1, n), lambda b, c: (0, 0))
    return pl.pallas_call(
        functools.partial(_ssd_prompt_kernel, nc=nc),
        grid=(nb, nc),
        in_specs=[
            pl.BlockSpec((q, IN_PAD_P), lambda b, c: (b * nc + c, 0)),
            pl.BlockSpec((None, CONV_W - 1, CONV_DIM), lambda b, c: (b, 0, 0)),
            pl.BlockSpec((None, D_INNER, D_STATE), lambda b, c: (b, 0, 0)),
            pl.BlockSpec((CONV_W, CONV_DIM), lambda b, c: (0, 0)),
            vec(CONV_DIM), vec(LANES), vec(LANES), vec(D_INNER), vec(D_INNER),
        ],
        out_specs=[
            pl.BlockSpec((q, D_INNER), lambda b, c: (b * nc + c, 0)),
            pl.BlockSpec((None, CONV_W - 1, CONV_DIM), lambda b, c: (b, 0, 0)),
            pl.BlockSpec((None, D_INNER, D_STATE), lambda b, c: (b, 0, 0)),
        ],
        out_shape=[
            jax.ShapeDtypeStruct((nb * SEQ, D_INNER), BF16),
            jax.ShapeDtypeStruct((nb, CONV_W - 1, CONV_DIM), F32),
            jax.ShapeDtypeStruct((nb, D_INNER, D_STATE), F32),
        ],
        scratch_shapes=[
            pltpu.VMEM((q + 8, CONV_DIM), F32),
            pltpu.VMEM((D_STATE, D_INNER), F32),
            pltpu.VMEM((q, D_INNER), F32),
        ],
        compiler_params=_params(2),
        name="ssd_prompt",
    )(zx, cbuf, st, cw, cb, dtb, alog, dsk, nw)


def _ssd_sample_kernel(zx_ref, cbuf_ref, st_ref, cw_ref, cb_ref, dtb_ref, alog_ref, dsk_ref, nw_ref,
                       y_ref, nconv_ref, nst_ref, xpad_ref):
    q = DEC_SEQ
    lead = 8
    tail0 = lead - (CONV_W - 1)
    xpad_ref[tail0:lead, :] = cbuf_ref[...]
    xpad_ref[lead:lead + q, :] = zx_ref[:, XBC_OFF:DT_OFF]
    conv = cw_ref[0:1, :] * xpad_ref[tail0:tail0 + q, :]
    for k in range(1, CONV_W):
        conv = conv + cw_ref[k:k + 1, :] * xpad_ref[tail0 + k:tail0 + k + q, :]
    conv = cb_ref[...] + conv
    nconv_ref[...] = xpad_ref[q + tail0:q + lead, :]

    xbc = _silu(conv)
    xs = xbc[:, :D_INNER]
    bm = xbc[:, D_INNER:D_INNER + GN]
    cm = xbc[:, D_INNER + GN:]

    dt = _softplus(zx_ref[:, DT_OFF:DT_OFF + D_INNER] + dtb_ref[...])
    a = -jnp.exp(alog_ref[...])
    da = dt * a
    rows = lax.broadcasted_iota(jnp.int32, (q, 1), 0)
    acs = jnp.where(rows >= 0, da[0:1, :], 0.0)
    for k in range(1, q):
        acs = acs + jnp.where(rows >= k, da[k:k + 1, :], 0.0)
    xdt = xs * dt
    gw = D_INNER // SSD_GROUPS

    y = jnp.zeros((q, D_INNER), F32)
    for k in range(q):
        decay = jnp.exp(jnp.where(rows >= k, acs - acs[k:k + 1, :], -jnp.inf))
        cb_k = jnp.concatenate(
            [jnp.broadcast_to(
                jnp.sum(cm[:, g * D_STATE:(g + 1) * D_STATE] * bm[k:k + 1, g * D_STATE:(g + 1) * D_STATE],
                        axis=-1, keepdims=True), (q, gw)) for g in range(SSD_GROUPS)], axis=1)
        y = y + cb_k * decay * xdt[k:k + 1, :]

    h0 = st_ref[...]
    cm_rows = jnp.concatenate([cm[:, g * D_STATE:(g + 1) * D_STATE] for g in range(SSD_GROUPS)], axis=0)
    r = lax.dot_general(cm_rows.astype(BF16), h0.astype(BF16), (((1,), (1,)), ((), ())),
                        preferred_element_type=F32)
    y_off = jnp.concatenate([r[g * q:(g + 1) * q, g * gw:(g + 1) * gw] for g in range(SSD_GROUPS)], axis=1)
    y = y + jnp.exp(acs) * y_off
    y = y + xs * dsk_ref[...]

    y = y * _silu(zx_ref[:, :D_INNER])
    for g in range(SSD_GROUPS):
        yg = y[:, g * gw:(g + 1) * gw]
        yg = yg * lax.rsqrt(jnp.mean(yg * yg, axis=-1, keepdims=True) + EPS)
        y_ref[:, g * gw:(g + 1) * gw] = yg * nw_ref[:, g * gw:(g + 1) * gw]

    last = acs[q - 1:q, :]
    xw = xdt * jnp.exp(last - acs)
    cd = jnp.broadcast_to(jnp.exp(last), (q, D_INNER))
    padded = jnp.concatenate([xw, cd, jnp.zeros((LANES - 2 * q, D_INNER), F32)], axis=0)
    padded_t = padded.T
    cd_col = padded_t[:, q:q + 1]
    zeros_k = jnp.zeros((LANES - q, D_STATE), F32)
    for g in range(SSD_GROUPS):
        bm_pad = jnp.concatenate([bm[:, g * D_STATE:(g + 1) * D_STATE], zeros_k], axis=0)
        st_g = _dot(padded_t[g * gw:(g + 1) * gw, :].astype(BF16), bm_pad.astype(BF16))
        nst_ref[g * gw:(g + 1) * gw, :] = h0[g * gw:(g + 1) * gw, :] * cd_col[g * gw:(g + 1) * gw, :] + st_g


def _ssd_sample(zx, cbuf, st, layer, cw, cb, dtb, alog, dsk, nw):
    q = DEC_SEQ
    vec = lambda n: pl.BlockSpec((1, n), lambda b: (0, 0))
    return pl.pallas_call(
        _ssd_sample_kernel,
        grid=(DEC_BATCH,),
        in_specs=[
            pl.BlockSpec((q, IN_PAD_S), lambda b: (b, 0)),
            pl.BlockSpec((None, None, CONV_W - 1, CONV_DIM), lambda b: (layer, b, 0, 0)),
            pl.BlockSpec((None, None, D_INNER, D_STATE), lambda b: (layer, b, 0, 0)),
            pl.BlockSpec((CONV_W, CONV_DIM), lambda b: (0, 0)),
            vec(CONV_DIM), vec(D_INNER), vec(D_INNER), vec(D_INNER), vec(D_INNER),
        ],
        out_specs=[
            pl.BlockSpec((q, D_INNER), lambda b: (b, 0)),
            pl.BlockSpec((None, CONV_W - 1, CONV_DIM), lambda b: (b, 0, 0)),
            pl.BlockSpec((None, D_INNER, D_STATE), lambda b: (b, 0, 0)),
        ],
        out_shape=[
            jax.ShapeDtypeStruct((ROWS_S, D_INNER), F32),
            jax.ShapeDtypeStruct((DEC_BATCH, CONV_W - 1, CONV_DIM), F32),
            jax.ShapeDtypeStruct((DEC_BATCH, D_INNER, D_STATE), F32),
        ],
        scratch_shapes=[pltpu.VMEM((q + 8, CONV_DIM), F32)],
        compiler_params=_params(1),
        name="ssd_sample",
    )(zx, cbuf, st, cw, cb, dtb, alog, dsk, nw)


def _pool_kernel(x_ref, g_ref, buf_ref, pw_ref, ps_ref, o_ref, np_ref, xx_ref, *, tb, tl, nl, pos0):
    l = pl.program_id(1)
    x = x_ref[...]
    u = _rms(x, g_ref[...]).reshape(tb, tl, D_MODEL)
    lead = POOL_BUF + 1

    @pl.when(l == 0)
    def _():
        xx_ref[:, 1:lead, :] = buf_ref[...]

    xx_ref[:, lead:lead + tl, :] = u
    pos = (pos0 + l * tl + lax.broadcasted_iota(jnp.int32, (1, tl, 1), 1)).astype(F32)
    outs = []
    for gi, w in enumerate(POOL_WINDOWS):
        lo = gi * POOL_GW
        s = xx_ref[:, lead:lead + tl, lo:lo + POOL_GW]
        for j in range(1, w):
            s = s + xx_ref[:, lead - j:lead - j + tl, lo:lo + POOL_GW]
        cnt = jnp.minimum(pos + 1.0, float(w))
        mix = s / cnt - u[:, :, lo:lo + POOL_GW]
        outs.append(_dot(mix.reshape(tb * tl, POOL_GW).astype(BF16), pw_ref[gi]))
    o_ref[...] = x + jnp.concatenate(outs, axis=-1) * ps_ref[...]

    tail = xx_ref[:, tl + 1:tl + lead, :]
    if nl > 1:
        xx_ref[:, 1:lead, :] = tail

    @pl.when(l == nl - 1)
    def _():
        np_ref[...] = tail


def _pool(x, g, buf_arr, buf_index, pw, ps, *, row_start, nb, seq, tb, tl, pos0, name):
    nl = seq // tl
    r0 = row_start // (tb * tl)
    x_spec = pl.BlockSpec((tb * tl, D_MODEL), lambda b, l: (r0 + b * nl + l, 0))
    return pl.pallas_call(
        functools.partial(_pool_kernel, tb=tb, tl=tl, nl=nl, pos0=pos0),
        grid=(nb // tb, nl),
        in_specs=[
            x_spec,
            pl.BlockSpec((1, D_MODEL), lambda b, l: (0, 0)),
            buf_index(tb),
            pl.BlockSpec((len(POOL_WINDOWS), POOL_GW, POOL_GW), lambda b, l: (0, 0, 0)),
            pl.BlockSpec((1, D_MODEL), lambda b, l: (0, 0)),
        ],
        out_specs=[x_spec, pl.BlockSpec((tb, POOL_BUF, D_MODEL), lambda b, l: (b, 0, 0))],
        out_shape=[jax.ShapeDtypeStruct(x.shape, F32),
                   jax.ShapeDtypeStruct((nb, POOL_BUF, D_MODEL), F32)],
        scratch_shapes=[pltpu.VMEM((tb, tl + POOL_BUF + 1, D_MODEL), F32)],
        input_output_aliases={0: 0},
        compiler_params=_params(2),
        name=name,
    )(x, g, buf_arr, pw, ps)


def _head_kv(ref, h):
    return jnp.concatenate([ref[pl.ds(c * MEM_HEADS + h, N_MEM, stride=MEM_CHUNKS), :]
                            for c in range(MEM_HEAD_DIM // LANES)], axis=1).astype(BF16)


def _xattn_prompt_kernel(x_ref, g_ref, wq_ref, k_ref, v_ref, wo_ref, o_ref):
    x = x_ref[...]
    u = _rms(x, g_ref[...]).astype(BF16)
    q = (_dot(u, wq_ref[...]) * (MEM_HEAD_DIM ** -0.5)).astype(BF16)
    outs = []
    for h in range(MEM_HEADS):
        q_h = q[:, h * MEM_HEAD_DIM:(h + 1) * MEM_HEAD_DIM]
        s = lax.dot_general(q_h, _head_kv(k_ref, h), (((1,), (1,)), ((), ())), preferred_element_type=F32)
        e = jnp.exp(s - jnp.max(s, axis=-1, keepdims=True))
        den = jnp.sum(e, axis=-1, keepdims=True)
        outs.append(_dot(e.astype(BF16), _head_kv(v_ref, h)) / den)
    o = jnp.concatenate(outs, axis=-1)
    o_ref[...] = x + _dot(o.astype(BF16), wo_ref[...])


def _xattn_prompt(x, g, wq, mk, mv, wo, layer, *, tl=512):
    nl = SEQ // tl
    x_spec = pl.BlockSpec((tl, D_MODEL), lambda b, l: (b * nl + l, 0))
    w_spec = pl.BlockSpec((D_MODEL, D_MODEL), lambda b, l: (0, 0))
    kv_spec = pl.BlockSpec((None, N_MEM * MEM_CHUNKS, LANES), lambda b, l: (layer, b, 0))
    return pl.pallas_call(
        _xattn_prompt_kernel,
        grid=(BATCH, nl),
        in_specs=[x_spec, pl.BlockSpec((1, D_MODEL), lambda b, l: (0, 0)), w_spec, kv_spec, kv_spec, w_spec],
        out_specs=x_spec,
        out_shape=jax.ShapeDtypeStruct(x.shape, F32),
        input_output_aliases={0: 0},
        compiler_params=_params(2),
        name="xattn_prompt",
    )(x, g, wq, mk, mv, wo)


def _chunk_rows(t):
    lead = t.shape[:-3]
    t = t.reshape(*lead, N_MEM, MEM_HEADS, MEM_HEAD_DIM // LANES, LANES)
    return jnp.swapaxes(t, -2, -3).reshape(*lead, N_MEM * MEM_CHUNKS, LANES)


def _from_chunk_rows(t):
    t = t.reshape(DEPTH, BATCH, N_MEM, MEM_HEAD_DIM // LANES, MEM_HEADS, LANES)
    return jnp.swapaxes(t, -2, -3).reshape(DEPTH, BATCH, N_MEM, MEM_HEADS, MEM_HEAD_DIM)


def _chunk_to_col(j):
    return ((j % MEM_HEADS) * (MEM_HEAD_DIM // LANES) + j // MEM_HEADS) * LANES


def _attn_sample_kernel(q_ref, k_ref, v_ref, o_ref, *, tb):
    n = DEC_SEQ
    rows = lax.broadcasted_iota(jnp.int32, (MEM_HEADS * n, D_MODEL), 0) // n
    chunk = lax.broadcasted_iota(jnp.int32, (MEM_HEADS * n, D_MODEL), 1) // LANES
    own = rows == chunk % MEM_HEADS

    def gather(ref, b):
        return jnp.concatenate([ref[b, pl.ds(j, N_MEM, stride=MEM_CHUNKS), :] for j in range(MEM_CHUNKS)],
                               axis=1).astype(BF16)

    for b in range(tb):
        q_b = q_ref[b * n:(b + 1) * n, :] * (MEM_HEAD_DIM ** -0.5)
        q_c = jnp.concatenate([q_b[:, _chunk_to_col(j):_chunk_to_col(j) + LANES] for j in range(MEM_CHUNKS)], axis=1)
        q_blk = jnp.where(own, jnp.concatenate([q_c] * MEM_HEADS, axis=0), 0.0).astype(BF16)
        s = lax.dot_general(q_blk, gather(k_ref, b), (((1,), (1,)), ((), ())), preferred_element_type=F32)
        e = jnp.exp(s - jnp.max(s, axis=-1, keepdims=True))
        den = jnp.sum(e, axis=-1, keepdims=True)
        o = _dot(e.astype(BF16), gather(v_ref, b)) / den
        for j in range(MEM_CHUNKS):
            h = j % MEM_HEADS
            o_ref[b * n:(b + 1) * n, _chunk_to_col(j):_chunk_to_col(j) + LANES] = (
                o[h * n:(h + 1) * n, j * LANES:(j + 1) * LANES])


def _attn_sample(q, ck, cv, layer, *, tb=4):
    n = DEC_SEQ
    kv_spec = pl.BlockSpec((None, tb, N_MEM * MEM_CHUNKS, LANES), lambda i: (layer, i, 0, 0))
    row_spec = pl.BlockSpec((tb * n, D_MODEL), lambda i: (i, 0))
    return pl.pallas_call(
        functools.partial(_attn_sample_kernel, tb=tb),
        grid=(DEC_BATCH // tb,),
        in_specs=[row_spec, kv_spec, kv_spec],
        out_specs=row_spec,
        out_shape=jax.ShapeDtypeStruct((ROWS_S, D_MODEL), F32),
        compiler_params=_params(1),
        name="attn_sample",
    )(q, ck, cv)


def _norm_kernel(x_ref, g_ref, o_ref):
    o_ref[...] = _rms(x_ref[...], g_ref[...])


def _final_norm(x, g, *, row_start, n_rows, tm=512):
    r0 = row_start // tm
    return pl.pallas_call(
        _norm_kernel,
        grid=(n_rows // tm,),
        in_specs=[pl.BlockSpec((tm, D_MODEL), lambda i: (r0 + i, 0)),
                  pl.BlockSpec((1, D_MODEL), lambda i: (0, 0))],
        out_specs=pl.BlockSpec((tm, D_MODEL), lambda i: (i, 0)),
        out_shape=jax.ShapeDtypeStruct((n_rows, D_MODEL), F32),
        compiler_params=_params(1),
        name="final_norm",
    )(x, g)


def _row(v):
    return v.reshape(1, -1)


def _per_channel(v):
    return jnp.repeat(v, SSD_HEAD_DIM).reshape(1, D_INNER)


def _pad_lanes(v):
    return jnp.pad(v, (0, LANES - v.shape[0])).reshape(1, LANES)


def kernel(x_prompt, x_sample, mem_prompt, cache_mem_k, cache_mem_v, state_ssm, state_conv, state_pool, norm_ffn1, ffn1_w_gate, ffn1_w_up, ffn1_w_down, norm_mix, ssd_in_w, ssd_conv_w, ssd_conv_b, ssd_dt_bias, ssd_a_log, ssd_d, ssd_norm_w, ssd_out_w, pool_w, pool_scale, norm_cross, norm_mem, xa_wq, xa_wk, xa_wv, xa_wo, norm_ffn2, ffn2_w_gate, ffn2_w_up, ffn2_w_down, final_norm):
    bf = lambda w: w.astype(BF16)
    x = jnp.concatenate([x_prompt.reshape(ROWS_P, D_MODEL), x_sample.reshape(ROWS_S, D_MODEL)], axis=0)

    mk_p, mv_p = _mem_kv(mem_prompt.reshape(BATCH * N_MEM, D_MODEL), norm_mem.reshape(DEPTH, 1, D_MODEL),
                         bf(xa_wk), bf(xa_wv))
    ck = _chunk_rows(cache_mem_k)
    cv = _chunk_rows(cache_mem_v)
    st_s = state_ssm.reshape(-1, DEC_BATCH, D_INNER, D_STATE)
    conv0 = jnp.zeros((BATCH, CONV_W - 1, CONV_DIM), F32)
    ssm0 = jnp.zeros((BATCH, D_INNER, D_STATE), F32)
    pool0 = jnp.zeros((BATCH, POOL_BUF, D_MODEL), F32)

    ssm_p, conv_p, pool_p, ssm_s, conv_s, pool_s = [], [], [], [], [], []
    for i in range(DEPTH):
        j = i // 2
        x = _ffn(x, _row(norm_ffn1[i]), bf(ffn1_w_gate[i]), bf(ffn1_w_up[i]), bf(ffn1_w_down[i]))
        g_mix = _row(norm_mix[i])
        if i % 2 == 0:
            w_in = ssd_in_w[j]
            w_dt = w_in[:, DT_OFF:]
            w_p = bf(jnp.concatenate(
                [w_in[:, :DT_OFF], jnp.pad(w_dt, ((0, 0), (0, IN_PAD_P - DT_OFF - SSD_HEADS)))], axis=1))
            w_s = bf(jnp.concatenate([w_in[:, :DT_OFF], jnp.repeat(w_dt, SSD_HEAD_DIM, axis=1)], axis=1))
            cw, cb = ssd_conv_w[j], _row(ssd_conv_b[j])
            dsk, nw, w_out = _per_channel(ssd_d[j]), _row(ssd_norm_w[j]), bf(ssd_out_w[j])

            zx_p = _rms_matmul(x, g_mix, w_p, row_start=0, n_rows=ROWS_P, tm=1024, tn=768, name="in_proj_p")
            y_p, nc_p, ns_p = _ssd_prompt(zx_p, conv0, ssm0, cw, cb, _pad_lanes(ssd_dt_bias[j]),
                                          _pad_lanes(ssd_a_log[j]), dsk, nw)
            zx_s = _rms_matmul(x, g_mix, w_s, row_start=ROWS_P, n_rows=ROWS_S, tm=1024, tn=1792,
                               name="in_proj_s")
            y_s, nc_s, ns_s = _ssd_sample(zx_s, state_conv, st_s, j, cw, cb, _per_channel(ssd_dt_bias[j]),
                                          _per_channel(ssd_a_log[j]), dsk, nw)
            x = _matmul_residual(x, y_p, w_out, row_start=0, tm=1024, name="out_proj_p")
            x = _matmul_residual(x, y_s, w_out, row_start=ROWS_P, tm=1024, name="out_proj_s")
            conv_p.append(nc_p)
            conv_s.append(nc_s)
            ssm_p.append(ns_p.reshape(BATCH, SSD_HEADS, SSD_HEAD_DIM, D_STATE))
            ssm_s.append(ns_s.reshape(DEC_BATCH, SSD_HEADS, SSD_HEAD_DIM, D_STATE))
        else:
            pw, ps = bf(pool_w[j]), _row(pool_scale[j])
            x, np_p = _pool(x, g_mix, pool0, lambda tb: pl.BlockSpec((tb, POOL_BUF, D_MODEL), lambda b, l: (b, 0, 0)),
                            pw, ps, row_start=0, nb=BATCH, seq=SEQ, tb=1, tl=512, pos0=0, name="pool_p")
            x, np_s = _pool(x, g_mix, state_pool,
                            lambda tb: pl.BlockSpec((None, tb, POOL_BUF, D_MODEL), lambda b, l: (j, b, 0, 0)),
                            pw, ps, row_start=ROWS_P, nb=DEC_BATCH, seq=DEC_SEQ, tb=8, tl=DEC_SEQ,
                            pos0=PAST_LEN, name="pool_s")
            pool_p.append(np_p)
            pool_s.append(np_s)

        g_x = _row(norm_cross[i])
        wq, wo = bf(xa_wq[i]), bf(xa_wo[i])
        x = _xattn_prompt(x, g_x, wq, mk_p, mv_p, wo, i)
        q_s = _rms_matmul(x, g_x, wq, row_start=ROWS_P, n_rows=ROWS_S, tm=1024, tn=1024, name="q_proj_s")
        o_s = _attn_sample(q_s, ck, cv, i)
        x = _matmul_residual(x, o_s, wo, row_start=ROWS_P, tm=1024, name="wo_s")

        x = _ffn(x, _row(norm_ffn2[i]), bf(ffn2_w_gate[i]), bf(ffn2_w_up[i]), bf(ffn2_w_down[i]))

    g_f = _row(final_norm)
    y_prompt = _final_norm(x, g_f, row_start=0, n_rows=ROWS_P).reshape(BATCH, SEQ, D_MODEL)
    y_sample = _final_norm(x, g_f, row_start=ROWS_P, n_rows=ROWS_S).reshape(DEC_BATCH, DEC_SEQ, D_MODEL)
    return (y_prompt, y_sample, jnp.stack(ssm_p), jnp.stack(conv_p), jnp.stack(pool_p),
            _from_chunk_rows(mk_p), _from_chunk_rows(mv_p),
            jnp.stack(ssm_s), jnp.stack(conv_s), jnp.stack(pool_s))
```

```python
import functools

import jax
import jax.numpy as jnp
from jax import lax
from jax.experimental import pallas as pl
from jax.experimental.pallas import tpu as pltpu

F32 = jnp.float32
BF16 = jnp.bfloat16

D_MODEL = 1024
BATCH = 8
SEQ = 2048
DEPTH = 4
DEC_BATCH = 128
DEC_SEQ = 8
PAST_LEN = 16384
D_FF = 2816
D_INNER = 2048
SSD_HEAD_DIM = 64
SSD_HEADS = 32
SSD_GROUPS = 4
HEADS_PER_GROUP = 8
D_STATE = 128
CONV_W = 4
GN = SSD_GROUPS * D_STATE
CONV_DIM = D_INNER + 2 * GN
SSD_CHUNK = 128
POOL_WINDOWS = (2, 4, 8, 16)
POOL_GW = 256
POOL_BUF = 15
N_MEM = 256
MEM_HEADS = 4
MEM_HEAD_DIM = 256
EPS = 1e-5

ROWS_P = BATCH * SEQ
ROWS_S = DEC_BATCH * DEC_SEQ
ROWS = ROWS_P + ROWS_S

LANES = 128
XBC_OFF = D_INNER
DT_OFF = D_INNER + CONV_DIM
IN_PAD_P = 5376
IN_PAD_S = DT_OFF + D_INNER
VMEM_LIMIT = 48 * 1024 * 1024
MXU_TILE = 256
FFN_CHUNK = MXU_TILE
MEM_CHUNKS = D_MODEL // LANES


def _params(n_axes, vmem=VMEM_LIMIT):
    return pltpu.CompilerParams(dimension_semantics=("arbitrary",) * n_axes,
                                vmem_limit_bytes=vmem)


def _rms(x, g):
    r = lax.rsqrt(jnp.mean(x * x, axis=-1, keepdims=True) + EPS)
    return x * r * g


def _silu(x):
    return x * (1.0 / (1.0 + jnp.exp(-x)))


def _softplus(x):
    return jnp.maximum(x, 0.0) + jnp.log(1.0 + jnp.exp(-jnp.abs(x)))


def _dot(a, b):
    return jnp.dot(a, b, preferred_element_type=F32)


def _ffn_kernel(x_ref, g_ref, wg_ref, wu_ref, wd_ref, o_ref, h_ref):
    x = x_ref[...]
    u = _rms(x, g_ref[...]).astype(BF16)
    for lo in range(0, D_FF, FFN_CHUNK):
        a = _dot(u, wg_ref[:, lo:lo + FFN_CHUNK])
        b = _dot(u, wu_ref[:, lo:lo + FFN_CHUNK])
        h_ref[:, lo:lo + FFN_CHUNK] = (_silu(a) * b).astype(BF16)
    o_ref[...] = x + 0.5 * _dot(h_ref[...], wd_ref[...])


def _resident(shape):
    return pl.BlockSpec(shape, lambda *_: (0,) * len(shape), pipeline_mode=pl.Buffered(1))


def _ffn(x, g, wg, wu, wd, *, tm=1024):
    rows = x.shape[0]
    row_spec = pl.BlockSpec((tm, D_MODEL), lambda i: (i, 0))
    return pl.pallas_call(
        _ffn_kernel,
        grid=(rows // tm,),
        in_specs=[row_spec, _resident((1, D_MODEL)), _resident((D_MODEL, D_FF)), _resident((D_MODEL, D_FF)),
                  _resident((D_FF, D_MODEL))],
        out_specs=row_spec,
        out_shape=jax.ShapeDtypeStruct((rows, D_MODEL), F32),
        scratch_shapes=[pltpu.VMEM((tm, D_FF), BF16)],
        compiler_params=_params(1),
        name="ffn",
    )(x, g, wg, wu, wd)


def _rms_mm_kernel(x_ref, g_ref, w_ref, o_ref, u_ref):
    @pl.when(pl.program_id(1) == 0)
    def _():
        u_ref[...] = _rms(x_ref[...], g_ref[...]).astype(BF16)

    o_ref[...] = _dot(u_ref[...], w_ref[...])


def _rms_matmul(x, g, w, *, row_start, n_rows, tm, tn, name):
    n = w.shape[1]
    r0 = row_start // tm
    return pl.pallas_call(
        _rms_mm_kernel,
        grid=(n_rows // tm, n // tn),
        in_specs=[
            pl.BlockSpec((tm, D_MODEL), lambda i, j: (r0 + i, 0)),
            pl.BlockSpec((1, D_MODEL), lambda i, j: (0, 0)),
            pl.BlockSpec((D_MODEL, tn), lambda i, j: (0, j)),
        ],
        out_specs=pl.BlockSpec((tm, tn), lambda i, j: (i, j)),
        out_shape=jax.ShapeDtypeStruct((n_rows, n), F32),
        scratch_shapes=[pltpu.VMEM((tm, D_MODEL), BF16)],
        compiler_params=_params(2),
        name=name,
    )(x, g, w)


def _mem_kv_kernel(m_ref, g_ref, wk_ref, wv_ref, k_ref, v_ref, *, tm):
    u = _rms(m_ref[...], g_ref[...]).astype(BF16)
    for w_ref, o_ref in ((wk_ref, k_ref), (wv_ref, v_ref)):
        kv = _dot(u, w_ref[...])
        for j in range(MEM_CHUNKS):
            o_ref[pl.ds(j, tm, stride=MEM_CHUNKS), :] = kv[:, _chunk_to_col(j):_chunk_to_col(j) + LANES]


def _mem_kv(mem, g, wk, wv, *, tm=512):
    rows = mem.shape[0]
    w_spec = pl.BlockSpec((None, D_MODEL, D_MODEL), lambda l, r: (l, 0, 0))
    o_spec = pl.BlockSpec((None, tm * MEM_CHUNKS, LANES), lambda l, r: (l, r, 0))
    o_shape = jax.ShapeDtypeStruct((DEPTH, rows * MEM_CHUNKS, LANES), F32)
    return pl.pallas_call(
        functools.partial(_mem_kv_kernel, tm=tm),
        grid=(DEPTH, rows // tm),
        in_specs=[
            pl.BlockSpec((tm, D_MODEL), lambda l, r: (r, 0)),
            pl.BlockSpec((None, 1, D_MODEL), lambda l, r: (l, 0, 0)),
            w_spec, w_spec,
        ],
        out_specs=[o_spec, o_spec],
        out_shape=[o_shape, o_shape],
        compiler_params=_params(2),
        name="mem_kv",
    )(mem, g, wk, wv)


def _mm_res_kernel(x_ref, y_ref, w_ref, o_ref):
    o_ref[...] = x_ref[...] + _dot(y_ref[...].astype(BF16), w_ref[...])


def _matmul_residual(x, y, w, *, row_start, tm, name):
    n_rows, k = y.shape
    r0 = row_start // tm
    return pl.pallas_call(
        _mm_res_kernel,
        grid=(n_rows // tm,),
        in_specs=[
            pl.BlockSpec((tm, D_MODEL), lambda i: (r0 + i, 0)),
            pl.BlockSpec((tm, k), lambda i: (i, 0)),
            pl.BlockSpec((k, D_MODEL), lambda i: (0, 0)),
        ],
        out_specs=pl.BlockSpec((tm, D_MODEL), lambda i: (r0 + i, 0)),
        out_shape=jax.ShapeDtypeStruct(x.shape, F32),
        input_output_aliases={0: 0},
        compiler_params=_params(1),
        name=name,
    )(x, y, w)


def _ssd_prompt_kernel(zx_ref, cbuf_ref, st_ref, cw_ref, cb_ref, dtb_ref, alog_ref, dsk_ref, nw_ref,
                       y_ref, nconv_ref, nst_ref, xpad_ref, ht_ref, ybuf_ref, *, nc):
    c = pl.program_id(1)
    q = SSD_CHUNK
    lead = 8
    tail0 = lead - (CONV_W - 1)

    @pl.when(c == 0)
    def _():
        xpad_ref[tail0:lead, :] = cbuf_ref[...]
        ht_ref[...] = st_ref[...].T

    xpad_ref[lead:lead + q, :] = zx_ref[:, XBC_OFF:DT_OFF]
    conv = cw_ref[0:1, :] * xpad_ref[tail0:tail0 + q, :]
    for k in range(1, CONV_W):
        conv = conv + cw_ref[k:k + 1, :] * xpad_ref[tail0 + k:tail0 + k + q, :]
    conv = cb_ref[...] + conv
    tail = xpad_ref[q + tail0:q + lead, :]
    xpad_ref[tail0:lead, :] = tail

    @pl.when(c == nc - 1)
    def _():
        nconv_ref[...] = tail

    xbc = _silu(conv)
    xs = xbc[:, :D_INNER]
    bm = xbc[:, D_INNER:D_INNER + GN]
    cm = xbc[:, D_INNER + GN:]

    dt = _softplus(zx_ref[:, DT_OFF:DT_OFF + LANES] + dtb_ref[...])
    a = -jnp.exp(alog_ref[...])
    da = dt * a
    rows = lax.broadcasted_iota(jnp.int32, (q, q), 0)
    cols = lax.broadcasted_iota(jnp.int32, (q, q), 1)
    causal = rows >= cols
    tri = jnp.where(causal, 1.0, 0.0).astype(F32)
    acs = jnp.dot(tri, da, precision=lax.Precision.HIGHEST, preferred_element_type=F32)
    acs_t = acs.T
    dt_t = dt.T
    eacs = jnp.exp(acs)
    last = acs_t[:, q - 1:q]
    w_t = jnp.exp(last - acs_t) * dt_t
    cd_t = jnp.exp(last)
    low_half = lax.broadcasted_iota(jnp.int32, (1, LANES), 1) < SSD_HEAD_DIM

    for g in range(SSD_GROUPS):
        bm_g = bm[:, g * D_STATE:(g + 1) * D_STATE]
        cm_g = cm[:, g * D_STATE:(g + 1) * D_STATE]
        bm_gt = bm_g.T
        cb_g = _dot(cm_g.astype(BF16), bm_gt.astype(BF16))
        for hh in range(0, HEADS_PER_GROUP, 2):
            h0 = g * HEADS_PER_GROUP + hh
            lo = (h0 // 2) * LANES
            x_pair = xs[:, lo:lo + LANES]
            ht_pair = ht_ref[:, lo:lo + LANES]
            x_pair_b = x_pair.astype(BF16)
            rhs = jnp.concatenate([x_pair_b, ht_pair.astype(BF16)], axis=0)
            ys, sts = [], []
            for h in (h0, h0 + 1):
                seg = acs[:, h:h + 1] - acs_t[h:h + 1, :]
                decay = jnp.exp(jnp.where(causal, seg, -jnp.inf))
                m_h = cb_g * decay * dt_t[h:h + 1, :]
                e_h = eacs[:, h:h + 1] * cm_g
                lhs = jnp.concatenate([m_h.astype(BF16), e_h.astype(BF16)], axis=1)
                ys.append(_dot(lhs, rhs))
                b_h = bm_gt * w_t[h:h + 1, :]
                sts.append(_dot(b_h.astype(BF16), x_pair_b))
            y_pair = jnp.where(low_half, ys[0], ys[1]) + x_pair * dsk_ref[:, lo:lo + LANES]
            st_pair = jnp.where(low_half, sts[0], sts[1])
            cd_row = jnp.where(low_half, cd_t[h0:h0 + 1, :], cd_t[h0 + 1:h0 + 2, :])
            ht_ref[:, lo:lo + LANES] = ht_pair * cd_row + st_pair
            ybuf_ref[:, lo:lo + LANES] = y_pair

    y = ybuf_ref[...] * _silu(zx_ref[:, :D_INNER])
    gw = D_INNER // SSD_GROUPS
    for g in range(SSD_GROUPS):
        yg = y[:, g * gw:(g + 1) * gw]
        yg = yg * lax.rsqrt(jnp.mean(yg * yg, axis=-1, keepdims=True) + EPS)
        y_ref[:, g * gw:(g + 1) * gw] = (yg * nw_ref[:, g * gw:(g + 1) * gw]).astype(BF16)

    @pl.when(c == nc - 1)
    def _():
        nst_ref[...] = ht_ref[...].T


def _ssd_prompt(zx, cbuf, st, cw, cb, dtb, alog, dsk, nw):
    nb = cbuf.shape[0]
    nc = SEQ // SSD_CHUNK
    q = SSD_CHUNK
    vec = lambda n: pl.BlockSpec((1, n), lambda b, c: (0, 0))
    return pl.pallas_call(
        functools.partial(_ssd_prompt_kernel, nc=nc),
        grid=(nb, nc),
        in_specs=[
            pl.BlockSpec((q, IN_PAD_P), lambda b, c: (b * nc + c, 0)),
            pl.BlockSpec((None, CONV_W - 1, CONV_DIM), lambda b, c: (b, 0, 0)),
            pl.BlockSpec((None, D_INNER, D_STATE), lambda b, c: (b, 0, 0)),
            pl.BlockSpec((CONV_W, CONV_DIM), lambda b, c: (0, 0)),
            vec(CONV_DIM), vec(LANES), vec(LANES), vec(D_INNER), vec(D_INNER),
        ],
        out_specs=[
            pl.BlockSpec((q, D_INNER), lambda b, c: (b * nc + c, 0)),
            pl.BlockSpec((None, CONV_W - 1, CONV_DIM), lambda b, c: (b, 0, 0)),
            pl.BlockSpec((None, D_INNER, D_STATE), lambda b, c: (b, 0, 0)),
        ],
        out_shape=[
            jax.ShapeDtypeStruct((nb * SEQ, D_INNER), BF16),
            jax.ShapeDtypeStruct((nb, CONV_W - 1, CONV_DIM), F32),
            jax.ShapeDtypeStruct((nb, D_INNER, D_STATE), F32),
        ],
        scratch_shapes=[
            pltpu.VMEM((q + 8, CONV_DIM), F32),
            pltpu.VMEM((D_STATE, D_INNER), F32),
            pltpu.VMEM((q, D_INNER), F32),
        ],
        compiler_params=_params(2),
        name="ssd_prompt",
    )(zx, cbuf, st, cw, cb, dtb, alog, dsk, nw)


def _ssd_sample_kernel(zx_ref, cbuf_ref, st_ref, cw_ref, cb_ref, dtb_ref, alog_ref, dsk_ref, nw_ref, *rest, tb):
    y_ref, nconv_ref, nst_ref, xpad_ref = rest[-4:]
    q = DEC_SEQ
    lead = 8
    tail0 = lead - (CONV_W - 1)
    gw = D_INNER // SSD_GROUPS
    a = -jnp.exp(alog_ref[...])
    rows = lax.broadcasted_iota(jnp.int32, (q, 1), 0)
    xws, cds, bms = [], [], []
    for b in range(tb):
        r0 = b * q
        xpad_ref[b, tail0:lead, :] = cbuf_ref[b]
        xpad_ref[b, lead:lead + q, :] = zx_ref[r0:r0 + q, XBC_OFF:DT_OFF]
        conv = cw_ref[0:1, :] * xpad_ref[b, tail0:tail0 + q, :]
        for k in range(1, CONV_W):
            conv = conv + cw_ref[k:k + 1, :] * xpad_ref[b, tail0 + k:tail0 + k + q, :]
        conv = cb_ref[...] + conv
        nconv_ref[b] = xpad_ref[b, q + tail0:q + lead, :]

        xbc = _silu(conv)
        xs = xbc[:, :D_INNER]
        bm = xbc[:, D_INNER:D_INNER + GN]
        cm = xbc[:, D_INNER + GN:]

        dt = _softplus(zx_ref[r0:r0 + q, DT_OFF:DT_OFF + D_INNER] + dtb_ref[...])
        da = dt * a
        acs = jnp.where(rows >= 0, da[0:1, :], 0.0)
        for k in range(1, q):
            acs = acs + jnp.where(rows >= k, da[k:k + 1, :], 0.0)
        xdt = xs * dt

        y = jnp.zeros((q, D_INNER), F32)
        for k in range(q):
            decay = jnp.exp(jnp.where(rows >= k, acs - acs[k:k + 1, :], -jnp.inf))
            cb_k = jnp.concatenate(
                [jnp.broadcast_to(
                    jnp.sum(cm[:, g * D_STATE:(g + 1) * D_STATE] * bm[k:k + 1, g * D_STATE:(g + 1) * D_STATE],
                            axis=-1, keepdims=True), (q, gw)) for g in range(SSD_GROUPS)], axis=1)
            y = y + cb_k * decay * xdt[k:k + 1, :]

        cm_rows = jnp.concatenate([cm[:, g * D_STATE:(g + 1) * D_STATE] for g in range(SSD_GROUPS)], axis=0)
        r = lax.dot_general(cm_rows.astype(BF16), st_ref[b].astype(BF16), (((1,), (1,)), ((), ())),
                            preferred_element_type=F32)
        y_off = jnp.concatenate([r[g * q:(g + 1) * q, g * gw:(g + 1) * gw] for g in range(SSD_GROUPS)], axis=1)
        y = y + jnp.exp(acs) * y_off
        y = y + xs * dsk_ref[...]

        y = y * _silu(zx_ref[r0:r0 + q, :D_INNER])
        for g in range(SSD_GROUPS):
            yg = y[:, g * gw:(g + 1) * gw]
            yg = yg * lax.rsqrt(jnp.mean(yg * yg, axis=-1, keepdims=True) + EPS)
            y_ref[r0:r0 + q, g * gw:(g + 1) * gw] = yg * nw_ref[:, g * gw:(g + 1) * gw]

        last = acs[q - 1:q, :]
        xws.append(xdt * jnp.exp(last - acs))
        cds.append(jnp.exp(last))
        bms.append(bm)

    padded = jnp.concatenate(xws + [jnp.zeros((LANES - tb * q, D_INNER), F32)], axis=0)
    padded_t = padded.T.astype(BF16)
    zero_blk = jnp.zeros((q, D_STATE), F32)
    for b0 in range(0, tb, 2):
        for g in range(SSD_GROUPS):
            blocks = []
            for i in range(LANES // q):
                left = bms[b0][:, g * D_STATE:(g + 1) * D_STATE] if i == b0 else zero_blk
                right = bms[b0 + 1][:, g * D_STATE:(g + 1) * D_STATE] if i == b0 + 1 else zero_blk
                blocks.append(jnp.concatenate([left, right], axis=1))
            rhs = jnp.concatenate(blocks, axis=0).astype(BF16)
            st_pair = _dot(padded_t[g * gw:(g + 1) * gw, :], rhs)
            for half, b in enumerate((b0, b0 + 1)):
                for hh in range(HEADS_PER_GROUP):
                    lo = g * gw + hh * SSD_HEAD_DIM
                    cd_h = jnp.broadcast_to(cds[b][:, lo:lo + 1], (SSD_HEAD_DIM, D_STATE))
                    nst_ref[b, lo:lo + SSD_HEAD_DIM, :] = (
                        st_ref[b, lo:lo + SSD_HEAD_DIM, :] * cd_h
                        + st_pair[hh * SSD_HEAD_DIM:(hh + 1) * SSD_HEAD_DIM, half * D_STATE:(half + 1) * D_STATE])


def _ssd_sample(zx, cbuf, st, layer, cw, cb, dtb, alog, dsk, nw, prev_states, *, tb=8):
    q = DEC_SEQ
    vec = lambda n: pl.BlockSpec((1, n), lambda i: (0, 0))
    n_layers = st.shape[0]
    in_specs = [
        pl.BlockSpec((tb * q, IN_PAD_S), lambda i: (i, 0)),
        pl.BlockSpec((None, tb, CONV_W - 1, CONV_DIM), lambda i: (layer, i, 0, 0)),
        pl.BlockSpec((None, tb, D_INNER, D_STATE), lambda i: (layer, i, 0, 0)),
        pl.BlockSpec((CONV_W, CONV_DIM), lambda i: (0, 0)),
        vec(CONV_DIM), vec(D_INNER), vec(D_INNER), vec(D_INNER), vec(D_INNER),
    ]
    args = [zx, cbuf, st, cw, cb, dtb, alog, dsk, nw]
    aliases = {}
    if prev_states is not None:
        in_specs.append(pl.BlockSpec(memory_space=pl.ANY))
        args.append(prev_states)
        aliases = {len(args) - 1: 2}
    return pl.pallas_call(
        functools.partial(_ssd_sample_kernel, tb=tb),
        grid=(DEC_BATCH // tb,),
        in_specs=in_specs,
        out_specs=[
            pl.BlockSpec((tb * q, D_INNER), lambda i: (i, 0)),
            pl.BlockSpec((tb, CONV_W - 1, CONV_DIM), lambda i: (i, 0, 0)),
            pl.BlockSpec((None, tb, D_INNER, D_STATE), lambda i: (layer, i, 0, 0)),
        ],
        out_shape=[
            jax.ShapeDtypeStruct((ROWS_S, D_INNER), F32),
            jax.ShapeDtypeStruct((DEC_BATCH, CONV_W - 1, CONV_DIM), F32),
            jax.ShapeDtypeStruct((n_layers, DEC_BATCH, D_INNER, D_STATE), F32),
        ],
        scratch_shapes=[pltpu.VMEM((tb, q + 8, CONV_DIM), F32)],
        input_output_aliases=aliases,
        compiler_params=_params(1),
        name="ssd_sample",
    )(*args)


def _pool_kernel(x_ref, g_ref, buf_ref, pw_ref, ps_ref, o_ref, np_ref, xx_ref, *, tb, tl, nl, pos0):
    l = pl.program_id(1)
    x = x_ref[...]
    u = _rms(x, g_ref[...]).reshape(tb, tl, D_MODEL)
    lead = POOL_BUF + 1

    @pl.when(l == 0)
    def _():
        xx_ref[:, 1:lead, :] = buf_ref[...]

    xx_ref[:, lead:lead + tl, :] = u
    pos = (pos0 + l * tl + lax.broadcasted_iota(jnp.int32, (1, tl, 1), 1)).astype(F32)
    outs = []
    for gi, w in enumerate(POOL_WINDOWS):
        lo = gi * POOL_GW
        s = xx_ref[:, lead:lead + tl, lo:lo + POOL_GW]
        for j in range(1, w):
            s = s + xx_ref[:, lead - j:lead - j + tl, lo:lo + POOL_GW]
        cnt = jnp.minimum(pos + 1.0, float(w))
        mix = s / cnt - u[:, :, lo:lo + POOL_GW]
        outs.append(_dot(mix.reshape(tb * tl, POOL_GW).astype(BF16), pw_ref[gi]))
    o_ref[...] = x + jnp.concatenate(outs, axis=-1) * ps_ref[...]

    tail = xx_ref[:, tl + 1:tl + lead, :]
    if nl > 1:
        xx_ref[:, 1:lead, :] = tail

    @pl.when(l == nl - 1)
    def _():
        np_ref[...] = tail


def _pool(x, g, buf_arr, buf_index, pw, ps, *, row_start, nb, seq, tb, tl, pos0, name):
    nl = seq // tl
    r0 = row_start // (tb * tl)
    x_spec = pl.BlockSpec((tb * tl, D_MODEL), lambda b, l: (r0 + b * nl + l, 0))
    return pl.pallas_call(
        functools.partial(_pool_kernel, tb=tb, tl=tl, nl=nl, pos0=pos0),
        grid=(nb // tb, nl),
        in_specs=[
            x_spec,
            pl.BlockSpec((1, D_MODEL), lambda b, l: (0, 0)),
            buf_index(tb),
            pl.BlockSpec((len(POOL_WINDOWS), POOL_GW, POOL_GW), lambda b, l: (0, 0, 0)),
            pl.BlockSpec((1, D_MODEL), lambda b, l: (0, 0)),
        ],
        out_specs=[x_spec, pl.BlockSpec((tb, POOL_BUF, D_MODEL), lambda b, l: (b, 0, 0))],
        out_shape=[jax.ShapeDtypeStruct(x.shape, F32),
                   jax.ShapeDtypeStruct((nb, POOL_BUF, D_MODEL), F32)],
        scratch_shapes=[pltpu.VMEM((tb, tl + POOL_BUF + 1, D_MODEL), F32)],
        input_output_aliases={0: 0},
        compiler_params=_params(2),
        name=name,
    )(x, g, buf_arr, pw, ps)


def _head_kv(ref, h):
    return jnp.concatenate([ref[pl.ds(c * MEM_HEADS + h, N_MEM, stride=MEM_CHUNKS), :]
                            for c in range(MEM_HEAD_DIM // LANES)], axis=1).astype(BF16)


def _xattn_prompt_kernel(x_ref, g_ref, wq_ref, k_ref, v_ref, wo_ref, o_ref):
    x = x_ref[...]
    u = _rms(x, g_ref[...]).astype(BF16)
    q = (_dot(u, wq_ref[...]) * (MEM_HEAD_DIM ** -0.5)).astype(BF16)
    outs = []
    for h in range(MEM_HEADS):
        q_h = q[:, h * MEM_HEAD_DIM:(h + 1) * MEM_HEAD_DIM]
        s = lax.dot_general(q_h, _head_kv(k_ref, h), (((1,), (1,)), ((), ())), preferred_element_type=F32)
        e = jnp.exp(s - jnp.max(s, axis=-1, keepdims=True))
        den = jnp.sum(e, axis=-1, keepdims=True)
        outs.append(_dot(e.astype(BF16), _head_kv(v_ref, h)) / den)
    o = jnp.concatenate(outs, axis=-1)
    o_ref[...] = x + _dot(o.astype(BF16), wo_ref[...])


def _xattn_prompt(x, g, wq, mk, mv, wo, layer, *, tl=512):
    nl = SEQ // tl
    x_spec = pl.BlockSpec((tl, D_MODEL), lambda b, l: (b * nl + l, 0))
    w_spec = pl.BlockSpec((D_MODEL, D_MODEL), lambda b, l: (0, 0))
    kv_spec = pl.BlockSpec((None, N_MEM * MEM_CHUNKS, LANES), lambda b, l: (layer, b, 0))
    return pl.pallas_call(
        _xattn_prompt_kernel,
        grid=(BATCH, nl),
        in_specs=[x_spec, pl.BlockSpec((1, D_MODEL), lambda b, l: (0, 0)), w_spec, kv_spec, kv_spec, w_spec],
        out_specs=x_spec,
        out_shape=jax.ShapeDtypeStruct(x.shape, F32),
        input_output_aliases={0: 0},
        compiler_params=_params(2),
        name="xattn_prompt",
    )(x, g, wq, mk, mv, wo)


def _chunk_rows(t):
    lead = t.shape[:-3]
    t = t.reshape(*lead, N_MEM, MEM_HEADS, MEM_HEAD_DIM // LANES, LANES)
    return jnp.swapaxes(t, -2, -3).reshape(*lead, N_MEM * MEM_CHUNKS, LANES)


def _from_chunk_rows(t):
    t = t.reshape(DEPTH, BATCH, N_MEM, MEM_HEAD_DIM // LANES, MEM_HEADS, LANES)
    return jnp.swapaxes(t, -2, -3).reshape(DEPTH, BATCH, N_MEM, MEM_HEADS, MEM_HEAD_DIM)


def _chunk_to_col(j):
    return ((j % MEM_HEADS) * (MEM_HEAD_DIM // LANES) + j // MEM_HEADS) * LANES


def _attn_sample_kernel(q_ref, k_ref, v_ref, o_ref, *, tb):
    n = DEC_SEQ
    rows = lax.broadcasted_iota(jnp.int32, (MEM_HEADS * n, D_MODEL), 0) // n
    chunk = lax.broadcasted_iota(jnp.int32, (MEM_HEADS * n, D_MODEL), 1) // LANES
    own = rows == chunk % MEM_HEADS

    def gather(ref, b):
        return jnp.concatenate([ref[b, pl.ds(j, N_MEM, stride=MEM_CHUNKS), :] for j in range(MEM_CHUNKS)],
                               axis=1).astype(BF16)

    for b in range(tb):
        q_b = q_ref[b * n:(b + 1) * n, :] * (MEM_HEAD_DIM ** -0.5)
        q_c = jnp.concatenate([q_b[:, _chunk_to_col(j):_chunk_to_col(j) + LANES] for j in range(MEM_CHUNKS)], axis=1)
        q_blk = jnp.where(own, jnp.concatenate([q_c] * MEM_HEADS, axis=0), 0.0).astype(BF16)
        s = lax.dot_general(q_blk, gather(k_ref, b), (((1,), (1,)), ((), ())), preferred_element_type=F32)
        e = jnp.exp(s - jnp.max(s, axis=-1, keepdims=True))
        den = jnp.sum(e, axis=-1, keepdims=True)
        o = _dot(e.astype(BF16), gather(v_ref, b)) / den
        for j in range(MEM_CHUNKS):
            h = j % MEM_HEADS
            o_ref[b * n:(b + 1) * n, _chunk_to_col(j):_chunk_to_col(j) + LANES] = (
                o[h * n:(h + 1) * n, j * LANES:(j + 1) * LANES])


def _attn_sample(q, ck, cv, layer, *, tb=4):
    n = DEC_SEQ
    kv_spec = pl.BlockSpec((None, tb, N_MEM * MEM_CHUNKS, LANES), lambda i: (layer, i, 0, 0))
    row_spec = pl.BlockSpec((tb * n, D_MODEL), lambda i: (i, 0))
    return pl.pallas_call(
        functools.partial(_attn_sample_kernel, tb=tb),
        grid=(DEC_BATCH // tb,),
        in_specs=[row_spec, kv_spec, kv_spec],
        out_specs=row_spec,
        out_shape=jax.ShapeDtypeStruct((ROWS_S, D_MODEL), F32),
        compiler_params=_params(1),
        name="attn_sample",
    )(q, ck, cv)


def _norm_kernel(x_ref, g_ref, o_ref):
    o_ref[...] = _rms(x_ref[...], g_ref[...])


def _final_norm(x, g, *, row_start, n_rows, tm=512):
    r0 = row_start // tm
    return pl.pallas_call(
        _norm_kernel,
        grid=(n_rows // tm,),
        in_specs=[pl.BlockSpec((tm, D_MODEL), lambda i: (r0 + i, 0)),
                  pl.BlockSpec((1, D_MODEL), lambda i: (0, 0))],
        out_specs=pl.BlockSpec((tm, D_MODEL), lambda i: (i, 0)),
        out_shape=jax.ShapeDtypeStruct((n_rows, D_MODEL), F32),
        compiler_params=_params(1),
        name="final_norm",
    )(x, g)


def _row(v):
    return v.reshape(1, -1)


def _per_channel(v):
    return jnp.repeat(v, SSD_HEAD_DIM).reshape(1, D_INNER)


def _pad_lanes(v):
    return jnp.pad(v, (0, LANES - v.shape[0])).reshape(1, LANES)


def kernel(x_prompt, x_sample, mem_prompt, cache_mem_k, cache_mem_v, state_ssm, state_conv, state_pool, norm_ffn1, ffn1_w_gate, ffn1_w_up, ffn1_w_down, norm_mix, ssd_in_w, ssd_conv_w, ssd_conv_b, ssd_dt_bias, ssd_a_log, ssd_d, ssd_norm_w, ssd_out_w, pool_w, pool_scale, norm_cross, norm_mem, xa_wq, xa_wk, xa_wv, xa_wo, norm_ffn2, ffn2_w_gate, ffn2_w_up, ffn2_w_down, final_norm):
    bf = lambda w: w.astype(BF16)
    x = jnp.concatenate([x_prompt.reshape(ROWS_P, D_MODEL), x_sample.reshape(ROWS_S, D_MODEL)], axis=0)

    mk_p, mv_p = _mem_kv(mem_prompt.reshape(BATCH * N_MEM, D_MODEL), norm_mem.reshape(DEPTH, 1, D_MODEL),
                         bf(xa_wk), bf(xa_wv))
    ck = _chunk_rows(cache_mem_k)
    cv = _chunk_rows(cache_mem_v)
    st_s = state_ssm.reshape(-1, DEC_BATCH, D_INNER, D_STATE)
    conv0 = jnp.zeros((BATCH, CONV_W - 1, CONV_DIM), F32)
    ssm0 = jnp.zeros((BATCH, D_INNER, D_STATE), F32)
    pool0 = jnp.zeros((BATCH, POOL_BUF, D_MODEL), F32)

    ssm_p, conv_p, pool_p, conv_s, pool_s = [], [], [], [], []
    ssm_s = None
    for i in range(DEPTH):
        j = i // 2
        x = _ffn(x, _row(norm_ffn1[i]), bf(ffn1_w_gate[i]), bf(ffn1_w_up[i]), bf(ffn1_w_down[i]))
        g_mix = _row(norm_mix[i])
        if i % 2 == 0:
            w_in = ssd_in_w[j]
            w_dt = w_in[:, DT_OFF:]
            w_p = bf(jnp.concatenate(
                [w_in[:, :DT_OFF], jnp.pad(w_dt, ((0, 0), (0, IN_PAD_P - DT_OFF - SSD_HEADS)))], axis=1))
            w_s = bf(jnp.concatenate([w_in[:, :DT_OFF], jnp.repeat(w_dt, SSD_HEAD_DIM, axis=1)], axis=1))
            cw, cb = ssd_conv_w[j], _row(ssd_conv_b[j])
            dsk, nw, w_out = _per_channel(ssd_d[j]), _row(ssd_norm_w[j]), bf(ssd_out_w[j])

            zx_p = _rms_matmul(x, g_mix, w_p, row_start=0, n_rows=ROWS_P, tm=1024, tn=768, name="in_proj_p")
            y_p, nc_p, ns_p = _ssd_prompt(zx_p, conv0, ssm0, cw, cb, _pad_lanes(ssd_dt_bias[j]),
                                          _pad_lanes(ssd_a_log[j]), dsk, nw)
            zx_s = _rms_matmul(x, g_mix, w_s, row_start=ROWS_P, n_rows=ROWS_S, tm=1024, tn=1792,
                               name="in_proj_s")
            y_s, nc_s, ssm_s = _ssd_sample(zx_s, state_conv, st_s, j, cw, cb, _per_channel(ssd_dt_bias[j]),
                                           _per_channel(ssd_a_log[j]), dsk, nw, ssm_s)
            x = _matmul_residual(x, y_p, w_out, row_start=0, tm=1024, name="out_proj_p")
            x = _matmul_residual(x, y_s, w_out, row_start=ROWS_P, tm=1024, name="out_proj_s")
            conv_p.append(nc_p)
            conv_s.append(nc_s)
            ssm_p.append(ns_p.reshape(BATCH, SSD_HEADS, SSD_HEAD_DIM, D_STATE))
        else:
            pw, ps = bf(pool_w[j]), _row(pool_scale[j])
            x, np_p = _pool(x, g_mix, pool0, lambda tb: pl.BlockSpec((tb, POOL_BUF, D_MODEL), lambda b, l: (b, 0, 0)),
                            pw, ps, row_start=0, nb=BATCH, seq=SEQ, tb=1, tl=512, pos0=0, name="pool_p")
            x, np_s = _pool(x, g_mix, state_pool,
                            lambda tb: pl.BlockSpec((None, tb, POOL_BUF, D_MODEL), lambda b, l: (j, b, 0, 0)),
                            pw, ps, row_start=ROWS_P, nb=DEC_BATCH, seq=DEC_SEQ, tb=8, tl=DEC_SEQ,
                            pos0=PAST_LEN, name="pool_s")
            pool_p.append(np_p)
            pool_s.append(np_s)

        g_x = _row(norm_cross[i])
        wq, wo = bf(xa_wq[i]), bf(xa_wo[i])
        x = _xattn_prompt(x, g_x, wq, mk_p, mv_p, wo, i)
        q_s = _rms_matmul(x, g_x, wq, row_start=ROWS_P, n_rows=ROWS_S, tm=1024, tn=1024, name="q_proj_s")
        o_s = _attn_sample(q_s, ck, cv, i)
        x = _matmul_residual(x, o_s, wo, row_start=ROWS_P, tm=1024, name="wo_s")

        x = _ffn(x, _row(norm_ffn2[i]), bf(ffn2_w_gate[i]), bf(ffn2_w_up[i]), bf(ffn2_w_down[i]))

    g_f = _row(final_norm)
    y_prompt = _final_norm(x, g_f, row_start=0, n_rows=ROWS_P).reshape(BATCH, SEQ, D_MODEL)
    y_sample = _final_norm(x, g_f, row_start=ROWS_P, n_rows=ROWS_S).reshape(DEC_BATCH, DEC_SEQ, D_MODEL)
    return (y_prompt, y_sample, jnp.stack(ssm_p), jnp.stack(conv_p), jnp.stack(pool_p),
            _from_chunk_rows(mk_p), _from_chunk_rows(mv_p),
            ssm_s.reshape(state_ssm.shape), jnp.stack(conv_s), jnp.stack(pool_s))
```

```python
import functools

import jax
import jax.numpy as jnp
from jax import lax
from jax.experimental import pallas as pl
from jax.experimental.pallas import tpu as pltpu

F32 = jnp.float32
BF16 = jnp.bfloat16

D_MODEL = 1024
BATCH = 8
SEQ = 2048
DEPTH = 4
DEC_BATCH = 128
DEC_SEQ = 8
PAST_LEN = 16384
D_FF = 2816
D_INNER = 2048
SSD_HEAD_DIM = 64
SSD_HEADS = 32
SSD_GROUPS = 4
HEADS_PER_GROUP = 8
D_STATE = 128
CONV_W = 4
GN = SSD_GROUPS * D_STATE
CONV_DIM = D_INNER + 2 * GN
SSD_CHUNK = 128
POOL_WINDOWS = (2, 4, 8, 16)
POOL_GW = 256
POOL_BUF = 15
N_MEM = 256
MEM_HEADS = 4
MEM_HEAD_DIM = 256
EPS = 1e-5

ROWS_P = BATCH * SEQ
ROWS_S = DEC_BATCH * DEC_SEQ
ROWS = ROWS_P + ROWS_S

LANES = 128
XBC_OFF = D_INNER
DT_OFF = D_INNER + CONV_DIM
IN_PAD_P = 5376
IN_PAD_S = DT_OFF + D_INNER
VMEM_LIMIT = 48 * 1024 * 1024
MXU_TILE = 256
FFN_CHUNK = MXU_TILE
MEM_CHUNKS = D_MODEL // LANES


def _params(n_axes, vmem=VMEM_LIMIT):
    return pltpu.CompilerParams(dimension_semantics=("arbitrary",) * n_axes,
                                vmem_limit_bytes=vmem)


def _rms(x, g):
    r = lax.rsqrt(jnp.mean(x * x, axis=-1, keepdims=True) + EPS)
    return x * r * g


def _silu(x):
    return x * (1.0 / (1.0 + jnp.exp(-x)))


def _softplus(x):
    return jnp.maximum(x, 0.0) + jnp.log(1.0 + jnp.exp(-jnp.abs(x)))


def _dot(a, b):
    return jnp.dot(a, b, preferred_element_type=F32)


def _ffn_kernel(*refs, final):
    x_ref, g_ref, wg_ref, wu_ref, wd_ref = refs[:5]
    o_ref, h_ref = refs[-2:]
    x = x_ref[...]
    u = _rms(x, g_ref[...]).astype(BF16)
    for lo in range(0, D_FF, FFN_CHUNK):
        a = _dot(u, wg_ref[:, lo:lo + FFN_CHUNK])
        b = _dot(u, wu_ref[:, lo:lo + FFN_CHUNK])
        h_ref[:, lo:lo + FFN_CHUNK] = (_silu(a) * b).astype(BF16)
    y = x + 0.5 * _dot(h_ref[...], wd_ref[...])
    o_ref[...] = _rms(y, refs[5][...]) if final else y


def _resident(shape):
    return pl.BlockSpec(shape, lambda *_: (0,) * len(shape), pipeline_mode=pl.Buffered(1))


def _ffn(src, g, wg, wu, wd, *, src_row_start=0, n_rows=None, out_rows=None, out_row_start=0, into=None,
         final_g=None, tm=1024):
    n_rows = src.shape[0] if n_rows is None else n_rows
    out_rows = n_rows if out_rows is None else out_rows
    s0, o0 = src_row_start // tm, out_row_start // tm
    in_specs = [pl.BlockSpec((tm, D_MODEL), lambda i: (s0 + i, 0)), _resident((1, D_MODEL)),
                _resident((D_MODEL, D_FF)), _resident((D_MODEL, D_FF)), _resident((D_FF, D_MODEL))]
    args = [src, g, wg, wu, wd]
    if final_g is not None:
        in_specs.append(_resident((1, D_MODEL)))
        args.append(final_g)
    aliases = {}
    if into is not None:
        in_specs.append(pl.BlockSpec(memory_space=pl.ANY))
        args.append(into)
        aliases = {len(args) - 1: 0}
    return pl.pallas_call(
        functools.partial(_ffn_kernel, final=final_g is not None),
        grid=(n_rows // tm,),
        in_specs=in_specs,
        out_specs=pl.BlockSpec((tm, D_MODEL), lambda i: (o0 + i, 0)),
        out_shape=jax.ShapeDtypeStruct((out_rows, D_MODEL), F32),
        scratch_shapes=[pltpu.VMEM((tm, D_FF), BF16)],
        input_output_aliases=aliases,
        compiler_params=_params(1),
        name="ffn",
    )(*args)


def _rms_mm_kernel(x_ref, g_ref, w_ref, o_ref, u_ref):
    @pl.when(pl.program_id(1) == 0)
    def _():
        u_ref[...] = _rms(x_ref[...], g_ref[...]).astype(BF16)

    o_ref[...] = _dot(u_ref[...], w_ref[...])


def _rms_matmul(x, g, w, *, row_start, n_rows, tm, tn, name):
    n = w.shape[1]
    r0 = row_start // tm
    return pl.pallas_call(
        _rms_mm_kernel,
        grid=(n_rows // tm, n // tn),
        in_specs=[
            pl.BlockSpec((tm, D_MODEL), lambda i, j: (r0 + i, 0)),
            pl.BlockSpec((1, D_MODEL), lambda i, j: (0, 0)),
            pl.BlockSpec((D_MODEL, tn), lambda i, j: (0, j)),
        ],
        out_specs=pl.BlockSpec((tm, tn), lambda i, j: (i, j)),
        out_shape=jax.ShapeDtypeStruct((n_rows, n), F32),
        scratch_shapes=[pltpu.VMEM((tm, D_MODEL), BF16)],
        compiler_params=_params(2),
        name=name,
    )(x, g, w)


def _mem_kv_kernel(m_ref, g_ref, wk_ref, wv_ref, k_ref, v_ref, *, tm):
    u = _rms(m_ref[...], g_ref[...]).astype(BF16)
    for w_ref, o_ref in ((wk_ref, k_ref), (wv_ref, v_ref)):
        kv = _dot(u, w_ref[...])
        for j in range(MEM_CHUNKS):
            o_ref[pl.ds(j, tm, stride=MEM_CHUNKS), :] = kv[:, _chunk_to_col(j):_chunk_to_col(j) + LANES]


def _mem_kv(mem, g, wk, wv, *, tm=512):
    rows = mem.shape[0]
    w_spec = pl.BlockSpec((None, D_MODEL, D_MODEL), lambda l, r: (l, 0, 0))
    o_spec = pl.BlockSpec((None, tm * MEM_CHUNKS, LANES), lambda l, r: (l, r, 0))
    o_shape = jax.ShapeDtypeStruct((DEPTH, rows * MEM_CHUNKS, LANES), F32)
    return pl.pallas_call(
        functools.partial(_mem_kv_kernel, tm=tm),
        grid=(DEPTH, rows // tm),
        in_specs=[
            pl.BlockSpec((tm, D_MODEL), lambda l, r: (r, 0)),
            pl.BlockSpec((None, 1, D_MODEL), lambda l, r: (l, 0, 0)),
            w_spec, w_spec,
        ],
        out_specs=[o_spec, o_spec],
        out_shape=[o_shape, o_shape],
        compiler_params=_params(2),
        name="mem_kv",
    )(mem, g, wk, wv)


def _mm_res_kernel(x_ref, y_ref, w_ref, *rest):
    o_ref = rest[-1]
    o_ref[...] = x_ref[...] + _dot(y_ref[...].astype(BF16), w_ref[...])


def _matmul_residual(x, y, w, *, row_start, tm, name, into=None):
    n_rows, k = y.shape
    r0 = row_start // tm
    in_specs = [
        pl.BlockSpec((tm, D_MODEL), lambda i: (r0 + i, 0)),
        pl.BlockSpec((tm, k), lambda i: (i, 0)),
        pl.BlockSpec((k, D_MODEL), lambda i: (0, 0)),
    ]
    args = [x, y, w]
    aliases = {0: 0}
    if into is not None:
        in_specs.append(pl.BlockSpec(memory_space=pl.ANY))
        args.append(into)
        aliases = {3: 0}
    return pl.pallas_call(
        _mm_res_kernel,
        grid=(n_rows // tm,),
        in_specs=in_specs,
        out_specs=pl.BlockSpec((tm, D_MODEL), lambda i: (r0 + i, 0)),
        out_shape=jax.ShapeDtypeStruct(x.shape, F32),
        input_output_aliases=aliases,
        compiler_params=_params(1),
        name=name,
    )(*args)


CONV_LEAD = 8
CONV_TAIL0 = CONV_LEAD - (CONV_W - 1)


def _scan_chunk(zx, x_res, cw_ref, cb_ref, dtb_ref, alog_ref, dsk_ref, nw_ref, wout_ref,
                xpad_ref, ht_ref, ybuf_ref):
    q = SSD_CHUNK
    lead, tail0 = CONV_LEAD, CONV_TAIL0
    xpad_ref[lead:lead + q, :] = zx[:, XBC_OFF:DT_OFF]
    conv = cw_ref[0:1, :] * xpad_ref[tail0:tail0 + q, :]
    for k in range(1, CONV_W):
        conv = conv + cw_ref[k:k + 1, :] * xpad_ref[tail0 + k:tail0 + k + q, :]
    conv = cb_ref[...] + conv
    tail = xpad_ref[q + tail0:q + lead, :]
    xpad_ref[tail0:lead, :] = tail

    xbc = _silu(conv)
    xs = xbc[:, :D_INNER]
    bm = xbc[:, D_INNER:D_INNER + GN]
    cm = xbc[:, D_INNER + GN:]

    dt = _softplus(zx[:, DT_OFF:DT_OFF + LANES] + dtb_ref[...])
    a = -jnp.exp(alog_ref[...])
    da = dt * a
    rows = lax.broadcasted_iota(jnp.int32, (q, q), 0)
    cols = lax.broadcasted_iota(jnp.int32, (q, q), 1)
    causal = rows >= cols
    tri = jnp.where(causal, 1.0, 0.0).astype(F32)
    acs = jnp.dot(tri, da, precision=lax.Precision.HIGHEST, preferred_element_type=F32)
    acs_t = acs.T
    dt_t = dt.T
    src_t = acs_t - jnp.log(dt_t)
    eacs = jnp.exp(acs)
    last = acs_t[:, q - 1:q]
    w_t = jnp.exp(last - acs_t) * dt_t
    cd_t = jnp.exp(last)
    low_half = lax.broadcasted_iota(jnp.int32, (1, LANES), 1) < SSD_HEAD_DIM

    for g in range(SSD_GROUPS):
        bm_g = bm[:, g * D_STATE:(g + 1) * D_STATE]
        cm_g = cm[:, g * D_STATE:(g + 1) * D_STATE]
        bm_gt = bm_g.T
        cb_g = _dot(cm_g.astype(BF16), bm_gt.astype(BF16))
        for hh in range(0, HEADS_PER_GROUP, 2):
            h0 = g * HEADS_PER_GROUP + hh
            lo = (h0 // 2) * LANES
            x_pair = xs[:, lo:lo + LANES]
            ht_pair = ht_ref[:, lo:lo + LANES]
            x_pair_b = x_pair.astype(BF16)
            rhs = jnp.concatenate([x_pair_b, ht_pair.astype(BF16)], axis=0)
            ys, sts = [], []
            for h in (h0, h0 + 1):
                seg = acs[:, h:h + 1] - src_t[h:h + 1, :]
                m_h = cb_g * jnp.exp(jnp.where(causal, seg, -jnp.inf))
                e_h = eacs[:, h:h + 1] * cm_g
                lhs = jnp.concatenate([m_h.astype(BF16), e_h.astype(BF16)], axis=1)
                ys.append(_dot(lhs, rhs))
                b_h = bm_gt * w_t[h:h + 1, :]
                sts.append(_dot(b_h.astype(BF16), x_pair_b))
            y_pair = jnp.where(low_half, ys[0], ys[1]) + x_pair * dsk_ref[:, lo:lo + LANES]
            st_pair = jnp.where(low_half, sts[0], sts[1])
            cd_row = jnp.where(low_half, cd_t[h0:h0 + 1, :], cd_t[h0 + 1:h0 + 2, :])
            ht_ref[:, lo:lo + LANES] = ht_pair * cd_row + st_pair
            ybuf_ref[:, lo:lo + LANES] = y_pair

    y = ybuf_ref[...] * _silu(zx[:, :D_INNER])
    gw = D_INNER // SSD_GROUPS
    y_norm = []
    for g in range(SSD_GROUPS):
        yg = y[:, g * gw:(g + 1) * gw]
        yg = yg * lax.rsqrt(jnp.mean(yg * yg, axis=-1, keepdims=True) + EPS)
        y_norm.append((yg * nw_ref[:, g * gw:(g + 1) * gw]).astype(BF16))
    return x_res + _dot(jnp.concatenate(y_norm, axis=1), wout_ref[...]), tail


def _ssd_prompt_kernel(xa_ref, xb_ref, g_ref, win_ref, cbuf_ref, st_ref, cw_ref, cb_ref, dtb_ref, alog_ref,
                       dsk_ref, nw_ref, wout_ref, o_ref, nconv_ref, nst_ref,
                       za_ref, zb_ref, xpad_ref, ht_ref, ybuf_ref, *, ns):
    s = pl.program_id(1)
    q = SSD_CHUNK

    def project(x_rows, z_ref):
        z_ref[...] = _dot(_rms(x_rows, g_ref[...]).astype(BF16), win_ref[...])

    scan = functools.partial(_scan_chunk, cw_ref=cw_ref, cb_ref=cb_ref, dtb_ref=dtb_ref, alog_ref=alog_ref,
                             dsk_ref=dsk_ref, nw_ref=nw_ref, wout_ref=wout_ref,
                             xpad_ref=xpad_ref, ht_ref=ht_ref, ybuf_ref=ybuf_ref)

    @pl.when(s == 0)
    def _():
        xpad_ref[CONV_TAIL0:CONV_LEAD, :] = cbuf_ref[...]
        ht_ref[...] = st_ref[...].T
        project(xa_ref[0:q, :], za_ref)

    x1 = xa_ref[q:2 * q, :]
    project(x1, zb_ref)
    o_ref[0:q, :], _ = scan(za_ref, xa_ref[0:q, :])
    project(xb_ref[...], za_ref)
    o_ref[q:2 * q, :], tail = scan(zb_ref, x1)

    @pl.when(s == ns - 1)
    def _():
        nconv_ref[...] = tail
        nst_ref[...] = ht_ref[...].T


def _ssd_prompt(x, g, w_in, cbuf, st, cw, cb, dtb, alog, dsk, nw, w_out):
    nb = cbuf.shape[0]
    nc = SEQ // SSD_CHUNK
    ns = nc // 2
    q = SSD_CHUNK
    vec = lambda n: pl.BlockSpec((1, n), lambda b, c: (0, 0))
    return pl.pallas_call(
        functools.partial(_ssd_prompt_kernel, ns=ns),
        grid=(nb, ns),
        in_specs=[
            pl.BlockSpec((2 * q, D_MODEL), lambda b, s: (b * ns + s, 0)),
            pl.BlockSpec((q, D_MODEL), lambda b, s: (b * nc + jnp.minimum(2 * s + 2, nc - 1), 0)),
            vec(D_MODEL),
            _resident((D_MODEL, IN_PAD_P)),
            pl.BlockSpec((None, CONV_W - 1, CONV_DIM), lambda b, c: (b, 0, 0)),
            pl.BlockSpec((None, D_INNER, D_STATE), lambda b, c: (b, 0, 0)),
            pl.BlockSpec((CONV_W, CONV_DIM), lambda b, c: (0, 0)),
            vec(CONV_DIM), vec(LANES), vec(LANES), vec(D_INNER), vec(D_INNER),
            _resident((D_INNER, D_MODEL)),
        ],
        out_specs=[
            pl.BlockSpec((2 * q, D_MODEL), lambda b, s: (b * ns + s, 0)),
            pl.BlockSpec((None, CONV_W - 1, CONV_DIM), lambda b, c: (b, 0, 0)),
            pl.BlockSpec((None, D_INNER, D_STATE), lambda b, c: (b, 0, 0)),
        ],
        out_shape=[
            jax.ShapeDtypeStruct(x.shape, F32),
            jax.ShapeDtypeStruct((nb, CONV_W - 1, CONV_DIM), F32),
            jax.ShapeDtypeStruct((nb, D_INNER, D_STATE), F32),
        ],
        scratch_shapes=[
            pltpu.VMEM((q, IN_PAD_P), F32),
            pltpu.VMEM((q, IN_PAD_P), F32),
            pltpu.VMEM((q + CONV_LEAD, CONV_DIM), F32),
            pltpu.VMEM((D_STATE, D_INNER), F32),
            pltpu.VMEM((q, D_INNER), F32),
        ],
        compiler_params=_params(2),
        name="ssd_prompt",
    )(x, x, g, w_in, cbuf, st, cw, cb, dtb, alog, dsk, nw, w_out)


def _ssd_sample_kernel(zx_ref, cbuf_ref, st_ref, cw_ref, cb_ref, dtb_ref, alog_ref, dsk_ref, nw_ref, *rest, tb):
    y_ref, nconv_ref, nst_ref, xpad_ref = rest[-4:]
    q = DEC_SEQ
    lead = 8
    tail0 = lead - (CONV_W - 1)
    gw = D_INNER // SSD_GROUPS
    a = -jnp.exp(alog_ref[...])
    rows = lax.broadcasted_iota(jnp.int32, (q, 1), 0)
    xws, cds, bms = [], [], []
    for b in range(tb):
        r0 = b * q
        xpad_ref[b, tail0:lead, :] = cbuf_ref[b]
        xpad_ref[b, lead:lead + q, :] = zx_ref[r0:r0 + q, XBC_OFF:DT_OFF]
        conv = cw_ref[0:1, :] * xpad_ref[b, tail0:tail0 + q, :]
        for k in range(1, CONV_W):
            conv = conv + cw_ref[k:k + 1, :] * xpad_ref[b, tail0 + k:tail0 + k + q, :]
        conv = cb_ref[...] + conv
        nconv_ref[b] = xpad_ref[b, q + tail0:q + lead, :]

        xbc = _silu(conv)
        xs = xbc[:, :D_INNER]
        bm = xbc[:, D_INNER:D_INNER + GN]
        cm = xbc[:, D_INNER + GN:]

        dt = _softplus(zx_ref[r0:r0 + q, DT_OFF:DT_OFF + D_INNER] + dtb_ref[...])
        da = dt * a
        acs = jnp.where(rows >= 0, da[0:1, :], 0.0)
        for k in range(1, q):
            acs = acs + jnp.where(rows >= k, da[k:k + 1, :], 0.0)
        xdt = xs * dt

        y = jnp.zeros((q, D_INNER), F32)
        for k in range(q):
            decay = jnp.exp(jnp.where(rows >= k, acs - acs[k:k + 1, :], -jnp.inf))
            cb_k = jnp.concatenate(
                [jnp.broadcast_to(
                    jnp.sum(cm[:, g * D_STATE:(g + 1) * D_STATE] * bm[k:k + 1, g * D_STATE:(g + 1) * D_STATE],
                            axis=-1, keepdims=True), (q, gw)) for g in range(SSD_GROUPS)], axis=1)
            y = y + cb_k * decay * xdt[k:k + 1, :]

        cm_rows = jnp.concatenate([cm[:, g * D_STATE:(g + 1) * D_STATE] for g in range(SSD_GROUPS)], axis=0)
        r = lax.dot_general(cm_rows.astype(BF16), st_ref[b].astype(BF16), (((1,), (1,)), ((), ())),
                            preferred_element_type=F32)
        y_off = jnp.concatenate([r[g * q:(g + 1) * q, g * gw:(g + 1) * gw] for g in range(SSD_GROUPS)], axis=1)
        y = y + jnp.exp(acs) * y_off
        y = y + xs * dsk_ref[...]

        y = y * _silu(zx_ref[r0:r0 + q, :D_INNER])
        for g in range(SSD_GROUPS):
            yg = y[:, g * gw:(g + 1) * gw]
            yg = yg * lax.rsqrt(jnp.mean(yg * yg, axis=-1, keepdims=True) + EPS)
            y_ref[r0:r0 + q, g * gw:(g + 1) * gw] = yg * nw_ref[:, g * gw:(g + 1) * gw]

        last = acs[q - 1:q, :]
        xws.append(xdt * jnp.exp(last - acs))
        cds.append(jnp.exp(last))
        bms.append(bm)

    padded = jnp.concatenate(xws + [jnp.zeros((LANES - tb * q, D_INNER), F32)], axis=0)
    padded_t = padded.T.astype(BF16)
    zero_blk = jnp.zeros((q, D_STATE), F32)
    for b0 in range(0, tb, 2):
        for g in range(SSD_GROUPS):
            blocks = []
            for i in range(LANES // q):
                left = bms[b0][:, g * D_STATE:(g + 1) * D_STATE] if i == b0 else zero_blk
                right = bms[b0 + 1][:, g * D_STATE:(g + 1) * D_STATE] if i == b0 + 1 else zero_blk
                blocks.append(jnp.concatenate([left, right], axis=1))
            rhs = jnp.concatenate(blocks, axis=0).astype(BF16)
            st_pair = _dot(padded_t[g * gw:(g + 1) * gw, :], rhs)
            for half, b in enumerate((b0, b0 + 1)):
                for hh in range(HEADS_PER_GROUP):
                    lo = g * gw + hh * SSD_HEAD_DIM
                    cd_h = jnp.broadcast_to(cds[b][:, lo:lo + 1], (SSD_HEAD_DIM, D_STATE))
                    nst_ref[b, lo:lo + SSD_HEAD_DIM, :] = (
                        st_ref[b, lo:lo + SSD_HEAD_DIM, :] * cd_h
                        + st_pair[hh * SSD_HEAD_DIM:(hh + 1) * SSD_HEAD_DIM, half * D_STATE:(half + 1) * D_STATE])


def _ssd_sample(zx, cbuf, st, layer, cw, cb, dtb, alog, dsk, nw, prev_states, *, tb=8):
    q = DEC_SEQ
    vec = lambda n: pl.BlockSpec((1, n), lambda i: (0, 0))
    n_layers = st.shape[0]
    in_specs = [
        pl.BlockSpec((tb * q, IN_PAD_S), lambda i: (i, 0)),
        pl.BlockSpec((None, tb, CONV_W - 1, CONV_DIM), lambda i: (layer, i, 0, 0)),
        pl.BlockSpec((None, tb, D_INNER, D_STATE), lambda i: (layer, i, 0, 0)),
        pl.BlockSpec((CONV_W, CONV_DIM), lambda i: (0, 0)),
        vec(CONV_DIM), vec(D_INNER), vec(D_INNER), vec(D_INNER), vec(D_INNER),
    ]
    args = [zx, cbuf, st, cw, cb, dtb, alog, dsk, nw]
    aliases = {}
    if prev_states is not None:
        in_specs.append(pl.BlockSpec(memory_space=pl.ANY))
        args.append(prev_states)
        aliases = {len(args) - 1: 2}
    return pl.pallas_call(
        functools.partial(_ssd_sample_kernel, tb=tb),
        grid=(DEC_BATCH // tb,),
        in_specs=in_specs,
        out_specs=[
            pl.BlockSpec((tb * q, D_INNER), lambda i: (i, 0)),
            pl.BlockSpec((tb, CONV_W - 1, CONV_DIM), lambda i: (i, 0, 0)),
            pl.BlockSpec((None, tb, D_INNER, D_STATE), lambda i: (layer, i, 0, 0)),
        ],
        out_shape=[
            jax.ShapeDtypeStruct((ROWS_S, D_INNER), F32),
            jax.ShapeDtypeStruct((DEC_BATCH, CONV_W - 1, CONV_DIM), F32),
            jax.ShapeDtypeStruct((n_layers, DEC_BATCH, D_INNER, D_STATE), F32),
        ],
        scratch_shapes=[pltpu.VMEM((tb, q + 8, CONV_DIM), F32)],
        input_output_aliases=aliases,
        compiler_params=_params(1),
        name="ssd_sample",
    )(*args)


def _pool_kernel(x_ref, g_ref, buf_ref, pw_ref, ps_ref, o_ref, np_ref, xx_ref, *, tb, tl, nl, pos0):
    l = pl.program_id(1)
    x = x_ref[...]
    u = _rms(x, g_ref[...]).reshape(tb, tl, D_MODEL)
    lead = POOL_BUF + 1

    @pl.when(l == 0)
    def _():
        xx_ref[:, 1:lead, :] = buf_ref[...]

    xx_ref[:, lead:lead + tl, :] = u
    pos = (pos0 + l * tl + lax.broadcasted_iota(jnp.int32, (1, tl, 1), 1)).astype(F32)
    outs = []
    for gi, w in enumerate(POOL_WINDOWS):
        lo = gi * POOL_GW
        s = xx_ref[:, lead:lead + tl, lo:lo + POOL_GW]
        for j in range(1, w):
            s = s + xx_ref[:, lead - j:lead - j + tl, lo:lo + POOL_GW]
        cnt = jnp.minimum(pos + 1.0, float(w))
        mix = s / cnt - u[:, :, lo:lo + POOL_GW]
        outs.append(_dot(mix.reshape(tb * tl, POOL_GW).astype(BF16), pw_ref[gi]))
    o_ref[...] = x + jnp.concatenate(outs, axis=-1) * ps_ref[...]

    tail = xx_ref[:, tl + 1:tl + lead, :]
    if nl > 1:
        xx_ref[:, 1:lead, :] = tail

    @pl.when(l == nl - 1)
    def _():
        np_ref[...] = tail


def _pool(x, g, buf_arr, buf_index, pw, ps, *, row_start, nb, seq, tb, tl, pos0, name):
    nl = seq // tl
    r0 = row_start // (tb * tl)
    x_spec = pl.BlockSpec((tb * tl, D_MODEL), lambda b, l: (r0 + b * nl + l, 0))
    return pl.pallas_call(
        functools.partial(_pool_kernel, tb=tb, tl=tl, nl=nl, pos0=pos0),
        grid=(nb // tb, nl),
        in_specs=[
            x_spec,
            pl.BlockSpec((1, D_MODEL), lambda b, l: (0, 0)),
            buf_index(tb),
            pl.BlockSpec((len(POOL_WINDOWS), POOL_GW, POOL_GW), lambda b, l: (0, 0, 0)),
            pl.BlockSpec((1, D_MODEL), lambda b, l: (0, 0)),
        ],
        out_specs=[x_spec, pl.BlockSpec((tb, POOL_BUF, D_MODEL), lambda b, l: (b, 0, 0))],
        out_shape=[jax.ShapeDtypeStruct(x.shape, F32),
                   jax.ShapeDtypeStruct((nb, POOL_BUF, D_MODEL), F32)],
        scratch_shapes=[pltpu.VMEM((tb, tl + POOL_BUF + 1, D_MODEL), F32)],
        input_output_aliases={0: 0},
        compiler_params=_params(2),
        name=name,
    )(x, g, buf_arr, pw, ps)


def _head_kv(ref, h):
    return jnp.concatenate([ref[pl.ds(c * MEM_HEADS + h, N_MEM, stride=MEM_CHUNKS), :]
                            for c in range(MEM_HEAD_DIM // LANES)], axis=1).astype(BF16)


def _xattn_prompt_kernel(x_ref, g_ref, wq_ref, k_ref, v_ref, wo_ref, o_ref):
    x = x_ref[...]
    u = _rms(x, g_ref[...]).astype(BF16)
    q = (_dot(u, wq_ref[...]) * (MEM_HEAD_DIM ** -0.5)).astype(BF16)
    outs = []
    for h in range(MEM_HEADS):
        q_h = q[:, h * MEM_HEAD_DIM:(h + 1) * MEM_HEAD_DIM]
        s = lax.dot_general(q_h, _head_kv(k_ref, h), (((1,), (1,)), ((), ())), preferred_element_type=F32)
        e = jnp.exp(s - jnp.max(s, axis=-1, keepdims=True))
        den = jnp.sum(e, axis=-1, keepdims=True)
        outs.append(_dot(e.astype(BF16), _head_kv(v_ref, h)) / den)
    o = jnp.concatenate(outs, axis=-1)
    o_ref[...] = x + _dot(o.astype(BF16), wo_ref[...])


def _xattn_prompt(x, g, wq, mk, mv, wo, layer, *, tl=512):
    nl = SEQ // tl
    x_spec = pl.BlockSpec((tl, D_MODEL), lambda b, l: (b * nl + l, 0))
    w_spec = pl.BlockSpec((D_MODEL, D_MODEL), lambda b, l: (0, 0))
    kv_spec = pl.BlockSpec((None, N_MEM * MEM_CHUNKS, LANES), lambda b, l: (layer, b, 0))
    return pl.pallas_call(
        _xattn_prompt_kernel,
        grid=(BATCH, nl),
        in_specs=[x_spec, pl.BlockSpec((1, D_MODEL), lambda b, l: (0, 0)), w_spec, kv_spec, kv_spec, w_spec],
        out_specs=x_spec,
        out_shape=jax.ShapeDtypeStruct(x.shape, F32),
        input_output_aliases={0: 0},
        compiler_params=_params(2),
        name="xattn_prompt",
    )(x, g, wq, mk, mv, wo)


def _chunk_rows(t):
    lead = t.shape[:-3]
    t = t.reshape(*lead, N_MEM, MEM_HEADS, MEM_HEAD_DIM // LANES, LANES)
    return jnp.swapaxes(t, -2, -3).reshape(*lead, N_MEM * MEM_CHUNKS, LANES)


def _from_chunk_rows(t):
    t = t.reshape(DEPTH, BATCH, N_MEM, MEM_HEAD_DIM // LANES, MEM_HEADS, LANES)
    return jnp.swapaxes(t, -2, -3).reshape(DEPTH, BATCH, N_MEM, MEM_HEADS, MEM_HEAD_DIM)


def _chunk_to_col(j):
    return ((j % MEM_HEADS) * (MEM_HEAD_DIM // LANES) + j // MEM_HEADS) * LANES


def _attn_sample_kernel(q_ref, k_ref, v_ref, o_ref, *, tb):
    n = DEC_SEQ
    rows = lax.broadcasted_iota(jnp.int32, (MEM_HEADS * n, D_MODEL), 0) // n
    chunk = lax.broadcasted_iota(jnp.int32, (MEM_HEADS * n, D_MODEL), 1) // LANES
    own = rows == chunk % MEM_HEADS

    def gather(ref, b):
        return jnp.concatenate([ref[b, pl.ds(j, N_MEM, stride=MEM_CHUNKS), :] for j in range(MEM_CHUNKS)],
                               axis=1).astype(BF16)

    for b in range(tb):
        q_b = q_ref[b * n:(b + 1) * n, :] * (MEM_HEAD_DIM ** -0.5)
        q_c = jnp.concatenate([q_b[:, _chunk_to_col(j):_chunk_to_col(j) + LANES] for j in range(MEM_CHUNKS)], axis=1)
        q_blk = jnp.where(own, jnp.concatenate([q_c] * MEM_HEADS, axis=0), 0.0).astype(BF16)
        s = lax.dot_general(q_blk, gather(k_ref, b), (((1,), (1,)), ((), ())), preferred_element_type=F32)
        e = jnp.exp(s - jnp.max(s, axis=-1, keepdims=True))
        den = jnp.sum(e, axis=-1, keepdims=True)
        o = _dot(e.astype(BF16), gather(v_ref, b)) / den
        for j in range(MEM_CHUNKS):
            h = j % MEM_HEADS
            o_ref[b * n:(b + 1) * n, _chunk_to_col(j):_chunk_to_col(j) + LANES] = (
                o[h * n:(h + 1) * n, j * LANES:(j + 1) * LANES])


def _attn_sample(q, ck, cv, layer, *, tb=4):
    n = DEC_SEQ
    kv_spec = pl.BlockSpec((None, tb, N_MEM * MEM_CHUNKS, LANES), lambda i: (layer, i, 0, 0))
    row_spec = pl.BlockSpec((tb * n, D_MODEL), lambda i: (i, 0))
    return pl.pallas_call(
        functools.partial(_attn_sample_kernel, tb=tb),
        grid=(DEC_BATCH // tb,),
        in_specs=[row_spec, kv_spec, kv_spec],
        out_specs=row_spec,
        out_shape=jax.ShapeDtypeStruct((ROWS_S, D_MODEL), F32),
        compiler_params=_params(1),
        name="attn_sample",
    )(q, ck, cv)


def _row(v):
    return v.reshape(1, -1)


def _per_channel(v):
    return jnp.repeat(v, SSD_HEAD_DIM).reshape(1, D_INNER)


def _pad_lanes(v):
    return jnp.pad(v, (0, LANES - v.shape[0])).reshape(1, LANES)


def kernel(x_prompt, x_sample, mem_prompt, cache_mem_k, cache_mem_v, state_ssm, state_conv, state_pool, norm_ffn1, ffn1_w_gate, ffn1_w_up, ffn1_w_down, norm_mix, ssd_in_w, ssd_conv_w, ssd_conv_b, ssd_dt_bias, ssd_a_log, ssd_d, ssd_norm_w, ssd_out_w, pool_w, pool_scale, norm_cross, norm_mem, xa_wq, xa_wk, xa_wv, xa_wo, norm_ffn2, ffn2_w_gate, ffn2_w_up, ffn2_w_down, final_norm):
    bf = lambda w: w.astype(BF16)
    mk_p, mv_p = _mem_kv(mem_prompt.reshape(BATCH * N_MEM, D_MODEL), norm_mem.reshape(DEPTH, 1, D_MODEL),
                         bf(xa_wk), bf(xa_wv))
    ck = _chunk_rows(cache_mem_k)
    cv = _chunk_rows(cache_mem_v)
    st_s = state_ssm.reshape(-1, DEC_BATCH, D_INNER, D_STATE)
    conv0 = jnp.zeros((BATCH, CONV_W - 1, CONV_DIM), F32)
    ssm0 = jnp.zeros((BATCH, D_INNER, D_STATE), F32)
    pool0 = jnp.zeros((BATCH, POOL_BUF, D_MODEL), F32)

    ssm_p, conv_p, pool_p, conv_s, pool_s = [], [], [], [], []
    ssm_s = None
    for i in range(DEPTH):
        j = i // 2
        ffn1 = (_row(norm_ffn1[i]), bf(ffn1_w_gate[i]), bf(ffn1_w_up[i]), bf(ffn1_w_down[i]))
        if i == 0:
            x = _ffn(x_prompt.reshape(ROWS_P, D_MODEL), *ffn1, out_rows=ROWS)
            x = _ffn(x_sample.reshape(ROWS_S, D_MODEL), *ffn1, out_rows=ROWS, out_row_start=ROWS_P, into=x)
        else:
            x = _ffn(x, *ffn1)
        g_mix = _row(norm_mix[i])
        if i % 2 == 0:
            w_in = ssd_in_w[j]
            w_dt = w_in[:, DT_OFF:]
            w_p = bf(jnp.concatenate(
                [w_in[:, :DT_OFF], jnp.pad(w_dt, ((0, 0), (0, IN_PAD_P - DT_OFF - SSD_HEADS)))], axis=1))
            w_s = bf(jnp.concatenate([w_in[:, :DT_OFF], jnp.repeat(w_dt, SSD_HEAD_DIM, axis=1)], axis=1))
            cw, cb = ssd_conv_w[j], _row(ssd_conv_b[j])
            dsk, nw, w_out = _per_channel(ssd_d[j]), _row(ssd_norm_w[j]), bf(ssd_out_w[j])

            x_p, nc_p, ns_p = _ssd_prompt(x, g_mix, w_p, conv0, ssm0, cw, cb, _pad_lanes(ssd_dt_bias[j]),
                                          _pad_lanes(ssd_a_log[j]), dsk, nw, w_out)
            zx_s = _rms_matmul(x, g_mix, w_s, row_start=ROWS_P, n_rows=ROWS_S, tm=1024, tn=1792,
                               name="in_proj_s")
            y_s, nc_s, ssm_s = _ssd_sample(zx_s, state_conv, st_s, j, cw, cb, _per_channel(ssd_dt_bias[j]),
                                           _per_channel(ssd_a_log[j]), dsk, nw, ssm_s)
            x = _matmul_residual(x, y_s, w_out, row_start=ROWS_P, tm=1024, name="out_proj_s", into=x_p)
            conv_p.append(nc_p)
            conv_s.append(nc_s)
            ssm_p.append(ns_p.reshape(BATCH, SSD_HEADS, SSD_HEAD_DIM, D_STATE))
        else:
            pw, ps = bf(pool_w[j]), _row(pool_scale[j])
            x, np_p = _pool(x, g_mix, pool0, lambda tb: pl.BlockSpec((tb, POOL_BUF, D_MODEL), lambda b, l: (b, 0, 0)),
                            pw, ps, row_start=0, nb=BATCH, seq=SEQ, tb=1, tl=512, pos0=0, name="pool_p")
            x, np_s = _pool(x, g_mix, state_pool,
                            lambda tb: pl.BlockSpec((None, tb, POOL_BUF, D_MODEL), lambda b, l: (j, b, 0, 0)),
                            pw, ps, row_start=ROWS_P, nb=DEC_BATCH, seq=DEC_SEQ, tb=8, tl=DEC_SEQ,
                            pos0=PAST_LEN, name="pool_s")
            pool_p.append(np_p)
            pool_s.append(np_s)

        g_x = _row(norm_cross[i])
        wq, wo = bf(xa_wq[i]), bf(xa_wo[i])
        x = _xattn_prompt(x, g_x, wq, mk_p, mv_p, wo, i)
        q_s = _rms_matmul(x, g_x, wq, row_start=ROWS_P, n_rows=ROWS_S, tm=1024, tn=1024, name="q_proj_s")
        o_s = _attn_sample(q_s, ck, cv, i)
        x = _matmul_residual(x, o_s, wo, row_start=ROWS_P, tm=1024, name="wo_s")

        ffn2 = (_row(norm_ffn2[i]), bf(ffn2_w_gate[i]), bf(ffn2_w_up[i]), bf(ffn2_w_down[i]))
        if i < DEPTH - 1:
            x = _ffn(x, *ffn2)

    g_f = _row(final_norm)
    y_prompt = _ffn(x, *ffn2, n_rows=ROWS_P, final_g=g_f).reshape(BATCH, SEQ, D_MODEL)
    y_sample = _ffn(x, *ffn2, src_row_start=ROWS_P, n_rows=ROWS_S, final_g=g_f).reshape(DEC_BATCH, DEC_SEQ, D_MODEL)
    return (y_prompt, y_sample, jnp.stack(ssm_p), jnp.stack(conv_p), jnp.stack(pool_p),
            _from_chunk_rows(mk_p), _from_chunk_rows(mv_p),
            ssm_s.reshape(state_ssm.shape), jnp.stack(conv_s), jnp.stack(pool_s))
```

```python
import functools

import jax
import jax.numpy as jnp
from jax import lax
from jax.experimental import pallas as pl
from jax.experimental.pallas import tpu as pltpu

F32 = jnp.float32
BF16 = jnp.bfloat16

D_MODEL = 1024
BATCH = 8
SEQ = 2048
DEPTH = 4
DEC_BATCH = 128
DEC_SEQ = 8
PAST_LEN = 16384
D_FF = 2816
D_INNER = 2048
SSD_HEAD_DIM = 64
SSD_HEADS = 32
SSD_GROUPS = 4
HEADS_PER_GROUP = 8
D_STATE = 128
CONV_W = 4
GN = SSD_GROUPS * D_STATE
CONV_DIM = D_INNER + 2 * GN
SSD_CHUNK = 128
POOL_WINDOWS = (2, 4, 8, 16)
POOL_GW = 256
POOL_BUF = 15
N_MEM = 256
MEM_HEADS = 4
MEM_HEAD_DIM = 256
EPS = 1e-5

ROWS_P = BATCH * SEQ
ROWS_S = DEC_BATCH * DEC_SEQ
ROWS = ROWS_P + ROWS_S

LANES = 128
XBC_OFF = D_INNER
DT_OFF = D_INNER + CONV_DIM
IN_PAD_P = 5376
IN_PAD_S = DT_OFF + D_INNER
VMEM_LIMIT = 48 * 1024 * 1024
MXU_TILE = 256
FFN_CHUNK = MXU_TILE
MEM_CHUNKS = D_MODEL // LANES


def _params(n_axes, vmem=VMEM_LIMIT):
    return pltpu.CompilerParams(dimension_semantics=("arbitrary",) * n_axes,
                                vmem_limit_bytes=vmem)


def _rms(x, g):
    r = lax.rsqrt(jnp.mean(x * x, axis=-1, keepdims=True) + EPS)
    return x * r * g


def _silu(x):
    return x * (1.0 / (1.0 + jnp.exp(-x)))


def _softplus(x):
    return jnp.maximum(x, 0.0) + jnp.log(1.0 + jnp.exp(-jnp.abs(x)))


def _dot(a, b):
    return jnp.dot(a, b, preferred_element_type=F32)


def _ffn_kernel(*refs, final, n_first):
    x_ref, g_ref, wg_ref, wu_ref, wd_ref = refs[:5]
    o_ref, h_ref = refs[-2:]
    x = x_ref[...]
    if n_first is not None:
        x = jnp.where(pl.program_id(0) < n_first, x, refs[-3][...])
    u = _rms(x, g_ref[...]).astype(BF16)
    for lo in range(0, D_FF, FFN_CHUNK):
        a = _dot(u, wg_ref[:, lo:lo + FFN_CHUNK])
        b = _dot(u, wu_ref[:, lo:lo + FFN_CHUNK])
        h_ref[:, lo:lo + FFN_CHUNK] = (_silu(a) * b).astype(BF16)
    y = x + 0.5 * _dot(h_ref[...], wd_ref[...])
    o_ref[...] = _rms(y, refs[5][...]) if final else y


def _resident(shape):
    return pl.BlockSpec(shape, lambda *_: (0,) * len(shape), pipeline_mode=pl.Buffered(1))


def _ffn(src, g, wg, wu, wd, *, src_row_start=0, n_rows=None, src2=None, final_g=None, tm=1024):
    n_rows = src.shape[0] if n_rows is None else n_rows
    s0, n1 = src_row_start // tm, n_rows // tm
    in_specs = [pl.BlockSpec((tm, D_MODEL), lambda i: (s0 + jnp.minimum(i, n1 - 1), 0)), _resident((1, D_MODEL)),
                _resident((D_MODEL, D_FF)), _resident((D_MODEL, D_FF)), _resident((D_FF, D_MODEL))]
    args = [src, g, wg, wu, wd]
    if final_g is not None:
        in_specs.append(_resident((1, D_MODEL)))
        args.append(final_g)
    n2 = 0
    if src2 is not None:
        n2 = src2.shape[0] // tm
        in_specs.append(pl.BlockSpec((tm, D_MODEL), lambda i: (jnp.maximum(i - n1, 0), 0)))
        args.append(src2)
    return pl.pallas_call(
        functools.partial(_ffn_kernel, final=final_g is not None, n_first=n1 if src2 is not None else None),
        grid=(n1 + n2,),
        in_specs=in_specs,
        out_specs=pl.BlockSpec((tm, D_MODEL), lambda i: (i, 0)),
        out_shape=jax.ShapeDtypeStruct(((n1 + n2) * tm, D_MODEL), F32),
        scratch_shapes=[pltpu.VMEM((tm, D_FF), BF16)],
        compiler_params=_params(1),
        name="ffn",
    )(*args)


def _rms_mm_kernel(x_ref, g_ref, w_ref, o_ref, u_ref):
    @pl.when(pl.program_id(1) == 0)
    def _():
        u_ref[...] = _rms(x_ref[...], g_ref[...]).astype(BF16)

    o_ref[...] = _dot(u_ref[...], w_ref[...])


def _rms_matmul(x, g, w, *, row_start, n_rows, tm, tn, name):
    n = w.shape[1]
    r0 = row_start // tm
    return pl.pallas_call(
        _rms_mm_kernel,
        grid=(n_rows // tm, n // tn),
        in_specs=[
            pl.BlockSpec((tm, D_MODEL), lambda i, j: (r0 + i, 0)),
            pl.BlockSpec((1, D_MODEL), lambda i, j: (0, 0)),
            pl.BlockSpec((D_MODEL, tn), lambda i, j: (0, j)),
        ],
        out_specs=pl.BlockSpec((tm, tn), lambda i, j: (i, j)),
        out_shape=jax.ShapeDtypeStruct((n_rows, n), F32),
        scratch_shapes=[pltpu.VMEM((tm, D_MODEL), BF16)],
        compiler_params=_params(2),
        name=name,
    )(x, g, w)


def _mem_kv_kernel(m_ref, g_ref, wk_ref, wv_ref, k_ref, v_ref, *, tm):
    u = _rms(m_ref[...], g_ref[...]).astype(BF16)
    for w_ref, o_ref in ((wk_ref, k_ref), (wv_ref, v_ref)):
        kv = _dot(u, w_ref[...])
        for j in range(MEM_CHUNKS):
            o_ref[pl.ds(j, tm, stride=MEM_CHUNKS), :] = kv[:, _chunk_to_col(j):_chunk_to_col(j) + LANES]


def _mem_kv(mem, g, wk, wv, *, tm=512):
    rows = mem.shape[0]
    w_spec = pl.BlockSpec((None, D_MODEL, D_MODEL), lambda l, r: (l, 0, 0))
    o_spec = pl.BlockSpec((None, tm * MEM_CHUNKS, LANES), lambda l, r: (l, r, 0))
    o_shape = jax.ShapeDtypeStruct((DEPTH, rows * MEM_CHUNKS, LANES), F32)
    return pl.pallas_call(
        functools.partial(_mem_kv_kernel, tm=tm),
        grid=(DEPTH, rows // tm),
        in_specs=[
            pl.BlockSpec((tm, D_MODEL), lambda l, r: (r, 0)),
            pl.BlockSpec((None, 1, D_MODEL), lambda l, r: (l, 0, 0)),
            w_spec, w_spec,
        ],
        out_specs=[o_spec, o_spec],
        out_shape=[o_shape, o_shape],
        compiler_params=_params(2),
        name="mem_kv",
    )(mem, g, wk, wv)


def _mm_res_kernel(x_ref, y_ref, w_ref, o_ref):
    o_ref[...] = x_ref[...] + _dot(y_ref[...].astype(BF16), w_ref[...])


def _matmul_residual(x, y, w, *, row_start, tm, name):
    n_rows, k = y.shape
    r0 = row_start // tm
    return pl.pallas_call(
        _mm_res_kernel,
        grid=(n_rows // tm,),
        in_specs=[
            pl.BlockSpec((tm, D_MODEL), lambda i: (r0 + i, 0)),
            pl.BlockSpec((tm, k), lambda i: (i, 0)),
            pl.BlockSpec((k, D_MODEL), lambda i: (0, 0)),
        ],
        out_specs=pl.BlockSpec((tm, D_MODEL), lambda i: (r0 + i, 0)),
        out_shape=jax.ShapeDtypeStruct(x.shape, F32),
        input_output_aliases={0: 0},
        compiler_params=_params(1),
        name=name,
    )(x, y, w)


CONV_LEAD = 8
CONV_TAIL0 = CONV_LEAD - (CONV_W - 1)


def _scan_chunk(zx, x_res, cw_ref, cb_ref, dtb_ref, alog_ref, dsk_ref, nw_ref, wout_ref,
                xpad_ref, ht_ref, ybuf_ref):
    q = SSD_CHUNK
    lead, tail0 = CONV_LEAD, CONV_TAIL0
    xpad_ref[lead:lead + q, :] = zx[:, XBC_OFF:DT_OFF]
    conv = cw_ref[0:1, :] * xpad_ref[tail0:tail0 + q, :]
    for k in range(1, CONV_W):
        conv = conv + cw_ref[k:k + 1, :] * xpad_ref[tail0 + k:tail0 + k + q, :]
    conv = cb_ref[...] + conv
    tail = xpad_ref[q + tail0:q + lead, :]
    xpad_ref[tail0:lead, :] = tail

    xbc = _silu(conv)
    xs = xbc[:, :D_INNER]
    bm = xbc[:, D_INNER:D_INNER + GN]
    cm = xbc[:, D_INNER + GN:]

    dt = _softplus(zx[:, DT_OFF:DT_OFF + LANES] + dtb_ref[...])
    a = -jnp.exp(alog_ref[...])
    da = dt * a
    rows = lax.broadcasted_iota(jnp.int32, (q, q), 0)
    cols = lax.broadcasted_iota(jnp.int32, (q, q), 1)
    causal = rows >= cols
    tri = jnp.where(causal, 1.0, 0.0).astype(F32)
    acs = jnp.dot(tri, da, precision=lax.Precision.HIGHEST, preferred_element_type=F32)
    acs_t = acs.T
    dt_t = dt.T
    src_t = acs_t - jnp.log(dt_t)
    eacs = jnp.exp(acs)
    last = acs_t[:, q - 1:q]
    w_t = jnp.exp(last - acs_t) * dt_t
    cd_t = jnp.exp(last)
    low_half = lax.broadcasted_iota(jnp.int32, (1, LANES), 1) < SSD_HEAD_DIM

    for g in range(SSD_GROUPS):
        bm_g = bm[:, g * D_STATE:(g + 1) * D_STATE]
        cm_g = cm[:, g * D_STATE:(g + 1) * D_STATE]
        bm_gt = bm_g.T
        cb_g = _dot(cm_g.astype(BF16), bm_gt.astype(BF16))
        for hh in range(0, HEADS_PER_GROUP, 2):
            h0 = g * HEADS_PER_GROUP + hh
            lo = (h0 // 2) * LANES
            x_pair = xs[:, lo:lo + LANES]
            ht_pair = ht_ref[:, lo:lo + LANES]
            x_pair_b = x_pair.astype(BF16)
            rhs = jnp.concatenate([x_pair_b, ht_pair.astype(BF16)], axis=0)
            ys, sts = [], []
            for h in (h0, h0 + 1):
                seg = acs[:, h:h + 1] - src_t[h:h + 1, :]
                m_h = cb_g * jnp.exp(jnp.where(causal, seg, -jnp.inf))
                e_h = eacs[:, h:h + 1] * cm_g
                lhs = jnp.concatenate([m_h.astype(BF16), e_h.astype(BF16)], axis=1)
                ys.append(_dot(lhs, rhs))
                b_h = bm_gt * w_t[h:h + 1, :]
                sts.append(_dot(b_h.astype(BF16), x_pair_b))
            y_pair = jnp.where(low_half, ys[0], ys[1]) + x_pair * dsk_ref[:, lo:lo + LANES]
            st_pair = jnp.where(low_half, sts[0], sts[1])
            cd_row = jnp.where(low_half, cd_t[h0:h0 + 1, :], cd_t[h0 + 1:h0 + 2, :])
            ht_ref[:, lo:lo + LANES] = ht_pair * cd_row + st_pair
            ybuf_ref[:, lo:lo + LANES] = y_pair

    y = ybuf_ref[...] * _silu(zx[:, :D_INNER])
    gw = D_INNER // SSD_GROUPS
    y_norm = []
    for g in range(SSD_GROUPS):
        yg = y[:, g * gw:(g + 1) * gw]
        yg = yg * lax.rsqrt(jnp.mean(yg * yg, axis=-1, keepdims=True) + EPS)
        y_norm.append((yg * nw_ref[:, g * gw:(g + 1) * gw]).astype(BF16))
    return x_res + _dot(jnp.concatenate(y_norm, axis=1), wout_ref[...]), tail


def _ssd_prompt_kernel(x_ref, g_ref, win_ref, cbuf_ref, st_ref, cw_ref, cb_ref, dtb_ref, alog_ref,
                       dsk_ref, nw_ref, wout_ref, o_ref, nconv_ref, nst_ref,
                       za_ref, zb_ref, xpad_ref, ht_ref, ybuf_ref, *, ns):
    s = pl.program_id(1)
    q = SSD_CHUNK
    xa_ref = x_ref.at[0:2 * q]
    xb_ref = x_ref.at[2 * q:3 * q]

    def project(x_rows, z_ref):
        z_ref[...] = _dot(_rms(x_rows, g_ref[...]).astype(BF16), win_ref[...])

    scan = functools.partial(_scan_chunk, cw_ref=cw_ref, cb_ref=cb_ref, dtb_ref=dtb_ref, alog_ref=alog_ref,
                             dsk_ref=dsk_ref, nw_ref=nw_ref, wout_ref=wout_ref,
                             xpad_ref=xpad_ref, ht_ref=ht_ref, ybuf_ref=ybuf_ref)

    @pl.when(s == 0)
    def _():
        xpad_ref[CONV_TAIL0:CONV_LEAD, :] = cbuf_ref[...]
        ht_ref[...] = st_ref[...].T
        project(xa_ref[0:q, :], za_ref)

    x1 = xa_ref[q:2 * q, :]
    project(x1, zb_ref)
    o_ref[0:q, :], _ = scan(za_ref, xa_ref[0:q, :])
    project(xb_ref[...], za_ref)
    o_ref[q:2 * q, :], tail = scan(zb_ref, x1)

    @pl.when(s == ns - 1)
    def _():
        nconv_ref[...] = tail
        nst_ref[...] = ht_ref[...].T


def _ssd_prompt(x, g, w_in, cbuf, st, cw, cb, dtb, alog, dsk, nw, w_out):
    nb = cbuf.shape[0]
    nc = SEQ // SSD_CHUNK
    ns = nc // 2
    q = SSD_CHUNK
    assert x.shape[0] >= nb * SEQ + q
    vec = lambda n: pl.BlockSpec((1, n), lambda b, c: (0, 0))
    return pl.pallas_call(
        functools.partial(_ssd_prompt_kernel, ns=ns),
        grid=(nb, ns),
        in_specs=[
            pl.BlockSpec((pl.Element(3 * q), pl.Element(D_MODEL)), lambda b, s: ((b * ns + s) * 2 * q, 0)),
            vec(D_MODEL),
            _resident((D_MODEL, IN_PAD_P)),
            pl.BlockSpec((None, CONV_W - 1, CONV_DIM), lambda b, c: (b, 0, 0)),
            pl.BlockSpec((None, D_INNER, D_STATE), lambda b, c: (b, 0, 0)),
            pl.BlockSpec((CONV_W, CONV_DIM), lambda b, c: (0, 0)),
            vec(CONV_DIM), vec(LANES), vec(LANES), vec(D_INNER), vec(D_INNER),
            _resident((D_INNER, D_MODEL)),
        ],
        out_specs=[
            pl.BlockSpec((2 * q, D_MODEL), lambda b, s: (b * ns + s, 0)),
            pl.BlockSpec((None, CONV_W - 1, CONV_DIM), lambda b, c: (b, 0, 0)),
            pl.BlockSpec((None, D_INNER, D_STATE), lambda b, c: (b, 0, 0)),
        ],
        out_shape=[
            jax.ShapeDtypeStruct(x.shape, F32),
            jax.ShapeDtypeStruct((nb, CONV_W - 1, CONV_DIM), F32),
            jax.ShapeDtypeStruct((nb, D_INNER, D_STATE), F32),
        ],
        scratch_shapes=[
            pltpu.VMEM((q, IN_PAD_P), F32),
            pltpu.VMEM((q, IN_PAD_P), F32),
            pltpu.VMEM((q + CONV_LEAD, CONV_DIM), F32),
            pltpu.VMEM((D_STATE, D_INNER), F32),
            pltpu.VMEM((q, D_INNER), F32),
        ],
        input_output_aliases={0: 0},
        compiler_params=_params(2),
        name="ssd_prompt",
    )(x, g, w_in, cbuf, st, cw, cb, dtb, alog, dsk, nw, w_out)


def _ssd_sample_kernel(zx_ref, cbuf_ref, st_ref, cw_ref, cb_ref, dtb_ref, alog_ref, dsk_ref, nw_ref, *rest, tb):
    y_ref, nconv_ref, nst_ref, xpad_ref = rest[-4:]
    q = DEC_SEQ
    lead = 8
    tail0 = lead - (CONV_W - 1)
    gw = D_INNER // SSD_GROUPS
    a = -jnp.exp(alog_ref[...])
    rows = lax.broadcasted_iota(jnp.int32, (q, 1), 0)
    xws, cds, bms = [], [], []
    for b in range(tb):
        r0 = b * q
        xpad_ref[b, tail0:lead, :] = cbuf_ref[b]
        xpad_ref[b, lead:lead + q, :] = zx_ref[r0:r0 + q, XBC_OFF:DT_OFF]
        conv = cw_ref[0:1, :] * xpad_ref[b, tail0:tail0 + q, :]
        for k in range(1, CONV_W):
            conv = conv + cw_ref[k:k + 1, :] * xpad_ref[b, tail0 + k:tail0 + k + q, :]
        conv = cb_ref[...] + conv
        nconv_ref[b] = xpad_ref[b, q + tail0:q + lead, :]

        xbc = _silu(conv)
        xs = xbc[:, :D_INNER]
        bm = xbc[:, D_INNER:D_INNER + GN]
        cm = xbc[:, D_INNER + GN:]

        dt = _softplus(zx_ref[r0:r0 + q, DT_OFF:DT_OFF + D_INNER] + dtb_ref[...])
        da = dt * a
        acs = jnp.where(rows >= 0, da[0:1, :], 0.0)
        for k in range(1, q):
            acs = acs + jnp.where(rows >= k, da[k:k + 1, :], 0.0)
        xdt = xs * dt

        y = jnp.zeros((q, D_INNER), F32)
        for k in range(q):
            decay = jnp.exp(jnp.where(rows >= k, acs - acs[k:k + 1, :], -jnp.inf))
            cb_k = jnp.concatenate(
                [jnp.broadcast_to(
                    jnp.sum(cm[:, g * D_STATE:(g + 1) * D_STATE] * bm[k:k + 1, g * D_STATE:(g + 1) * D_STATE],
                            axis=-1, keepdims=True), (q, gw)) for g in range(SSD_GROUPS)], axis=1)
            y = y + cb_k * decay * xdt[k:k + 1, :]

        cm_rows = jnp.concatenate([cm[:, g * D_STATE:(g + 1) * D_STATE] for g in range(SSD_GROUPS)], axis=0)
        r = lax.dot_general(cm_rows.astype(BF16), st_ref[b].astype(BF16), (((1,), (1,)), ((), ())),
                            preferred_element_type=F32)
        y_off = jnp.concatenate([r[g * q:(g + 1) * q, g * gw:(g + 1) * gw] for g in range(SSD_GROUPS)], axis=1)
        y = y + jnp.exp(acs) * y_off
        y = y + xs * dsk_ref[...]

        y = y * _silu(zx_ref[r0:r0 + q, :D_INNER])
        for g in range(SSD_GROUPS):
            yg = y[:, g * gw:(g + 1) * gw]
            yg = yg * lax.rsqrt(jnp.mean(yg * yg, axis=-1, keepdims=True) + EPS)
            y_ref[r0:r0 + q, g * gw:(g + 1) * gw] = yg * nw_ref[:, g * gw:(g + 1) * gw]

        last = acs[q - 1:q, :]
        xws.append(xdt * jnp.exp(last - acs))
        cds.append(jnp.exp(last))
        bms.append(bm)

    padded = jnp.concatenate(xws + [jnp.zeros((LANES - tb * q, D_INNER), F32)], axis=0)
    padded_t = padded.T.astype(BF16)
    zero_blk = jnp.zeros((q, D_STATE), F32)
    for b0 in range(0, tb, 2):
        for g in range(SSD_GROUPS):
            blocks = []
            for i in range(LANES // q):
                left = bms[b0][:, g * D_STATE:(g + 1) * D_STATE] if i == b0 else zero_blk
                right = bms[b0 + 1][:, g * D_STATE:(g + 1) * D_STATE] if i == b0 + 1 else zero_blk
                blocks.append(jnp.concatenate([left, right], axis=1))
            rhs = jnp.concatenate(blocks, axis=0).astype(BF16)
            st_pair = _dot(padded_t[g * gw:(g + 1) * gw, :], rhs)
            for half, b in enumerate((b0, b0 + 1)):
                for hh in range(HEADS_PER_GROUP):
                    lo = g * gw + hh * SSD_HEAD_DIM
                    cd_h = jnp.broadcast_to(cds[b][:, lo:lo + 1], (SSD_HEAD_DIM, D_STATE))
                    nst_ref[b, lo:lo + SSD_HEAD_DIM, :] = (
                        st_ref[b, lo:lo + SSD_HEAD_DIM, :] * cd_h
                        + st_pair[hh * SSD_HEAD_DIM:(hh + 1) * SSD_HEAD_DIM, half * D_STATE:(half + 1) * D_STATE])


def _ssd_sample(zx, cbuf, st, layer, cw, cb, dtb, alog, dsk, nw, prev_states, *, tb=8):
    q = DEC_SEQ
    vec = lambda n: pl.BlockSpec((1, n), lambda i: (0, 0))
    n_layers = st.shape[0]
    in_specs = [
        pl.BlockSpec((tb * q, IN_PAD_S), lambda i: (i, 0)),
        pl.BlockSpec((None, tb, CONV_W - 1, CONV_DIM), lambda i: (layer, i, 0, 0)),
        pl.BlockSpec((None, tb, D_INNER, D_STATE), lambda i: (layer, i, 0, 0)),
        pl.BlockSpec((CONV_W, CONV_DIM), lambda i: (0, 0)),
        vec(CONV_DIM), vec(D_INNER), vec(D_INNER), vec(D_INNER), vec(D_INNER),
    ]
    args = [zx, cbuf, st, cw, cb, dtb, alog, dsk, nw]
    aliases = {}
    if prev_states is not None:
        in_specs.append(pl.BlockSpec(memory_space=pl.ANY))
        args.append(prev_states)
        aliases = {len(args) - 1: 2}
    return pl.pallas_call(
        functools.partial(_ssd_sample_kernel, tb=tb),
        grid=(DEC_BATCH // tb,),
        in_specs=in_specs,
        out_specs=[
            pl.BlockSpec((tb * q, D_INNER), lambda i: (i, 0)),
            pl.BlockSpec((tb, CONV_W - 1, CONV_DIM), lambda i: (i, 0, 0)),
            pl.BlockSpec((None, tb, D_INNER, D_STATE), lambda i: (layer, i, 0, 0)),
        ],
        out_shape=[
            jax.ShapeDtypeStruct((ROWS_S, D_INNER), F32),
            jax.ShapeDtypeStruct((DEC_BATCH, CONV_W - 1, CONV_DIM), F32),
            jax.ShapeDtypeStruct((n_layers, DEC_BATCH, D_INNER, D_STATE), F32),
        ],
        scratch_shapes=[pltpu.VMEM((tb, q + 8, CONV_DIM), F32)],
        input_output_aliases=aliases,
        compiler_params=_params(1),
        name="ssd_sample",
    )(*args)


def _pool_kernel(x_ref, g_ref, buf_ref, pw_ref, ps_ref, o_ref, np_ref, xx_ref, *, tb, tl, nl, pos0):
    l = pl.program_id(1)
    x = x_ref[...]
    u = _rms(x, g_ref[...]).reshape(tb, tl, D_MODEL)
    lead = POOL_BUF + 1

    @pl.when(l == 0)
    def _():
        xx_ref[:, 1:lead, :] = buf_ref[...]

    xx_ref[:, lead:lead + tl, :] = u
    pos = (pos0 + l * tl + lax.broadcasted_iota(jnp.int32, (1, tl, 1), 1)).astype(F32)
    outs = []
    for gi, w in enumerate(POOL_WINDOWS):
        lo = gi * POOL_GW
        s = xx_ref[:, lead:lead + tl, lo:lo + POOL_GW]
        for j in range(1, w):
            s = s + xx_ref[:, lead - j:lead - j + tl, lo:lo + POOL_GW]
        cnt = jnp.minimum(pos + 1.0, float(w))
        mix = s / cnt - u[:, :, lo:lo + POOL_GW]
        outs.append(_dot(mix.reshape(tb * tl, POOL_GW).astype(BF16), pw_ref[gi]))
    o_ref[...] = x + jnp.concatenate(outs, axis=-1) * ps_ref[...]

    tail = xx_ref[:, tl + 1:tl + lead, :]
    if nl > 1:
        xx_ref[:, 1:lead, :] = tail

    @pl.when(l == nl - 1)
    def _():
        np_ref[...] = tail


def _pool(x, g, buf_arr, buf_index, pw, ps, *, row_start, nb, seq, tb, tl, pos0, name):
    nl = seq // tl
    r0 = row_start // (tb * tl)
    x_spec = pl.BlockSpec((tb * tl, D_MODEL), lambda b, l: (r0 + b * nl + l, 0))
    return pl.pallas_call(
        functools.partial(_pool_kernel, tb=tb, tl=tl, nl=nl, pos0=pos0),
        grid=(nb // tb, nl),
        in_specs=[
            x_spec,
            pl.BlockSpec((1, D_MODEL), lambda b, l: (0, 0)),
            buf_index(tb),
            pl.BlockSpec((len(POOL_WINDOWS), POOL_GW, POOL_GW), lambda b, l: (0, 0, 0)),
            pl.BlockSpec((1, D_MODEL), lambda b, l: (0, 0)),
        ],
        out_specs=[x_spec, pl.BlockSpec((tb, POOL_BUF, D_MODEL), lambda b, l: (b, 0, 0))],
        out_shape=[jax.ShapeDtypeStruct(x.shape, F32),
                   jax.ShapeDtypeStruct((nb, POOL_BUF, D_MODEL), F32)],
        scratch_shapes=[pltpu.VMEM((tb, tl + POOL_BUF + 1, D_MODEL), F32)],
        input_output_aliases={0: 0},
        compiler_params=_params(2),
        name=name,
    )(x, g, buf_arr, pw, ps)


def _head_kv(ref, h):
    return jnp.concatenate([ref[pl.ds(c * MEM_HEADS + h, N_MEM, stride=MEM_CHUNKS), :]
                            for c in range(MEM_HEAD_DIM // LANES)], axis=1).astype(BF16)


def _xattn_prompt_kernel(x_ref, g_ref, wq_ref, k_ref, v_ref, wo_ref, o_ref):
    x = x_ref[...]
    u = _rms(x, g_ref[...]).astype(BF16)
    q = (_dot(u, wq_ref[...]) * (MEM_HEAD_DIM ** -0.5)).astype(BF16)
    outs = []
    for h in range(MEM_HEADS):
        q_h = q[:, h * MEM_HEAD_DIM:(h + 1) * MEM_HEAD_DIM]
        s = lax.dot_general(q_h, _head_kv(k_ref, h), (((1,), (1,)), ((), ())), preferred_element_type=F32)
        e = jnp.exp(s - jnp.max(s, axis=-1, keepdims=True))
        den = jnp.sum(e, axis=-1, keepdims=True)
        outs.append(_dot(e.astype(BF16), _head_kv(v_ref, h)) / den)
    o = jnp.concatenate(outs, axis=-1)
    o_ref[...] = x + _dot(o.astype(BF16), wo_ref[...])


def _xattn_prompt(x, g, wq, mk, mv, wo, layer, *, tl=1024):
    nl = SEQ // tl
    x_spec = pl.BlockSpec((tl, D_MODEL), lambda b, l: (b * nl + l, 0))
    w_spec = pl.BlockSpec((D_MODEL, D_MODEL), lambda b, l: (0, 0))
    kv_spec = pl.BlockSpec((None, N_MEM * MEM_CHUNKS, LANES), lambda b, l: (layer, b, 0))
    return pl.pallas_call(
        _xattn_prompt_kernel,
        grid=(BATCH, nl),
        in_specs=[x_spec, pl.BlockSpec((1, D_MODEL), lambda b, l: (0, 0)), w_spec, kv_spec, kv_spec, w_spec],
        out_specs=x_spec,
        out_shape=jax.ShapeDtypeStruct(x.shape, F32),
        input_output_aliases={0: 0},
        compiler_params=_params(2),
        name="xattn_prompt",
    )(x, g, wq, mk, mv, wo)


def _chunk_rows(t):
    lead = t.shape[:-3]
    t = t.reshape(*lead, N_MEM, MEM_HEADS, MEM_HEAD_DIM // LANES, LANES)
    return jnp.swapaxes(t, -2, -3).reshape(*lead, N_MEM * MEM_CHUNKS, LANES)


def _from_chunk_rows(t):
    t = t.reshape(DEPTH, BATCH, N_MEM, MEM_HEAD_DIM // LANES, MEM_HEADS, LANES)
    return jnp.swapaxes(t, -2, -3).reshape(DEPTH, BATCH, N_MEM, MEM_HEADS, MEM_HEAD_DIM)


def _chunk_to_col(j):
    return ((j % MEM_HEADS) * (MEM_HEAD_DIM // LANES) + j // MEM_HEADS) * LANES


def _attn_sample_kernel(q_ref, k_ref, v_ref, o_ref, *, tb):
    n = DEC_SEQ
    rows = lax.broadcasted_iota(jnp.int32, (MEM_HEADS * n, D_MODEL), 0) // n
    chunk = lax.broadcasted_iota(jnp.int32, (MEM_HEADS * n, D_MODEL), 1) // LANES
    own = rows == chunk % MEM_HEADS

    def gather(ref, b):
        return jnp.concatenate([ref[b, pl.ds(j, N_MEM, stride=MEM_CHUNKS), :] for j in range(MEM_CHUNKS)],
                               axis=1).astype(BF16)

    for b in range(tb):
        q_b = q_ref[b * n:(b + 1) * n, :] * (MEM_HEAD_DIM ** -0.5)
        q_c = jnp.concatenate([q_b[:, _chunk_to_col(j):_chunk_to_col(j) + LANES] for j in range(MEM_CHUNKS)], axis=1)
        q_blk = jnp.where(own, jnp.concatenate([q_c] * MEM_HEADS, axis=0), 0.0).astype(BF16)
        s = lax.dot_general(q_blk, gather(k_ref, b), (((1,), (1,)), ((), ())), preferred_element_type=F32)
        e = jnp.exp(s - jnp.max(s, axis=-1, keepdims=True))
        den = jnp.sum(e, axis=-1, keepdims=True)
        o = _dot(e.astype(BF16), gather(v_ref, b)) / den
        for j in range(MEM_CHUNKS):
            h = j % MEM_HEADS
            o_ref[b * n:(b + 1) * n, _chunk_to_col(j):_chunk_to_col(j) + LANES] = (
                o[h * n:(h + 1) * n, j * LANES:(j + 1) * LANES])


def _attn_sample(q, ck, cv, layer, *, tb=8):
    n = DEC_SEQ
    kv_spec = pl.BlockSpec((None, tb, N_MEM * MEM_CHUNKS, LANES), lambda i: (layer, i, 0, 0))
    row_spec = pl.BlockSpec((tb * n, D_MODEL), lambda i: (i, 0))
    return pl.pallas_call(
        functools.partial(_attn_sample_kernel, tb=tb),
        grid=(DEC_BATCH // tb,),
        in_specs=[row_spec, kv_spec, kv_spec],
        out_specs=row_spec,
        out_shape=jax.ShapeDtypeStruct((ROWS_S, D_MODEL), F32),
        compiler_params=_params(1),
        name="attn_sample",
    )(q, ck, cv)


def _row(v):
    return v.reshape(1, -1)


def _per_channel(v):
    return jnp.repeat(v, SSD_HEAD_DIM).reshape(1, D_INNER)


def _pad_lanes(v):
    return jnp.pad(v, (0, LANES - v.shape[0])).reshape(1, LANES)


def kernel(x_prompt, x_sample, mem_prompt, cache_mem_k, cache_mem_v, state_ssm, state_conv, state_pool, norm_ffn1, ffn1_w_gate, ffn1_w_up, ffn1_w_down, norm_mix, ssd_in_w, ssd_conv_w, ssd_conv_b, ssd_dt_bias, ssd_a_log, ssd_d, ssd_norm_w, ssd_out_w, pool_w, pool_scale, norm_cross, norm_mem, xa_wq, xa_wk, xa_wv, xa_wo, norm_ffn2, ffn2_w_gate, ffn2_w_up, ffn2_w_down, final_norm):
    bf = lambda w: w.astype(BF16)
    mk_p, mv_p = _mem_kv(mem_prompt.reshape(BATCH * N_MEM, D_MODEL), norm_mem.reshape(DEPTH, 1, D_MODEL),
                         bf(xa_wk), bf(xa_wv))
    ck = _chunk_rows(cache_mem_k)
    cv = _chunk_rows(cache_mem_v)
    st_s = state_ssm.reshape(-1, DEC_BATCH, D_INNER, D_STATE)
    conv0 = jnp.zeros((BATCH, CONV_W - 1, CONV_DIM), F32)
    ssm0 = jnp.zeros((BATCH, D_INNER, D_STATE), F32)
    pool0 = jnp.zeros((BATCH, POOL_BUF, D_MODEL), F32)

    ssm_p, conv_p, pool_p, conv_s, pool_s = [], [], [], [], []
    ssm_s = None
    for i in range(DEPTH):
        j = i // 2
        ffn1 = (_row(norm_ffn1[i]), bf(ffn1_w_gate[i]), bf(ffn1_w_up[i]), bf(ffn1_w_down[i]))
        if i == 0:
            x = _ffn(x_prompt.reshape(ROWS_P, D_MODEL), *ffn1, src2=x_sample.reshape(ROWS_S, D_MODEL))
        else:
            x = _ffn(x, *ffn1)
        g_mix = _row(norm_mix[i])
        if i % 2 == 0:
            w_in = ssd_in_w[j]
            w_dt = w_in[:, DT_OFF:]
            w_p = bf(jnp.concatenate(
                [w_in[:, :DT_OFF], jnp.pad(w_dt, ((0, 0), (0, IN_PAD_P - DT_OFF - SSD_HEADS)))], axis=1))
            w_s = bf(jnp.concatenate([w_in[:, :DT_OFF], jnp.repeat(w_dt, SSD_HEAD_DIM, axis=1)], axis=1))
            cw, cb = ssd_conv_w[j], _row(ssd_conv_b[j])
            dsk, nw, w_out = _per_channel(ssd_d[j]), _row(ssd_norm_w[j]), bf(ssd_out_w[j])

            x, nc_p, ns_p = _ssd_prompt(x, g_mix, w_p, conv0, ssm0, cw, cb, _pad_lanes(ssd_dt_bias[j]),
                                        _pad_lanes(ssd_a_log[j]), dsk, nw, w_out)
            zx_s = _rms_matmul(x, g_mix, w_s, row_start=ROWS_P, n_rows=ROWS_S, tm=1024, tn=1792,
                               name="in_proj_s")
            y_s, nc_s, ssm_s = _ssd_sample(zx_s, state_conv, st_s, j, cw, cb, _per_channel(ssd_dt_bias[j]),
                                           _per_channel(ssd_a_log[j]), dsk, nw, ssm_s)
            x = _matmul_residual(x, y_s, w_out, row_start=ROWS_P, tm=1024, name="out_proj_s")
            conv_p.append(nc_p)
            conv_s.append(nc_s)
            ssm_p.append(ns_p.reshape(BATCH, SSD_HEADS, SSD_HEAD_DIM, D_STATE))
        else:
            pw, ps = bf(pool_w[j]), _row(pool_scale[j])
            x, np_p = _pool(x, g_mix, pool0, lambda tb: pl.BlockSpec((tb, POOL_BUF, D_MODEL), lambda b, l: (b, 0, 0)),
                            pw, ps, row_start=0, nb=BATCH, seq=SEQ, tb=1, tl=1024, pos0=0, name="pool_p")
            x, np_s = _pool(x, g_mix, state_pool,
                            lambda tb: pl.BlockSpec((None, tb, POOL_BUF, D_MODEL), lambda b, l: (j, b, 0, 0)),
                            pw, ps, row_start=ROWS_P, nb=DEC_BATCH, seq=DEC_SEQ, tb=8, tl=DEC_SEQ,
                            pos0=PAST_LEN, name="pool_s")
            pool_p.append(np_p)
            pool_s.append(np_s)

        g_x = _row(norm_cross[i])
        wq, wo = bf(xa_wq[i]), bf(xa_wo[i])
        x = _xattn_prompt(x, g_x, wq, mk_p, mv_p, wo, i)
        q_s = _rms_matmul(x, g_x, wq, row_start=ROWS_P, n_rows=ROWS_S, tm=1024, tn=1024, name="q_proj_s")
        o_s = _attn_sample(q_s, ck, cv, i)
        x = _matmul_residual(x, o_s, wo, row_start=ROWS_P, tm=1024, name="wo_s")

        ffn2 = (_row(norm_ffn2[i]), bf(ffn2_w_gate[i]), bf(ffn2_w_up[i]), bf(ffn2_w_down[i]))
        if i < DEPTH - 1:
            x = _ffn(x, *ffn2)

    g_f = _row(final_norm)
    y_prompt = _ffn(x, *ffn2, n_rows=ROWS_P, final_g=g_f).reshape(BATCH, SEQ, D_MODEL)
    y_sample = _ffn(x, *ffn2, src_row_start=ROWS_P, n_rows=ROWS_S, final_g=g_f).reshape(DEC_BATCH, DEC_SEQ, D_MODEL)
    return (y_prompt, y_sample, jnp.stack(ssm_p), jnp.stack(conv_p), jnp.stack(pool_p),
            _from_chunk_rows(mk_p), _from_chunk_rows(mv_p),
            ssm_s.reshape(state_ssm.shape), jnp.stack(conv_s), jnp.stack(pool_s))
```

```python
import functools

import jax
import jax.numpy as jnp
from jax import lax
from jax.experimental import pallas as pl
from jax.experimental.pallas import tpu as pltpu

F32 = jnp.float32
BF16 = jnp.bfloat16

D_MODEL = 1024
BATCH = 8
SEQ = 2048
DEPTH = 4
DEC_BATCH = 128
DEC_SEQ = 8
PAST_LEN = 16384
D_FF = 2816
D_INNER = 2048
SSD_HEAD_DIM = 64
SSD_HEADS = 32
SSD_GROUPS = 4
HEADS_PER_GROUP = 8
D_STATE = 128
CONV_W = 4
GN = SSD_GROUPS * D_STATE
CONV_DIM = D_INNER + 2 * GN
SSD_CHUNK = 128
POOL_WINDOWS = (2, 4, 8, 16)
POOL_GW = 256
POOL_BUF = 15
N_MEM = 256
MEM_HEADS = 4
MEM_HEAD_DIM = 256
EPS = 1e-5

ROWS_P = BATCH * SEQ
ROWS_S = DEC_BATCH * DEC_SEQ
ROWS = ROWS_P + ROWS_S

LANES = 128
XBC_OFF = D_INNER
DT_OFF = D_INNER + CONV_DIM
IN_PAD_P = 5376
IN_PAD_S = DT_OFF + D_INNER
VMEM_LIMIT = 48 * 1024 * 1024
MXU_TILE = 256
FFN_CHUNK = MXU_TILE
MEM_CHUNKS = D_MODEL // LANES


def _params(n_axes, vmem=VMEM_LIMIT):
    return pltpu.CompilerParams(dimension_semantics=("arbitrary",) * n_axes,
                                vmem_limit_bytes=vmem)


def _rms(x, g):
    r = lax.rsqrt(jnp.mean(x * x, axis=-1, keepdims=True) + EPS)
    return x * r * g


def _silu(x):
    return x * (1.0 / (1.0 + jnp.exp(-x)))


def _softplus(x):
    return jnp.maximum(x, 0.0) + jnp.log(1.0 + jnp.exp(-jnp.abs(x)))


def _dot(a, b):
    return jnp.dot(a, b, preferred_element_type=F32)


def _ffn_kernel(*refs, final, n_first):
    x_ref, g_ref, wg_ref, wu_ref, wd_ref = refs[:5]
    o_ref, h_ref = refs[-2:]
    x = x_ref[...]
    if n_first is not None:
        x = jnp.where(pl.program_id(0) < n_first, x, refs[-3][...])
    u = _rms(x, g_ref[...]).astype(BF16)
    for lo in range(0, D_FF, FFN_CHUNK):
        a = _dot(u, wg_ref[:, lo:lo + FFN_CHUNK])
        b = _dot(u, wu_ref[:, lo:lo + FFN_CHUNK])
        h_ref[:, lo:lo + FFN_CHUNK] = (_silu(a) * b).astype(BF16)
    y = x + 0.5 * _dot(h_ref[...], wd_ref[...])
    o_ref[...] = _rms(y, refs[5][...]) if final else y


def _resident(shape, layer=None):
    if layer is None:
        return pl.BlockSpec(shape, lambda *_: (0,) * len(shape), pipeline_mode=pl.Buffered(1))
    return pl.BlockSpec((None,) + tuple(shape), lambda *_: (layer,) + (0,) * len(shape),
                        pipeline_mode=pl.Buffered(1))


def _ffn(src, g, wg, wu, wd, layer, *, src_row_start=0, n_rows=None, src2=None, final_g=None, tm=1024):
    n_rows = src.shape[0] if n_rows is None else n_rows
    s0, n1 = src_row_start // tm, n_rows // tm
    in_specs = [pl.BlockSpec((tm, D_MODEL), lambda i: (s0 + jnp.minimum(i, n1 - 1), 0)), _resident((1, D_MODEL)),
                _resident((D_MODEL, D_FF), layer), _resident((D_MODEL, D_FF), layer),
                _resident((D_FF, D_MODEL), layer)]
    args = [src, g, wg, wu, wd]
    if final_g is not None:
        in_specs.append(_resident((1, D_MODEL)))
        args.append(final_g)
    n2 = 0
    if src2 is not None:
        n2 = src2.shape[0] // tm
        in_specs.append(pl.BlockSpec((tm, D_MODEL), lambda i: (jnp.maximum(i - n1, 0), 0)))
        args.append(src2)
    return pl.pallas_call(
        functools.partial(_ffn_kernel, final=final_g is not None, n_first=n1 if src2 is not None else None),
        grid=(n1 + n2,),
        in_specs=in_specs,
        out_specs=pl.BlockSpec((tm, D_MODEL), lambda i: (i, 0)),
        out_shape=jax.ShapeDtypeStruct(((n1 + n2) * tm, D_MODEL), F32),
        scratch_shapes=[pltpu.VMEM((tm, D_FF), BF16)],
        compiler_params=_params(1),
        name="ffn",
    )(*args)


def _rms_mm_kernel(x_ref, g_ref, w_ref, o_ref, u_ref):
    @pl.when(pl.program_id(1) == 0)
    def _():
        u_ref[...] = _rms(x_ref[...], g_ref[...]).astype(BF16)

    o_ref[...] = _dot(u_ref[...], w_ref[...])


def _rms_matmul(x, g, w, *, row_start, n_rows, tm, tn, name):
    n = w.shape[1]
    r0 = row_start // tm
    return pl.pallas_call(
        _rms_mm_kernel,
        grid=(n_rows // tm, n // tn),
        in_specs=[
            pl.BlockSpec((tm, D_MODEL), lambda i, j: (r0 + i, 0)),
            pl.BlockSpec((1, D_MODEL), lambda i, j: (0, 0)),
            pl.BlockSpec((D_MODEL, tn), lambda i, j: (0, j)),
        ],
        out_specs=pl.BlockSpec((tm, tn), lambda i, j: (i, j)),
        out_shape=jax.ShapeDtypeStruct((n_rows, n), F32),
        scratch_shapes=[pltpu.VMEM((tm, D_MODEL), BF16)],
        compiler_params=_params(2),
        name=name,
    )(x, g, w)


def _mem_kv_kernel(m_ref, g_ref, wk_ref, wv_ref, k_ref, v_ref, *, tm):
    u = _rms(m_ref[...], g_ref[...]).astype(BF16)
    for w_ref, o_ref in ((wk_ref, k_ref), (wv_ref, v_ref)):
        kv = _dot(u, w_ref[...])
        for j in range(MEM_CHUNKS):
            o_ref[pl.ds(j, tm, stride=MEM_CHUNKS), :] = kv[:, _chunk_to_col(j):_chunk_to_col(j) + LANES]


def _mem_kv(mem, g, wk, wv, *, tm=512):
    rows = mem.shape[0]
    w_spec = pl.BlockSpec((None, D_MODEL, D_MODEL), lambda l, r: (l, 0, 0))
    o_spec = pl.BlockSpec((None, tm * MEM_CHUNKS, LANES), lambda l, r: (l, r, 0))
    o_shape = jax.ShapeDtypeStruct((DEPTH, rows * MEM_CHUNKS, LANES), F32)
    return pl.pallas_call(
        functools.partial(_mem_kv_kernel, tm=tm),
        grid=(DEPTH, rows // tm),
        in_specs=[
            pl.BlockSpec((tm, D_MODEL), lambda l, r: (r, 0)),
            pl.BlockSpec((None, 1, D_MODEL), lambda l, r: (l, 0, 0)),
            w_spec, w_spec,
        ],
        out_specs=[o_spec, o_spec],
        out_shape=[o_shape, o_shape],
        compiler_params=_params(2),
        name="mem_kv",
    )(mem, g, wk, wv)


def _mm_res_kernel(x_ref, y_ref, w_ref, o_ref):
    o_ref[...] = x_ref[...] + _dot(y_ref[...].astype(BF16), w_ref[...])


def _matmul_residual(x, y, w, *, row_start, tm, name):
    n_rows, k = y.shape
    r0 = row_start // tm
    return pl.pallas_call(
        _mm_res_kernel,
        grid=(n_rows // tm,),
        in_specs=[
            pl.BlockSpec((tm, D_MODEL), lambda i: (r0 + i, 0)),
            pl.BlockSpec((tm, k), lambda i: (i, 0)),
            pl.BlockSpec((k, D_MODEL), lambda i: (0, 0)),
        ],
        out_specs=pl.BlockSpec((tm, D_MODEL), lambda i: (r0 + i, 0)),
        out_shape=jax.ShapeDtypeStruct(x.shape, F32),
        input_output_aliases={0: 0},
        compiler_params=_params(1),
        name=name,
    )(x, y, w)


CONV_LEAD = 8
CONV_TAIL0 = CONV_LEAD - (CONV_W - 1)


def _scan_chunk(zx, x_res, cw_ref, cb_ref, dtb_ref, alog_ref, dsk_ref, nw_ref, wout_ref,
                xpad_ref, ht_ref, ybuf_ref):
    q = SSD_CHUNK
    lead, tail0 = CONV_LEAD, CONV_TAIL0
    xpad_ref[lead:lead + q, :] = zx[:, XBC_OFF:DT_OFF]
    conv = cw_ref[0:1, :] * xpad_ref[tail0:tail0 + q, :]
    for k in range(1, CONV_W):
        conv = conv + cw_ref[k:k + 1, :] * xpad_ref[tail0 + k:tail0 + k + q, :]
    conv = cb_ref[...] + conv
    tail = xpad_ref[q + tail0:q + lead, :]
    xpad_ref[tail0:lead, :] = tail

    xbc = _silu(conv)
    xs = xbc[:, :D_INNER]
    bm = xbc[:, D_INNER:D_INNER + GN]
    cm = xbc[:, D_INNER + GN:]

    dt = _softplus(zx[:, DT_OFF:DT_OFF + LANES] + dtb_ref[...])
    a = -jnp.exp(alog_ref[...])
    da = dt * a
    rows = lax.broadcasted_iota(jnp.int32, (q, q), 0)
    cols = lax.broadcasted_iota(jnp.int32, (q, q), 1)
    causal = rows >= cols
    tri = jnp.where(causal, 1.0, 0.0).astype(F32)
    acs = jnp.dot(tri, da, precision=lax.Precision.HIGHEST, preferred_element_type=F32)
    acs_t = acs.T
    dt_t = dt.T
    src_t = acs_t - jnp.log(dt_t)
    eacs = jnp.exp(acs)
    last = acs_t[:, q - 1:q]
    w_t = jnp.exp(last - acs_t) * dt_t
    cd_t = jnp.exp(last)
    low_half = lax.broadcasted_iota(jnp.int32, (1, LANES), 1) < SSD_HEAD_DIM

    for g in range(SSD_GROUPS):
        bm_g = bm[:, g * D_STATE:(g + 1) * D_STATE]
        cm_g = cm[:, g * D_STATE:(g + 1) * D_STATE]
        bm_gt = bm_g.T
        cb_g = _dot(cm_g.astype(BF16), bm_gt.astype(BF16))
        for hh in range(0, HEADS_PER_GROUP, 2):
            h0 = g * HEADS_PER_GROUP + hh
            lo = (h0 // 2) * LANES
            x_pair = xs[:, lo:lo + LANES]
            ht_pair = ht_ref[:, lo:lo + LANES]
            x_pair_b = x_pair.astype(BF16)
            rhs = jnp.concatenate([x_pair_b, ht_pair.astype(BF16)], axis=0)
            ys, sts = [], []
            for h in (h0, h0 + 1):
                seg = acs[:, h:h + 1] - src_t[h:h + 1, :]
                m_h = cb_g * jnp.exp(jnp.where(causal, seg, -jnp.inf))
                e_h = eacs[:, h:h + 1] * cm_g
                lhs = jnp.concatenate([m_h.astype(BF16), e_h.astype(BF16)], axis=1)
                ys.append(_dot(lhs, rhs))
                b_h = bm_gt * w_t[h:h + 1, :]
                sts.append(_dot(b_h.astype(BF16), x_pair_b))
            y_pair = jnp.where(low_half, ys[0], ys[1]) + x_pair * dsk_ref[:, lo:lo + LANES]
            st_pair = jnp.where(low_half, sts[0], sts[1])
            cd_row = jnp.where(low_half, cd_t[h0:h0 + 1, :], cd_t[h0 + 1:h0 + 2, :])
            ht_ref[:, lo:lo + LANES] = ht_pair * cd_row + st_pair
            ybuf_ref[:, lo:lo + LANES] = y_pair

    y = ybuf_ref[...] * _silu(zx[:, :D_INNER])
    gw = D_INNER // SSD_GROUPS
    y_norm = []
    for g in range(SSD_GROUPS):
        yg = y[:, g * gw:(g + 1) * gw]
        yg = yg * lax.rsqrt(jnp.mean(yg * yg, axis=-1, keepdims=True) + EPS)
        y_norm.append((yg * nw_ref[:, g * gw:(g + 1) * gw]).astype(BF16))
    return x_res + _dot(jnp.concatenate(y_norm, axis=1), wout_ref[...]), tail


def _ssd_prompt_kernel(x_ref, g_ref, win_ref, cbuf_ref, st_ref, cw_ref, cb_ref, dtb_ref, alog_ref,
                       dsk_ref, nw_ref, wout_ref, o_ref, nconv_ref, nst_ref,
                       za_ref, zb_ref, xpad_ref, ht_ref, ybuf_ref, *, ns):
    s = pl.program_id(1)
    q = SSD_CHUNK
    xa_ref = x_ref.at[0:2 * q]
    xb_ref = x_ref.at[2 * q:3 * q]

    def project(x_rows, z_ref):
        z_ref[...] = _dot(_rms(x_rows, g_ref[...]).astype(BF16), win_ref[...])

    scan = functools.partial(_scan_chunk, cw_ref=cw_ref, cb_ref=cb_ref, dtb_ref=dtb_ref, alog_ref=alog_ref,
                             dsk_ref=dsk_ref, nw_ref=nw_ref, wout_ref=wout_ref,
                             xpad_ref=xpad_ref, ht_ref=ht_ref, ybuf_ref=ybuf_ref)

    @pl.when(s == 0)
    def _():
        xpad_ref[CONV_TAIL0:CONV_LEAD, :] = cbuf_ref[...]
        ht_ref[...] = st_ref[...].T
        project(xa_ref[0:q, :], za_ref)

    x1 = xa_ref[q:2 * q, :]
    project(x1, zb_ref)
    o_ref[0:q, :], _ = scan(za_ref, xa_ref[0:q, :])
    project(xb_ref[...], za_ref)
    o_ref[q:2 * q, :], tail = scan(zb_ref, x1)

    @pl.when(s == ns - 1)
    def _():
        nconv_ref[...] = tail
        nst_ref[...] = ht_ref[...].T


def _ssd_prompt(x, g, w_in, cbuf, st, cw, cb, dtb, alog, dsk, nw, w_out):
    nb = cbuf.shape[0]
    nc = SEQ // SSD_CHUNK
    ns = nc // 2
    q = SSD_CHUNK
    assert x.shape[0] >= nb * SEQ + q
    vec = lambda n: pl.BlockSpec((1, n), lambda b, c: (0, 0))
    return pl.pallas_call(
        functools.partial(_ssd_prompt_kernel, ns=ns),
        grid=(nb, ns),
        in_specs=[
            pl.BlockSpec((pl.Element(3 * q), pl.Element(D_MODEL)), lambda b, s: ((b * ns + s) * 2 * q, 0)),
            vec(D_MODEL),
            _resident((D_MODEL, IN_PAD_P)),
            pl.BlockSpec((None, CONV_W - 1, CONV_DIM), lambda b, c: (b, 0, 0)),
            pl.BlockSpec((None, D_INNER, D_STATE), lambda b, c: (b, 0, 0)),
            pl.BlockSpec((CONV_W, CONV_DIM), lambda b, c: (0, 0)),
            vec(CONV_DIM), vec(LANES), vec(LANES), vec(D_INNER), vec(D_INNER),
            _resident((D_INNER, D_MODEL)),
        ],
        out_specs=[
            pl.BlockSpec((2 * q, D_MODEL), lambda b, s: (b * ns + s, 0)),
            pl.BlockSpec((None, CONV_W - 1, CONV_DIM), lambda b, c: (b, 0, 0)),
            pl.BlockSpec((None, D_INNER, D_STATE), lambda b, c: (b, 0, 0)),
        ],
        out_shape=[
            jax.ShapeDtypeStruct(x.shape, F32),
            jax.ShapeDtypeStruct((nb, CONV_W - 1, CONV_DIM), F32),
            jax.ShapeDtypeStruct((nb, D_INNER, D_STATE), F32),
        ],
        scratch_shapes=[
            pltpu.VMEM((q, IN_PAD_P), F32),
            pltpu.VMEM((q, IN_PAD_P), F32),
            pltpu.VMEM((q + CONV_LEAD, CONV_DIM), F32),
            pltpu.VMEM((D_STATE, D_INNER), F32),
            pltpu.VMEM((q, D_INNER), F32),
        ],
        input_output_aliases={0: 0},
        compiler_params=_params(2),
        name="ssd_prompt",
    )(x, g, w_in, cbuf, st, cw, cb, dtb, alog, dsk, nw, w_out)


def _ssd_sample_kernel(zx_ref, cbuf_ref, st_ref, cw_ref, cb_ref, dtb_ref, alog_ref, dsk_ref, nw_ref, *rest, tb):
    y_ref, nconv_ref, nst_ref, xpad_ref = rest[-4:]
    q = DEC_SEQ
    lead = 8
    tail0 = lead - (CONV_W - 1)
    gw = D_INNER // SSD_GROUPS
    a = -jnp.exp(alog_ref[...])
    rows = lax.broadcasted_iota(jnp.int32, (q, 1), 0)
    xws, cds, bms = [], [], []
    for b in range(tb):
        r0 = b * q
        xpad_ref[b, tail0:lead, :] = cbuf_ref[b]
        xpad_ref[b, lead:lead + q, :] = zx_ref[r0:r0 + q, XBC_OFF:DT_OFF]
        conv = cw_ref[0:1, :] * xpad_ref[b, tail0:tail0 + q, :]
        for k in range(1, CONV_W):
            conv = conv + cw_ref[k:k + 1, :] * xpad_ref[b, tail0 + k:tail0 + k + q, :]
        conv = cb_ref[...] + conv
        nconv_ref[b] = xpad_ref[b, q + tail0:q + lead, :]

        xbc = _silu(conv)
        xs = xbc[:, :D_INNER]
        bm = xbc[:, D_INNER:D_INNER + GN]
        cm = xbc[:, D_INNER + GN:]

        dt = _softplus(zx_ref[r0:r0 + q, DT_OFF:DT_OFF + D_INNER] + dtb_ref[...])
        da = dt * a
        acs = jnp.where(rows >= 0, da[0:1, :], 0.0)
        for k in range(1, q):
            acs = acs + jnp.where(rows >= k, da[k:k + 1, :], 0.0)
        xdt = xs * dt

        y = jnp.zeros((q, D_INNER), F32)
        for k in range(q):
            decay = jnp.exp(jnp.where(rows >= k, acs - acs[k:k + 1, :], -jnp.inf))
            cb_k = jnp.concatenate(
                [jnp.broadcast_to(
                    jnp.sum(cm[:, g * D_STATE:(g + 1) * D_STATE] * bm[k:k + 1, g * D_STATE:(g + 1) * D_STATE],
                            axis=-1, keepdims=True), (q, gw)) for g in range(SSD_GROUPS)], axis=1)
            y = y + cb_k * decay * xdt[k:k + 1, :]

        cm_rows = jnp.concatenate([cm[:, g * D_STATE:(g + 1) * D_STATE] for g in range(SSD_GROUPS)], axis=0)
        r = lax.dot_general(cm_rows.astype(BF16), st_ref[b].astype(BF16), (((1,), (1,)), ((), ())),
                            preferred_element_type=F32)
        y_off = jnp.concatenate([r[g * q:(g + 1) * q, g * gw:(g + 1) * gw] for g in range(SSD_GROUPS)], axis=1)
        y = y + jnp.exp(acs) * y_off
        y = y + xs * dsk_ref[...]

        y = y * _silu(zx_ref[r0:r0 + q, :D_INNER])
        for g in range(SSD_GROUPS):
            yg = y[:, g * gw:(g + 1) * gw]
            yg = yg * lax.rsqrt(jnp.mean(yg * yg, axis=-1, keepdims=True) + EPS)
            y_ref[r0:r0 + q, g * gw:(g + 1) * gw] = yg * nw_ref[:, g * gw:(g + 1) * gw]

        last = acs[q - 1:q, :]
        xws.append(xdt * jnp.exp(last - acs))
        cds.append(jnp.exp(last))
        bms.append(bm)

    padded = jnp.concatenate(xws + [jnp.zeros((LANES - tb * q, D_INNER), F32)], axis=0)
    padded_t = padded.T.astype(BF16)
    zero_blk = jnp.zeros((q, D_STATE), F32)
    for b0 in range(0, tb, 2):
        for g in range(SSD_GROUPS):
            blocks = []
            for i in range(LANES // q):
                left = bms[b0][:, g * D_STATE:(g + 1) * D_STATE] if i == b0 else zero_blk
                right = bms[b0 + 1][:, g * D_STATE:(g + 1) * D_STATE] if i == b0 + 1 else zero_blk
                blocks.append(jnp.concatenate([left, right], axis=1))
            rhs = jnp.concatenate(blocks, axis=0).astype(BF16)
            st_pair = _dot(padded_t[g * gw:(g + 1) * gw, :], rhs)
            for half, b in enumerate((b0, b0 + 1)):
                for hh in range(HEADS_PER_GROUP):
                    lo = g * gw + hh * SSD_HEAD_DIM
                    cd_h = jnp.broadcast_to(cds[b][:, lo:lo + 1], (SSD_HEAD_DIM, D_STATE))
                    nst_ref[b, lo:lo + SSD_HEAD_DIM, :] = (
                        st_ref[b, lo:lo + SSD_HEAD_DIM, :] * cd_h
                        + st_pair[hh * SSD_HEAD_DIM:(hh + 1) * SSD_HEAD_DIM, half * D_STATE:(half + 1) * D_STATE])


def _ssd_sample(zx, cbuf, st, layer, cw, cb, dtb, alog, dsk, nw, prev_states, *, tb=8):
    q = DEC_SEQ
    vec = lambda n: pl.BlockSpec((1, n), lambda i: (0, 0))
    n_layers = st.shape[0]
    in_specs = [
        pl.BlockSpec((tb * q, IN_PAD_S), lambda i: (i, 0)),
        pl.BlockSpec((None, tb, CONV_W - 1, CONV_DIM), lambda i: (layer, i, 0, 0)),
        pl.BlockSpec((None, tb, D_INNER, D_STATE), lambda i: (layer, i, 0, 0)),
        pl.BlockSpec((CONV_W, CONV_DIM), lambda i: (0, 0)),
        vec(CONV_DIM), vec(D_INNER), vec(D_INNER), vec(D_INNER), vec(D_INNER),
    ]
    args = [zx, cbuf, st, cw, cb, dtb, alog, dsk, nw]
    aliases = {}
    if prev_states is not None:
        in_specs.append(pl.BlockSpec(memory_space=pl.ANY))
        args.append(prev_states)
        aliases = {len(args) - 1: 2}
    return pl.pallas_call(
        functools.partial(_ssd_sample_kernel, tb=tb),
        grid=(DEC_BATCH // tb,),
        in_specs=in_specs,
        out_specs=[
            pl.BlockSpec((tb * q, D_INNER), lambda i: (i, 0)),
            pl.BlockSpec((tb, CONV_W - 1, CONV_DIM), lambda i: (i, 0, 0)),
            pl.BlockSpec((None, tb, D_INNER, D_STATE), lambda i: (layer, i, 0, 0)),
        ],
        out_shape=[
            jax.ShapeDtypeStruct((ROWS_S, D_INNER), F32),
            jax.ShapeDtypeStruct((DEC_BATCH, CONV_W - 1, CONV_DIM), F32),
            jax.ShapeDtypeStruct((n_layers, DEC_BATCH, D_INNER, D_STATE), F32),
        ],
        scratch_shapes=[pltpu.VMEM((tb, q + 8, CONV_DIM), F32)],
        input_output_aliases=aliases,
        compiler_params=_params(1),
        name="ssd_sample",
    )(*args)


POOL_LEAD = 32


def _pool_kernel(x_ref, g_ref, buf_ref, pw_ref, ps_ref, o_ref, np_ref, xx_ref, s2_ref, s4_ref, s8_ref,
                 *, tb, tl, nl, pos0):
    l = pl.program_id(1)
    x = x_ref[...]
    u = _rms(x, g_ref[...]).reshape(tb, tl, D_MODEL)
    lead = POOL_LEAD
    b0 = lead - POOL_BUF
    end = lead + tl
    gw = POOL_GW

    @pl.when(l == 0)
    def _():
        xx_ref[:, 0:b0, :] = jnp.zeros((tb, b0, D_MODEL), F32)
        xx_ref[:, b0:lead, :] = buf_ref[...]

    xx_ref[:, lead:end, :] = u
    s2_ref[:, 8:end, :] = xx_ref[:, 8:end, :] + xx_ref[:, 7:end - 1, :]
    s4_ref[:, 16:end, :] = s2_ref[:, 16:end, 2 * gw:] + s2_ref[:, 14:end - 2, 2 * gw:]
    s8_ref[:, 24:end, :] = s4_ref[:, 24:end, gw:] + s4_ref[:, 20:end - 4, gw:]
    sums = (s2_ref[:, lead:end, 0:gw],
            s2_ref[:, lead:end, gw:2 * gw] + s2_ref[:, lead - 2:end - 2, gw:2 * gw],
            s4_ref[:, lead:end, 0:gw] + s4_ref[:, lead - 4:end - 4, 0:gw],
            s8_ref[:, lead:end, :] + s8_ref[:, lead - 8:end - 8, :])

    pos = (pos0 + l * tl + lax.broadcasted_iota(jnp.int32, (1, tl, 1), 1)).astype(F32)
    outs = []
    for gi, w in enumerate(POOL_WINDOWS):
        lo = gi * gw
        cnt = jnp.minimum(pos + 1.0, float(w))
        mix = sums[gi] / cnt - u[:, :, lo:lo + gw]
        outs.append(_dot(mix.reshape(tb * tl, gw).astype(BF16), pw_ref[gi]))
    o_ref[...] = x + jnp.concatenate(outs, axis=-1) * ps_ref[...]

    tail = xx_ref[:, end - POOL_BUF:end, :]
    if nl > 1:
        xx_ref[:, b0:lead, :] = tail

    @pl.when(l == nl - 1)
    def _():
        np_ref[...] = tail


def _pool(x, g, buf_arr, buf_index, pw, ps, *, row_start, nb, seq, tb, tl, pos0, name):
    nl = seq // tl
    r0 = row_start // (tb * tl)
    x_spec = pl.BlockSpec((tb * tl, D_MODEL), lambda b, l: (r0 + b * nl + l, 0))
    return pl.pallas_call(
        functools.partial(_pool_kernel, tb=tb, tl=tl, nl=nl, pos0=pos0),
        grid=(nb // tb, nl),
        in_specs=[
            x_spec,
            pl.BlockSpec((1, D_MODEL), lambda b, l: (0, 0)),
            buf_index(tb),
            pl.BlockSpec((len(POOL_WINDOWS), POOL_GW, POOL_GW), lambda b, l: (0, 0, 0)),
            pl.BlockSpec((1, D_MODEL), lambda b, l: (0, 0)),
        ],
        out_specs=[x_spec, pl.BlockSpec((tb, POOL_BUF, D_MODEL), lambda b, l: (b, 0, 0))],
        out_shape=[jax.ShapeDtypeStruct(x.shape, F32),
                   jax.ShapeDtypeStruct((nb, POOL_BUF, D_MODEL), F32)],
        scratch_shapes=[pltpu.VMEM((tb, POOL_LEAD + tl, D_MODEL), F32),
                        pltpu.VMEM((tb, POOL_LEAD + tl, D_MODEL), F32),
                        pltpu.VMEM((tb, POOL_LEAD + tl, D_MODEL - 2 * POOL_GW), F32),
                        pltpu.VMEM((tb, POOL_LEAD + tl, POOL_GW), F32)],
        input_output_aliases={0: 0},
        compiler_params=_params(2),
        name=name,
    )(x, g, buf_arr, pw, ps)


def _head_kv(ref, h):
    return jnp.concatenate([ref[pl.ds(c * MEM_HEADS + h, N_MEM, stride=MEM_CHUNKS), :]
                            for c in range(MEM_HEAD_DIM // LANES)], axis=1).astype(BF16)


def _xattn_prompt_kernel(x_ref, g_ref, wq_ref, k_ref, v_ref, wo_ref, o_ref):
    x = x_ref[...]
    u = _rms(x, g_ref[...]).astype(BF16)
    q = (_dot(u, wq_ref[...]) * (MEM_HEAD_DIM ** -0.5)).astype(BF16)
    outs = []
    for h in range(MEM_HEADS):
        q_h = q[:, h * MEM_HEAD_DIM:(h + 1) * MEM_HEAD_DIM]
        s = lax.dot_general(q_h, _head_kv(k_ref, h), (((1,), (1,)), ((), ())), preferred_element_type=F32)
        e = jnp.exp(s - jnp.max(s, axis=-1, keepdims=True))
        den = jnp.sum(e, axis=-1, keepdims=True)
        outs.append(_dot(e.astype(BF16), _head_kv(v_ref, h)) / den)
    o = jnp.concatenate(outs, axis=-1)
    o_ref[...] = x + _dot(o.astype(BF16), wo_ref[...])


def _xattn_prompt(x, g, wq, mk, mv, wo, layer, *, tl=1024):
    nl = SEQ // tl
    x_spec = pl.BlockSpec((tl, D_MODEL), lambda b, l: (b * nl + l, 0))
    w_spec = _resident((D_MODEL, D_MODEL), layer)
    kv_spec = pl.BlockSpec((None, N_MEM * MEM_CHUNKS, LANES), lambda b, l: (layer, b, 0))
    return pl.pallas_call(
        _xattn_prompt_kernel,
        grid=(BATCH, nl),
        in_specs=[x_spec, pl.BlockSpec((1, D_MODEL), lambda b, l: (0, 0)), w_spec, kv_spec, kv_spec, w_spec],
        out_specs=x_spec,
        out_shape=jax.ShapeDtypeStruct(x.shape, F32),
        input_output_aliases={0: 0},
        compiler_params=_params(2),
        name="xattn_prompt",
    )(x, g, wq, mk, mv, wo)


def _chunk_rows(t):
    lead = t.shape[:-3]
    t = t.reshape(*lead, N_MEM, MEM_HEADS, MEM_HEAD_DIM // LANES, LANES)
    return jnp.swapaxes(t, -2, -3).reshape(*lead, N_MEM * MEM_CHUNKS, LANES)


def _from_chunk_rows(t):
    t = t.reshape(DEPTH, BATCH, N_MEM, MEM_HEAD_DIM // LANES, MEM_HEADS, LANES)
    return jnp.swapaxes(t, -2, -3).reshape(DEPTH, BATCH, N_MEM, MEM_HEADS, MEM_HEAD_DIM)


def _chunk_to_col(j):
    return ((j % MEM_HEADS) * (MEM_HEAD_DIM // LANES) + j // MEM_HEADS) * LANES


def _attn_sample_kernel(x_ref, g_ref, wq_ref, k_ref, v_ref, wo_ref, o_ref, obuf_ref, *, tb):
    n = DEC_SEQ
    rows = lax.broadcasted_iota(jnp.int32, (MEM_HEADS * n, D_MODEL), 0) // n
    chunk = lax.broadcasted_iota(jnp.int32, (MEM_HEADS * n, D_MODEL), 1) // LANES
    own = rows == chunk % MEM_HEADS

    def gather(ref, b):
        return jnp.concatenate([ref[b, pl.ds(j, N_MEM, stride=MEM_CHUNKS), :] for j in range(MEM_CHUNKS)],
                               axis=1).astype(BF16)

    x = x_ref[...]
    q = _dot(_rms(x, g_ref[...]).astype(BF16), wq_ref[...]) * (MEM_HEAD_DIM ** -0.5)
    for b in range(tb):
        q_b = q[b * n:(b + 1) * n, :]
        q_c = jnp.concatenate([q_b[:, _chunk_to_col(j):_chunk_to_col(j) + LANES] for j in range(MEM_CHUNKS)], axis=1)
        q_blk = jnp.where(own, jnp.concatenate([q_c] * MEM_HEADS, axis=0), 0.0).astype(BF16)
        s = lax.dot_general(q_blk, gather(k_ref, b), (((1,), (1,)), ((), ())), preferred_element_type=F32)
        e = jnp.exp(s - jnp.max(s, axis=-1, keepdims=True))
        den = jnp.sum(e, axis=-1, keepdims=True)
        o = _dot(e.astype(BF16), gather(v_ref, b)) / den
        for j in range(MEM_CHUNKS):
            h = j % MEM_HEADS
            obuf_ref[b * n:(b + 1) * n, _chunk_to_col(j):_chunk_to_col(j) + LANES] = (
                o[h * n:(h + 1) * n, j * LANES:(j + 1) * LANES])
    o_ref[...] = x + _dot(obuf_ref[...].astype(BF16), wo_ref[...])


def _xattn_sample(x, g, wq, ck, cv, wo, layer, *, tb=8):
    n = DEC_SEQ
    r0 = ROWS_P // (tb * n)
    kv_spec = pl.BlockSpec((None, tb, N_MEM * MEM_CHUNKS, LANES), lambda i: (layer, i, 0, 0))
    row_spec = pl.BlockSpec((tb * n, D_MODEL), lambda i: (r0 + i, 0))
    w_spec = _resident((D_MODEL, D_MODEL), layer)
    return pl.pallas_call(
        functools.partial(_attn_sample_kernel, tb=tb),
        grid=(DEC_BATCH // tb,),
        in_specs=[row_spec, _resident((1, D_MODEL)), w_spec, kv_spec, kv_spec, w_spec],
        out_specs=row_spec,
        out_shape=jax.ShapeDtypeStruct(x.shape, F32),
        scratch_shapes=[pltpu.VMEM((tb * n, D_MODEL), F32)],
        input_output_aliases={0: 0},
        compiler_params=_params(1),
        name="xattn_sample",
    )(x, g, wq, ck, cv, wo)


def _row(v):
    return v.reshape(1, -1)


def _per_channel(v):
    return jnp.repeat(v, SSD_HEAD_DIM).reshape(1, D_INNER)


def _pad_lanes(v):
    return jnp.pad(v, (0, LANES - v.shape[0])).reshape(1, LANES)


def kernel(x_prompt, x_sample, mem_prompt, cache_mem_k, cache_mem_v, state_ssm, state_conv, state_pool, norm_ffn1, ffn1_w_gate, ffn1_w_up, ffn1_w_down, norm_mix, ssd_in_w, ssd_conv_w, ssd_conv_b, ssd_dt_bias, ssd_a_log, ssd_d, ssd_norm_w, ssd_out_w, pool_w, pool_scale, norm_cross, norm_mem, xa_wq, xa_wk, xa_wv, xa_wo, norm_ffn2, ffn2_w_gate, ffn2_w_up, ffn2_w_down, final_norm):
    bf = lambda w: w.astype(BF16)
    ffn1_w = (bf(ffn1_w_gate), bf(ffn1_w_up), bf(ffn1_w_down))
    ffn2_w = (bf(ffn2_w_gate), bf(ffn2_w_up), bf(ffn2_w_down))
    xa_wq_b, xa_wo_b = bf(xa_wq), bf(xa_wo)
    mk_p, mv_p = _mem_kv(mem_prompt.reshape(BATCH * N_MEM, D_MODEL), norm_mem.reshape(DEPTH, 1, D_MODEL),
                         bf(xa_wk), bf(xa_wv))
    ck = _chunk_rows(cache_mem_k)
    cv = _chunk_rows(cache_mem_v)
    st_s = state_ssm.reshape(-1, DEC_BATCH, D_INNER, D_STATE)
    conv0 = jnp.zeros((BATCH, CONV_W - 1, CONV_DIM), F32)
    ssm0 = jnp.zeros((BATCH, D_INNER, D_STATE), F32)
    pool0 = jnp.zeros((BATCH, POOL_BUF, D_MODEL), F32)

    ssm_p, conv_p, pool_p, conv_s, pool_s = [], [], [], [], []
    ssm_s = None
    for i in range(DEPTH):
        j = i // 2
        ffn1 = (_row(norm_ffn1[i]), *ffn1_w, i)
        if i == 0:
            x = _ffn(x_prompt.reshape(ROWS_P, D_MODEL), *ffn1, src2=x_sample.reshape(ROWS_S, D_MODEL))
        else:
            x = _ffn(x, *ffn1)
        g_mix = _row(norm_mix[i])
        if i % 2 == 0:
            w_in = ssd_in_w[j]
            w_dt = w_in[:, DT_OFF:]
            w_p = bf(jnp.concatenate(
                [w_in[:, :DT_OFF], jnp.pad(w_dt, ((0, 0), (0, IN_PAD_P - DT_OFF - SSD_HEADS)))], axis=1))
            w_s = bf(jnp.concatenate([w_in[:, :DT_OFF], jnp.repeat(w_dt, SSD_HEAD_DIM, axis=1)], axis=1))
            cw, cb = ssd_conv_w[j], _row(ssd_conv_b[j])
            dsk, nw, w_out = _per_channel(ssd_d[j]), _row(ssd_norm_w[j]), bf(ssd_out_w[j])

            x, nc_p, ns_p = _ssd_prompt(x, g_mix, w_p, conv0, ssm0, cw, cb, _pad_lanes(ssd_dt_bias[j]),
                                        _pad_lanes(ssd_a_log[j]), dsk, nw, w_out)
            zx_s = _rms_matmul(x, g_mix, w_s, row_start=ROWS_P, n_rows=ROWS_S, tm=1024, tn=1792,
                               name="in_proj_s")
            y_s, nc_s, ssm_s = _ssd_sample(zx_s, state_conv, st_s, j, cw, cb, _per_channel(ssd_dt_bias[j]),
                                           _per_channel(ssd_a_log[j]), dsk, nw, ssm_s)
            x = _matmul_residual(x, y_s, w_out, row_start=ROWS_P, tm=1024, name="out_proj_s")
            conv_p.append(nc_p)
            conv_s.append(nc_s)
            ssm_p.append(ns_p.reshape(BATCH, SSD_HEADS, SSD_HEAD_DIM, D_STATE))
        else:
            pw, ps = bf(pool_w[j]), _row(pool_scale[j])
            x, np_p = _pool(x, g_mix, pool0, lambda tb: pl.BlockSpec((tb, POOL_BUF, D_MODEL), lambda b, l: (b, 0, 0)),
                            pw, ps, row_start=0, nb=BATCH, seq=SEQ, tb=1, tl=1024, pos0=0, name="pool_p")
            x, np_s = _pool(x, g_mix, state_pool,
                            lambda tb: pl.BlockSpec((None, tb, POOL_BUF, D_MODEL), lambda b, l: (j, b, 0, 0)),
                            pw, ps, row_start=ROWS_P, nb=DEC_BATCH, seq=DEC_SEQ, tb=8, tl=DEC_SEQ,
                            pos0=PAST_LEN, name="pool_s")
            pool_p.append(np_p)
            pool_s.append(np_s)

        g_x = _row(norm_cross[i])
        x = _xattn_prompt(x, g_x, xa_wq_b, mk_p, mv_p, xa_wo_b, i)
        x = _xattn_sample(x, g_x, xa_wq_b, ck, cv, xa_wo_b, i)

        ffn2 = (_row(norm_ffn2[i]), *ffn2_w, i)
        if i < DEPTH - 1:
            x = _ffn(x, *ffn2)

    g_f = _row(final_norm)
    y_prompt = _ffn(x, *ffn2, n_rows=ROWS_P, final_g=g_f).reshape(BATCH, SEQ, D_MODEL)
    y_sample = _ffn(x, *ffn2, src_row_start=ROWS_P, n_rows=ROWS_S, final_g=g_f).reshape(DEC_BATCH, DEC_SEQ, D_MODEL)
    return (y_prompt, y_sample, jnp.stack(ssm_p), jnp.stack(conv_p), jnp.stack(pool_p),
            _from_chunk_rows(mk_p), _from_chunk_rows(mv_p),
            ssm_s.reshape(state_ssm.shape), jnp.stack(conv_s), jnp.stack(pool_s))
```

```python
import functools

import jax
import jax.numpy as jnp
from jax import lax
from jax.experimental import pallas as pl
from jax.experimental.pallas import tpu as pltpu

F32 = jnp.float32
BF16 = jnp.bfloat16

D_MODEL = 1024
BATCH = 8
SEQ = 2048
DEPTH = 4
DEC_BATCH = 128
DEC_SEQ = 8
PAST_LEN = 16384
D_FF = 2816
D_INNER = 2048
SSD_HEAD_DIM = 64
SSD_HEADS = 32
SSD_GROUPS = 4
HEADS_PER_GROUP = 8
D_STATE = 128
CONV_W = 4
GN = SSD_GROUPS * D_STATE
CONV_DIM = D_INNER + 2 * GN
SSD_CHUNK = 128
POOL_WINDOWS = (2, 4, 8, 16)
POOL_GW = 256
POOL_BUF = 15
N_MEM = 256
MEM_HEADS = 4
MEM_HEAD_DIM = 256
EPS = 1e-5

ROWS_P = BATCH * SEQ
ROWS_S = DEC_BATCH * DEC_SEQ
ROWS = ROWS_P + ROWS_S

LANES = 128
XBC_OFF = D_INNER
DT_OFF = D_INNER + CONV_DIM
IN_PAD_P = 5376
IN_PAD_S = DT_OFF + D_INNER
VMEM_LIMIT = 48 * 1024 * 1024
MXU_TILE = 256
FFN_CHUNK = MXU_TILE
MEM_CHUNKS = D_MODEL // LANES


def _params(n_axes, vmem=VMEM_LIMIT):
    return pltpu.CompilerParams(dimension_semantics=("arbitrary",) * n_axes,
                                vmem_limit_bytes=vmem)


def _rms(x, g):
    r = lax.rsqrt(jnp.mean(x * x, axis=-1, keepdims=True) + EPS)
    return x * r * g


def _silu(x):
    return x * (1.0 / (1.0 + jnp.exp(-x)))


def _softplus(x):
    return jnp.maximum(x, 0.0) + jnp.log(1.0 + jnp.exp(-jnp.abs(x)))


def _dot(a, b):
    return jnp.dot(a, b, preferred_element_type=F32)


def _ffn_kernel(*refs, final, n_first):
    x_ref, g_ref, wg_ref, wu_ref, wd_ref = refs[:5]
    o_ref, h_ref = refs[-2:]
    x = x_ref[...]
    if n_first is not None:
        x = jnp.where(pl.program_id(0) < n_first, x, refs[-3][...])
    u = _rms(x, g_ref[...]).astype(BF16)
    for lo in range(0, D_FF, FFN_CHUNK):
        a = _dot(u, wg_ref[:, lo:lo + FFN_CHUNK])
        b = _dot(u, wu_ref[:, lo:lo + FFN_CHUNK])
        h_ref[:, lo:lo + FFN_CHUNK] = (_silu(a) * b).astype(BF16)
    y = x + 0.5 * _dot(h_ref[...], wd_ref[...])
    o_ref[...] = _rms(y, refs[5][...]) if final else y


def _resident(shape, layer=None):
    if layer is None:
        return pl.BlockSpec(shape, lambda *_: (0,) * len(shape), pipeline_mode=pl.Buffered(1))
    return pl.BlockSpec((None,) + tuple(shape), lambda *_: (layer,) + (0,) * len(shape),
                        pipeline_mode=pl.Buffered(1))


def _ffn(src, g, wg, wu, wd, layer, *, src_row_start=0, n_rows=None, src2=None, final_g=None, tm=1024):
    n_rows = src.shape[0] if n_rows is None else n_rows
    s0, n1 = src_row_start // tm, n_rows // tm
    in_specs = [pl.BlockSpec((tm, D_MODEL), lambda i: (s0 + jnp.minimum(i, n1 - 1), 0)), _resident((1, D_MODEL)),
                _resident((D_MODEL, D_FF), layer), _resident((D_MODEL, D_FF), layer),
                _resident((D_FF, D_MODEL), layer)]
    args = [src, g, wg, wu, wd]
    if final_g is not None:
        in_specs.append(_resident((1, D_MODEL)))
        args.append(final_g)
    n2 = 0
    if src2 is not None:
        n2 = src2.shape[0] // tm
        in_specs.append(pl.BlockSpec((tm, D_MODEL), lambda i: (jnp.maximum(i - n1, 0), 0)))
        args.append(src2)
    return pl.pallas_call(
        functools.partial(_ffn_kernel, final=final_g is not None, n_first=n1 if src2 is not None else None),
        grid=(n1 + n2,),
        in_specs=in_specs,
        out_specs=pl.BlockSpec((tm, D_MODEL), lambda i: (i, 0)),
        out_shape=jax.ShapeDtypeStruct(((n1 + n2) * tm, D_MODEL), F32),
        scratch_shapes=[pltpu.VMEM((tm, D_FF), BF16)],
        compiler_params=_params(1),
        name="ffn",
    )(*args)


def _rms_mm_kernel(x_ref, g_ref, w_ref, o_ref, u_ref):
    @pl.when(pl.program_id(1) == 0)
    def _():
        u_ref[...] = _rms(x_ref[...], g_ref[...]).astype(BF16)

    o_ref[...] = _dot(u_ref[...], w_ref[...])


def _rms_matmul(x, g, w, *, row_start, n_rows, tm, tn, name):
    n = w.shape[1]
    r0 = row_start // tm
    return pl.pallas_call(
        _rms_mm_kernel,
        grid=(n_rows // tm, n // tn),
        in_specs=[
            pl.BlockSpec((tm, D_MODEL), lambda i, j: (r0 + i, 0)),
            pl.BlockSpec((1, D_MODEL), lambda i, j: (0, 0)),
            pl.BlockSpec((D_MODEL, tn), lambda i, j: (0, j)),
        ],
        out_specs=pl.BlockSpec((tm, tn), lambda i, j: (i, j)),
        out_shape=jax.ShapeDtypeStruct((n_rows, n), F32),
        scratch_shapes=[pltpu.VMEM((tm, D_MODEL), BF16)],
        compiler_params=_params(2),
        name=name,
    )(x, g, w)


def _mem_kv_kernel(m_ref, g_ref, wk_ref, wv_ref, k_ref, v_ref, *, tm):
    u = _rms(m_ref[...], g_ref[...]).astype(BF16)
    for w_ref, o_ref in ((wk_ref, k_ref), (wv_ref, v_ref)):
        kv = _dot(u, w_ref[...])
        for j in range(MEM_CHUNKS):
            o_ref[pl.ds(j, tm, stride=MEM_CHUNKS), :] = kv[:, _chunk_to_col(j):_chunk_to_col(j) + LANES]


def _mem_kv(mem, g, wk, wv, *, tm=512):
    rows = mem.shape[0]
    w_spec = pl.BlockSpec((None, D_MODEL, D_MODEL), lambda l, r: (l, 0, 0))
    o_spec = pl.BlockSpec((None, tm * MEM_CHUNKS, LANES), lambda l, r: (l, r, 0))
    o_shape = jax.ShapeDtypeStruct((DEPTH, rows * MEM_CHUNKS, LANES), F32)
    return pl.pallas_call(
        functools.partial(_mem_kv_kernel, tm=tm),
        grid=(DEPTH, rows // tm),
        in_specs=[
            pl.BlockSpec((tm, D_MODEL), lambda l, r: (r, 0)),
            pl.BlockSpec((None, 1, D_MODEL), lambda l, r: (l, 0, 0)),
            w_spec, w_spec,
        ],
        out_specs=[o_spec, o_spec],
        out_shape=[o_shape, o_shape],
        compiler_params=_params(2),
        name="mem_kv",
    )(mem, g, wk, wv)


def _mm_res_kernel(x_ref, y_ref, w_ref, o_ref):
    o_ref[...] = x_ref[...] + _dot(y_ref[...].astype(BF16), w_ref[...])


def _matmul_residual(x, y, w, *, row_start, tm, name):
    n_rows, k = y.shape
    r0 = row_start // tm
    return pl.pallas_call(
        _mm_res_kernel,
        grid=(n_rows // tm,),
        in_specs=[
            pl.BlockSpec((tm, D_MODEL), lambda i: (r0 + i, 0)),
            pl.BlockSpec((tm, k), lambda i: (i, 0)),
            pl.BlockSpec((k, D_MODEL), lambda i: (0, 0)),
        ],
        out_specs=pl.BlockSpec((tm, D_MODEL), lambda i: (r0 + i, 0)),
        out_shape=jax.ShapeDtypeStruct(x.shape, F32),
        input_output_aliases={0: 0},
        compiler_params=_params(1),
        name=name,
    )(x, y, w)


CONV_LEAD = 8
CONV_TAIL0 = CONV_LEAD - (CONV_W - 1)


def _scan_chunk(zx, x_res, cw_ref, cb_ref, dtb_ref, alog_ref, dsk_ref, nw_ref, wout_ref,
                xpad_ref, ht_ref, ybuf_ref):
    q = SSD_CHUNK
    lead, tail0 = CONV_LEAD, CONV_TAIL0
    xpad_ref[lead:lead + q, :] = zx[:, XBC_OFF:DT_OFF]
    conv = cw_ref[0:1, :] * xpad_ref[tail0:tail0 + q, :]
    for k in range(1, CONV_W):
        conv = conv + cw_ref[k:k + 1, :] * xpad_ref[tail0 + k:tail0 + k + q, :]
    conv = cb_ref[...] + conv
    tail = xpad_ref[q + tail0:q + lead, :]
    xpad_ref[tail0:lead, :] = tail

    xbc = _silu(conv)
    xs = xbc[:, :D_INNER]
    bm = xbc[:, D_INNER:D_INNER + GN]
    cm = xbc[:, D_INNER + GN:]

    dt = _softplus(zx[:, DT_OFF:DT_OFF + LANES] + dtb_ref[...])
    a = -jnp.exp(alog_ref[...])
    da = dt * a
    rows = lax.broadcasted_iota(jnp.int32, (q, q), 0)
    cols = lax.broadcasted_iota(jnp.int32, (q, q), 1)
    causal = rows >= cols
    tri = jnp.where(causal, 1.0, 0.0).astype(F32)
    acs = jnp.dot(tri, da, precision=lax.Precision.HIGHEST, preferred_element_type=F32)
    acs_t = acs.T
    dt_t = dt.T
    src_t = acs_t - jnp.log(dt_t)
    eacs = jnp.exp(acs)
    last = acs_t[:, q - 1:q]
    w_t = jnp.exp(last - acs_t) * dt_t
    cd_t = jnp.exp(last)
    low_half = lax.broadcasted_iota(jnp.int32, (1, LANES), 1) < SSD_HEAD_DIM

    for g in range(SSD_GROUPS):
        bm_g = bm[:, g * D_STATE:(g + 1) * D_STATE]
        cm_g = cm[:, g * D_STATE:(g + 1) * D_STATE]
        bm_gt = bm_g.T
        cb_g = _dot(cm_g.astype(BF16), bm_gt.astype(BF16))
        for hh in range(0, HEADS_PER_GROUP, 2):
            h0 = g * HEADS_PER_GROUP + hh
            lo = (h0 // 2) * LANES
            x_pair = xs[:, lo:lo + LANES]
            ht_pair = ht_ref[:, lo:lo + LANES]
            x_pair_b = x_pair.astype(BF16)
            rhs = jnp.concatenate([x_pair_b, ht_pair.astype(BF16)], axis=0)
            ys, sts = [], []
            for h in (h0, h0 + 1):
                seg = acs[:, h:h + 1] - src_t[h:h + 1, :]
                m_h = cb_g * jnp.exp(jnp.where(causal, seg, -jnp.inf))
                e_h = eacs[:, h:h + 1] * cm_g
                lhs = jnp.concatenate([m_h.astype(BF16), e_h.astype(BF16)], axis=1)
                ys.append(_dot(lhs, rhs))
                b_h = bm_gt * w_t[h:h + 1, :]
                sts.append(_dot(b_h.astype(BF16), x_pair_b))
            y_pair = jnp.where(low_half, ys[0], ys[1]) + x_pair * dsk_ref[:, lo:lo + LANES]
            st_pair = jnp.where(low_half, sts[0], sts[1])
            cd_row = jnp.where(low_half, cd_t[h0:h0 + 1, :], cd_t[h0 + 1:h0 + 2, :])
            ht_ref[:, lo:lo + LANES] = ht_pair * cd_row + st_pair
            ybuf_ref[:, lo:lo + LANES] = y_pair

    y = ybuf_ref[...] * _silu(zx[:, :D_INNER])
    gw = D_INNER // SSD_GROUPS
    y_norm = []
    for g in range(SSD_GROUPS):
        yg = y[:, g * gw:(g + 1) * gw]
        yg = yg * lax.rsqrt(jnp.mean(yg * yg, axis=-1, keepdims=True) + EPS)
        y_norm.append((yg * nw_ref[:, g * gw:(g + 1) * gw]).astype(BF16))
    return x_res + _dot(jnp.concatenate(y_norm, axis=1), wout_ref[...]), tail


def _ssd_prompt_kernel(x_ref, g_ref, win_ref, cbuf_ref, st_ref, cw_ref, cb_ref, dtb_ref, alog_ref,
                       dsk_ref, nw_ref, wout_ref, o_ref, nconv_ref, nst_ref,
                       za_ref, zb_ref, xpad_ref, ht_ref, ybuf_ref, *, ns):
    s = pl.program_id(1)
    q = SSD_CHUNK
    xa_ref = x_ref.at[0:2 * q]
    xb_ref = x_ref.at[2 * q:3 * q]

    def project(x_rows, z_ref):
        z_ref[...] = _dot(_rms(x_rows, g_ref[...]).astype(BF16), win_ref[...])

    scan = functools.partial(_scan_chunk, cw_ref=cw_ref, cb_ref=cb_ref, dtb_ref=dtb_ref, alog_ref=alog_ref,
                             dsk_ref=dsk_ref, nw_ref=nw_ref, wout_ref=wout_ref,
                             xpad_ref=xpad_ref, ht_ref=ht_ref, ybuf_ref=ybuf_ref)

    @pl.when(s == 0)
    def _():
        xpad_ref[CONV_TAIL0:CONV_LEAD, :] = cbuf_ref[...]
        ht_ref[...] = st_ref[...].T
        project(xa_ref[0:q, :], za_ref)

    x1 = xa_ref[q:2 * q, :]
    project(x1, zb_ref)
    o_ref[0:q, :], _ = scan(za_ref, xa_ref[0:q, :])
    project(xb_ref[...], za_ref)
    o_ref[q:2 * q, :], tail = scan(zb_ref, x1)

    @pl.when(s == ns - 1)
    def _():
        nconv_ref[...] = tail
        nst_ref[...] = ht_ref[...].T


def _ssd_prompt(x, g, w_in, cbuf, st, cw, cb, dtb, alog, dsk, nw, w_out):
    nb = cbuf.shape[0]
    nc = SEQ // SSD_CHUNK
    ns = nc // 2
    q = SSD_CHUNK
    assert x.shape[0] >= nb * SEQ + q
    vec = lambda n: pl.BlockSpec((1, n), lambda b, c: (0, 0))
    return pl.pallas_call(
        functools.partial(_ssd_prompt_kernel, ns=ns),
        grid=(nb, ns),
        in_specs=[
            pl.BlockSpec((pl.Element(3 * q), pl.Element(D_MODEL)), lambda b, s: ((b * ns + s) * 2 * q, 0)),
            vec(D_MODEL),
            _resident((D_MODEL, IN_PAD_P)),
            pl.BlockSpec((None, CONV_W - 1, CONV_DIM), lambda b, c: (b, 0, 0)),
            pl.BlockSpec((None, D_INNER, D_STATE), lambda b, c: (b, 0, 0)),
            pl.BlockSpec((CONV_W, CONV_DIM), lambda b, c: (0, 0)),
            vec(CONV_DIM), vec(LANES), vec(LANES), vec(D_INNER), vec(D_INNER),
            _resident((D_INNER, D_MODEL)),
        ],
        out_specs=[
            pl.BlockSpec((2 * q, D_MODEL), lambda b, s: (b * ns + s, 0)),
            pl.BlockSpec((None, CONV_W - 1, CONV_DIM), lambda b, c: (b, 0, 0)),
            pl.BlockSpec((None, D_INNER, D_STATE), lambda b, c: (b, 0, 0)),
        ],
        out_shape=[
            jax.ShapeDtypeStruct(x.shape, F32),
            jax.ShapeDtypeStruct((nb, CONV_W - 1, CONV_DIM), F32),
            jax.ShapeDtypeStruct((nb, D_INNER, D_STATE), F32),
        ],
        scratch_shapes=[
            pltpu.VMEM((q, IN_PAD_P), F32),
            pltpu.VMEM((q, IN_PAD_P), F32),
            pltpu.VMEM((q + CONV_LEAD, CONV_DIM), F32),
            pltpu.VMEM((D_STATE, D_INNER), F32),
            pltpu.VMEM((q, D_INNER), F32),
        ],
        input_output_aliases={0: 0},
        compiler_params=_params(2),
        name="ssd_prompt",
    )(x, g, w_in, cbuf, st, cw, cb, dtb, alog, dsk, nw, w_out)


def _ssd_sample_kernel(zx_ref, cbuf_ref, st_ref, cw_ref, cb_ref, dtb_ref, alog_ref, dsk_ref, nw_ref, *rest, tb):
    y_ref, nconv_ref, nst_ref, xpad_ref = rest[-4:]
    q = DEC_SEQ
    lead = 8
    tail0 = lead - (CONV_W - 1)
    gw = D_INNER // SSD_GROUPS
    a = -jnp.exp(alog_ref[...])
    rows = lax.broadcasted_iota(jnp.int32, (q, 1), 0)
    xws, cds, bms = [], [], []
    for b in range(tb):
        r0 = b * q
        xpad_ref[b, tail0:lead, :] = cbuf_ref[b]
        xpad_ref[b, lead:lead + q, :] = zx_ref[r0:r0 + q, XBC_OFF:DT_OFF]
        conv = cw_ref[0:1, :] * xpad_ref[b, tail0:tail0 + q, :]
        for k in range(1, CONV_W):
            conv = conv + cw_ref[k:k + 1, :] * xpad_ref[b, tail0 + k:tail0 + k + q, :]
        conv = cb_ref[...] + conv
        nconv_ref[b] = xpad_ref[b, q + tail0:q + lead, :]

        xbc = _silu(conv)
        xs = xbc[:, :D_INNER]
        bm = xbc[:, D_INNER:D_INNER + GN]
        cm = xbc[:, D_INNER + GN:]

        dt = _softplus(zx_ref[r0:r0 + q, DT_OFF:DT_OFF + D_INNER] + dtb_ref[...])
        da = dt * a
        acs = jnp.where(rows >= 0, da[0:1, :], 0.0)
        for k in range(1, q):
            acs = acs + jnp.where(rows >= k, da[k:k + 1, :], 0.0)
        xdt = xs * dt

        y = jnp.zeros((q, D_INNER), F32)
        for k in range(q):
            decay = jnp.exp(jnp.where(rows >= k, acs - acs[k:k + 1, :], -jnp.inf))
            cb_k = jnp.concatenate(
                [jnp.broadcast_to(
                    jnp.sum(cm[:, g * D_STATE:(g + 1) * D_STATE] * bm[k:k + 1, g * D_STATE:(g + 1) * D_STATE],
                            axis=-1, keepdims=True), (q, gw)) for g in range(SSD_GROUPS)], axis=1)
            y = y + cb_k * decay * xdt[k:k + 1, :]

        cm_rows = jnp.concatenate([cm[:, g * D_STATE:(g + 1) * D_STATE] for g in range(SSD_GROUPS)], axis=0)
        r = lax.dot_general(cm_rows.astype(BF16), st_ref[b].astype(BF16), (((1,), (1,)), ((), ())),
                            preferred_element_type=F32)
        y_off = jnp.concatenate([r[g * q:(g + 1) * q, g * gw:(g + 1) * gw] for g in range(SSD_GROUPS)], axis=1)
        y = y + jnp.exp(acs) * y_off
        y = y + xs * dsk_ref[...]

        y = y * _silu(zx_ref[r0:r0 + q, :D_INNER])
        for g in range(SSD_GROUPS):
            yg = y[:, g * gw:(g + 1) * gw]
            yg = yg * lax.rsqrt(jnp.mean(yg * yg, axis=-1, keepdims=True) + EPS)
            y_ref[r0:r0 + q, g * gw:(g + 1) * gw] = yg * nw_ref[:, g * gw:(g + 1) * gw]

        last = acs[q - 1:q, :]
        xws.append(xdt * jnp.exp(last - acs))
        cds.append(jnp.exp(last))
        bms.append(bm)

    padded = jnp.concatenate(xws + [jnp.zeros((LANES - tb * q, D_INNER), F32)], axis=0)
    padded_t = padded.T.astype(BF16)
    zero_blk = jnp.zeros((q, D_STATE), F32)
    for b0 in range(0, tb, 2):
        for g in range(SSD_GROUPS):
            blocks = []
            for i in range(LANES // q):
                left = bms[b0][:, g * D_STATE:(g + 1) * D_STATE] if i == b0 else zero_blk
                right = bms[b0 + 1][:, g * D_STATE:(g + 1) * D_STATE] if i == b0 + 1 else zero_blk
                blocks.append(jnp.concatenate([left, right], axis=1))
            rhs = jnp.concatenate(blocks, axis=0).astype(BF16)
            st_pair = _dot(padded_t[g * gw:(g + 1) * gw, :], rhs)
            for half, b in enumerate((b0, b0 + 1)):
                for hh in range(HEADS_PER_GROUP):
                    lo = g * gw + hh * SSD_HEAD_DIM
                    cd_h = jnp.broadcast_to(cds[b][:, lo:lo + 1], (SSD_HEAD_DIM, D_STATE))
                    nst_ref[b, lo:lo + SSD_HEAD_DIM, :] = (
                        st_ref[b, lo:lo + SSD_HEAD_DIM, :] * cd_h
                        + st_pair[hh * SSD_HEAD_DIM:(hh + 1) * SSD_HEAD_DIM, half * D_STATE:(half + 1) * D_STATE])


def _ssd_sample(zx, cbuf, st, layer, cw, cb, dtb, alog, dsk, nw, prev_states, *, tb=8):
    q = DEC_SEQ
    vec = lambda n: pl.BlockSpec((1, n), lambda i: (0, 0))
    n_layers = st.shape[0]
    in_specs = [
        pl.BlockSpec((tb * q, IN_PAD_S), lambda i: (i, 0)),
        pl.BlockSpec((None, tb, CONV_W - 1, CONV_DIM), lambda i: (layer, i, 0, 0)),
        pl.BlockSpec((None, tb, D_INNER, D_STATE), lambda i: (layer, i, 0, 0)),
        pl.BlockSpec((CONV_W, CONV_DIM), lambda i: (0, 0)),
        vec(CONV_DIM), vec(D_INNER), vec(D_INNER), vec(D_INNER), vec(D_INNER),
    ]
    args = [zx, cbuf, st, cw, cb, dtb, alog, dsk, nw]
    aliases = {}
    if prev_states is not None:
        in_specs.append(pl.BlockSpec(memory_space=pl.ANY))
        args.append(prev_states)
        aliases = {len(args) - 1: 2}
    return pl.pallas_call(
        functools.partial(_ssd_sample_kernel, tb=tb),
        grid=(DEC_BATCH // tb,),
        in_specs=in_specs,
        out_specs=[
            pl.BlockSpec((tb * q, D_INNER), lambda i: (i, 0)),
            pl.BlockSpec((tb, CONV_W - 1, CONV_DIM), lambda i: (i, 0, 0)),
            pl.BlockSpec((None, tb, D_INNER, D_STATE), lambda i: (layer, i, 0, 0)),
        ],
        out_shape=[
            jax.ShapeDtypeStruct((ROWS_S, D_INNER), F32),
            jax.ShapeDtypeStruct((DEC_BATCH, CONV_W - 1, CONV_DIM), F32),
            jax.ShapeDtypeStruct((n_layers, DEC_BATCH, D_INNER, D_STATE), F32),
        ],
        scratch_shapes=[pltpu.VMEM((tb, q + 8, CONV_DIM), F32)],
        input_output_aliases=aliases,
        compiler_params=_params(1),
        name="ssd_sample",
    )(*args)


POOL_LEAD = 32


def _pool_kernel(x_ref, g_ref, buf_ref, pw_ref, ps_ref, o_ref, np_ref, xx_ref, s2_ref, s4_ref, s8_ref,
                 *, tb, tl, nl, pos0):
    l = pl.program_id(1)
    x = x_ref[...]
    u = _rms(x, g_ref[...]).reshape(tb, tl, D_MODEL)
    lead = POOL_LEAD
    b0 = lead - POOL_BUF
    end = lead + tl
    gw = POOL_GW

    @pl.when(l == 0)
    def _():
        xx_ref[:, 0:b0, :] = jnp.zeros((tb, b0, D_MODEL), F32)
        xx_ref[:, b0:lead, :] = buf_ref[...]

    xx_ref[:, lead:end, :] = u
    s2_ref[:, 8:end, :] = xx_ref[:, 8:end, :] + xx_ref[:, 7:end - 1, :]
    s4_ref[:, 16:end, :] = s2_ref[:, 16:end, 2 * gw:] + s2_ref[:, 14:end - 2, 2 * gw:]
    s8_ref[:, 24:end, :] = s4_ref[:, 24:end, gw:] + s4_ref[:, 20:end - 4, gw:]
    sums = (s2_ref[:, lead:end, 0:gw],
            s2_ref[:, lead:end, gw:2 * gw] + s2_ref[:, lead - 2:end - 2, gw:2 * gw],
            s4_ref[:, lead:end, 0:gw] + s4_ref[:, lead - 4:end - 4, 0:gw],
            s8_ref[:, lead:end, :] + s8_ref[:, lead - 8:end - 8, :])

    pos = (pos0 + l * tl + lax.broadcasted_iota(jnp.int32, (1, tl, 1), 1)).astype(F32)
    outs = []
    for gi, w in enumerate(POOL_WINDOWS):
        lo = gi * gw
        cnt = jnp.minimum(pos + 1.0, float(w))
        mix = sums[gi] / cnt - u[:, :, lo:lo + gw]
        outs.append(_dot(mix.reshape(tb * tl, gw).astype(BF16), pw_ref[gi]))
    o_ref[...] = x + jnp.concatenate(outs, axis=-1) * ps_ref[...]

    tail = xx_ref[:, end - POOL_BUF:end, :]
    if nl > 1:
        xx_ref[:, b0:lead, :] = tail

    @pl.when(l == nl - 1)
    def _():
        np_ref[...] = tail


def _pool(x, g, buf_arr, buf_index, pw, ps, *, row_start, nb, seq, tb, tl, pos0, name):
    nl = seq // tl
    r0 = row_start // (tb * tl)
    x_spec = pl.BlockSpec((tb * tl, D_MODEL), lambda b, l: (r0 + b * nl + l, 0))
    return pl.pallas_call(
        functools.partial(_pool_kernel, tb=tb, tl=tl, nl=nl, pos0=pos0),
        grid=(nb // tb, nl),
        in_specs=[
            x_spec,
            pl.BlockSpec((1, D_MODEL), lambda b, l: (0, 0)),
            buf_index(tb),
            pl.BlockSpec((len(POOL_WINDOWS), POOL_GW, POOL_GW), lambda b, l: (0, 0, 0)),
            pl.BlockSpec((1, D_MODEL), lambda b, l: (0, 0)),
        ],
        out_specs=[x_spec, pl.BlockSpec((tb, POOL_BUF, D_MODEL), lambda b, l: (b, 0, 0))],
        out_shape=[jax.ShapeDtypeStruct(x.shape, F32),
                   jax.ShapeDtypeStruct((nb, POOL_BUF, D_MODEL), F32)],
        scratch_shapes=[pltpu.VMEM((tb, POOL_LEAD + tl, D_MODEL), F32),
                        pltpu.VMEM((tb, POOL_LEAD + tl, D_MODEL), F32),
                        pltpu.VMEM((tb, POOL_LEAD + tl, D_MODEL - 2 * POOL_GW), F32),
                        pltpu.VMEM((tb, POOL_LEAD + tl, POOL_GW), F32)],
        input_output_aliases={0: 0},
        compiler_params=_params(2),
        name=name,
    )(x, g, buf_arr, pw, ps)


def _head_kv(ref, h):
    return jnp.concatenate([ref[pl.ds(c * MEM_HEADS + h, N_MEM, stride=MEM_CHUNKS), :]
                            for c in range(MEM_HEAD_DIM // LANES)], axis=1).astype(BF16)


def _xattn_prompt_kernel(x_ref, g_ref, wq_ref, k_ref, v_ref, wo_ref, o_ref):
    x = x_ref[...]
    u = _rms(x, g_ref[...]).astype(BF16)
    q = (_dot(u, wq_ref[...]) * (MEM_HEAD_DIM ** -0.5)).astype(BF16)
    outs = []
    for h in range(MEM_HEADS):
        q_h = q[:, h * MEM_HEAD_DIM:(h + 1) * MEM_HEAD_DIM]
        s = lax.dot_general(q_h, _head_kv(k_ref, h), (((1,), (1,)), ((), ())), preferred_element_type=F32)
        e = jnp.exp(s - jnp.max(s, axis=-1, keepdims=True))
        den = jnp.sum(e, axis=-1, keepdims=True)
        outs.append(_dot(e.astype(BF16), _head_kv(v_ref, h)) / den)
    o = jnp.concatenate(outs, axis=-1)
    o_ref[...] = x + _dot(o.astype(BF16), wo_ref[...])


def _xattn_prompt(x, g, wq, mk, mv, wo, layer, *, tl=1024):
    nl = SEQ // tl
    x_spec = pl.BlockSpec((tl, D_MODEL), lambda b, l: (b * nl + l, 0))
    w_spec = _resident((D_MODEL, D_MODEL), layer)
    kv_spec = pl.BlockSpec((None, N_MEM * MEM_CHUNKS, LANES), lambda b, l: (layer, b, 0))
    return pl.pallas_call(
        _xattn_prompt_kernel,
        grid=(BATCH, nl),
        in_specs=[x_spec, pl.BlockSpec((1, D_MODEL), lambda b, l: (0, 0)), w_spec, kv_spec, kv_spec, w_spec],
        out_specs=x_spec,
        out_shape=jax.ShapeDtypeStruct(x.shape, F32),
        input_output_aliases={0: 0},
        compiler_params=_params(2),
        name="xattn_prompt",
    )(x, g, wq, mk, mv, wo)


def _chunk_rows(t):
    lead = t.shape[:-3]
    t = t.reshape(*lead, N_MEM, MEM_HEADS, MEM_HEAD_DIM // LANES, LANES)
    return jnp.swapaxes(t, -2, -3).reshape(*lead, N_MEM * MEM_CHUNKS, LANES)


def _from_chunk_rows(t):
    t = t.reshape(DEPTH, BATCH, N_MEM, MEM_HEAD_DIM // LANES, MEM_HEADS, LANES)
    return jnp.swapaxes(t, -2, -3).reshape(DEPTH, BATCH, N_MEM, MEM_HEADS, MEM_HEAD_DIM)


def _chunk_to_col(j):
    return ((j % MEM_HEADS) * (MEM_HEAD_DIM // LANES) + j // MEM_HEADS) * LANES


def _attn_sample_kernel(x_ref, g_ref, wq_ref, k_ref, v_ref, wo_ref, o_ref, obuf_ref, *, tb):
    n = DEC_SEQ
    rows = lax.broadcasted_iota(jnp.int32, (MEM_HEADS * n, D_MODEL), 0) // n
    chunk = lax.broadcasted_iota(jnp.int32, (MEM_HEADS * n, D_MODEL), 1) // LANES
    own = rows == chunk % MEM_HEADS

    def gather(ref, b):
        return jnp.concatenate([ref[b, pl.ds(j, N_MEM, stride=MEM_CHUNKS), :] for j in range(MEM_CHUNKS)],
                               axis=1).astype(BF16)

    x = x_ref[...]
    q = _dot(_rms(x, g_ref[...]).astype(BF16), wq_ref[...]) * (MEM_HEAD_DIM ** -0.5)
    hn = MEM_HEADS * n
    scores = []
    for b in range(tb):
        q_b = q[b * n:(b + 1) * n, :]
        q_c = jnp.concatenate([q_b[:, _chunk_to_col(j):_chunk_to_col(j) + LANES] for j in range(MEM_CHUNKS)], axis=1)
        q_blk = jnp.where(own, jnp.concatenate([q_c] * MEM_HEADS, axis=0), 0.0).astype(BF16)
        scores.append(lax.dot_general(q_blk, gather(k_ref, b), (((1,), (1,)), ((), ())),
                                      preferred_element_type=F32))
    s = jnp.concatenate(scores, axis=0)
    e = jnp.exp(s - jnp.max(s, axis=-1, keepdims=True))
    den = jnp.sum(e, axis=-1, keepdims=True)
    e = e.astype(BF16)
    for b in range(tb):
        o = _dot(e[b * hn:(b + 1) * hn, :], gather(v_ref, b)) / den[b * hn:(b + 1) * hn, :]
        for j in range(MEM_CHUNKS):
            h = j % MEM_HEADS
            obuf_ref[b * n:(b + 1) * n, _chunk_to_col(j):_chunk_to_col(j) + LANES] = (
                o[h * n:(h + 1) * n, j * LANES:(j + 1) * LANES])
    o_ref[...] = x + _dot(obuf_ref[...].astype(BF16), wo_ref[...])


def _xattn_sample(x, g, wq, ck, cv, wo, layer, *, tb=8):
    n = DEC_SEQ
    r0 = ROWS_P // (tb * n)
    kv_spec = pl.BlockSpec((None, tb, N_MEM * MEM_CHUNKS, LANES), lambda i: (layer, i, 0, 0))
    row_spec = pl.BlockSpec((tb * n, D_MODEL), lambda i: (r0 + i, 0))
    w_spec = _resident((D_MODEL, D_MODEL), layer)
    return pl.pallas_call(
        functools.partial(_attn_sample_kernel, tb=tb),
        grid=(DEC_BATCH // tb,),
        in_specs=[row_spec, _resident((1, D_MODEL)), w_spec, kv_spec, kv_spec, w_spec],
        out_specs=row_spec,
        out_shape=jax.ShapeDtypeStruct(x.shape, F32),
        scratch_shapes=[pltpu.VMEM((tb * n, D_MODEL), F32)],
        input_output_aliases={0: 0},
        compiler_params=_params(1),
        name="xattn_sample",
    )(x, g, wq, ck, cv, wo)


def _row(v):
    return v.reshape(1, -1)


def _per_channel(v):
    return jnp.repeat(v, SSD_HEAD_DIM).reshape(1, D_INNER)


def _pad_lanes(v):
    return jnp.pad(v, (0, LANES - v.shape[0])).reshape(1, LANES)


def kernel(x_prompt, x_sample, mem_prompt, cache_mem_k, cache_mem_v, state_ssm, state_conv, state_pool, norm_ffn1, ffn1_w_gate, ffn1_w_up, ffn1_w_down, norm_mix, ssd_in_w, ssd_conv_w, ssd_conv_b, ssd_dt_bias, ssd_a_log, ssd_d, ssd_norm_w, ssd_out_w, pool_w, pool_scale, norm_cross, norm_mem, xa_wq, xa_wk, xa_wv, xa_wo, norm_ffn2, ffn2_w_gate, ffn2_w_up, ffn2_w_down, final_norm):
    bf = lambda w: w.astype(BF16)
    ffn1_w = (bf(ffn1_w_gate), bf(ffn1_w_up), bf(ffn1_w_down))
    ffn2_w = (bf(ffn2_w_gate), bf(ffn2_w_up), bf(ffn2_w_down))
    xa_wq_b, xa_wo_b = bf(xa_wq), bf(xa_wo)
    mk_p, mv_p = _mem_kv(mem_prompt.reshape(BATCH * N_MEM, D_MODEL), norm_mem.reshape(DEPTH, 1, D_MODEL),
                         bf(xa_wk), bf(xa_wv))
    ck = _chunk_rows(cache_mem_k)
    cv = _chunk_rows(cache_mem_v)
    st_s = state_ssm.reshape(-1, DEC_BATCH, D_INNER, D_STATE)
    conv0 = jnp.zeros((BATCH, CONV_W - 1, CONV_DIM), F32)
    ssm0 = jnp.zeros((BATCH, D_INNER, D_STATE), F32)
    pool0 = jnp.zeros((BATCH, POOL_BUF, D_MODEL), F32)

    ssm_p, conv_p, pool_p, conv_s, pool_s = [], [], [], [], []
    ssm_s = None
    for i in range(DEPTH):
        j = i // 2
        ffn1 = (_row(norm_ffn1[i]), *ffn1_w, i)
        if i == 0:
            x = _ffn(x_prompt.reshape(ROWS_P, D_MODEL), *ffn1, src2=x_sample.reshape(ROWS_S, D_MODEL))
        else:
            x = _ffn(x, *ffn1)
        g_mix = _row(norm_mix[i])
        if i % 2 == 0:
            w_in = ssd_in_w[j]
            w_dt = w_in[:, DT_OFF:]
            w_p = bf(jnp.concatenate(
                [w_in[:, :DT_OFF], jnp.pad(w_dt, ((0, 0), (0, IN_PAD_P - DT_OFF - SSD_HEADS)))], axis=1))
            w_s = bf(jnp.concatenate([w_in[:, :DT_OFF], jnp.repeat(w_dt, SSD_HEAD_DIM, axis=1)], axis=1))
            cw, cb = ssd_conv_w[j], _row(ssd_conv_b[j])
            dsk, nw, w_out = _per_channel(ssd_d[j]), _row(ssd_norm_w[j]), bf(ssd_out_w[j])

            x, nc_p, ns_p = _ssd_prompt(x, g_mix, w_p, conv0, ssm0, cw, cb, _pad_lanes(ssd_dt_bias[j]),
                                        _pad_lanes(ssd_a_log[j]), dsk, nw, w_out)
            zx_s = _rms_matmul(x, g_mix, w_s, row_start=ROWS_P, n_rows=ROWS_S, tm=1024, tn=1792,
                               name="in_proj_s")
            y_s, nc_s, ssm_s = _ssd_sample(zx_s, state_conv, st_s, j, cw, cb, _per_channel(ssd_dt_bias[j]),
                                           _per_channel(ssd_a_log[j]), dsk, nw, ssm_s)
            x = _matmul_residual(x, y_s, w_out, row_start=ROWS_P, tm=1024, name="out_proj_s")
            conv_p.append(nc_p)
            conv_s.append(nc_s)
            ssm_p.append(ns_p.reshape(BATCH, SSD_HEADS, SSD_HEAD_DIM, D_STATE))
        else:
            pw, ps = bf(pool_w[j]), _row(pool_scale[j])
            x, np_p = _pool(x, g_mix, pool0, lambda tb: pl.BlockSpec((tb, POOL_BUF, D_MODEL), lambda b, l: (b, 0, 0)),
                            pw, ps, row_start=0, nb=BATCH, seq=SEQ, tb=1, tl=1024, pos0=0, name="pool_p")
            x, np_s = _pool(x, g_mix, state_pool,
                            lambda tb: pl.BlockSpec((None, tb, POOL_BUF, D_MODEL), lambda b, l: (j, b, 0, 0)),
                            pw, ps, row_start=ROWS_P, nb=DEC_BATCH, seq=DEC_SEQ, tb=8, tl=DEC_SEQ,
                            pos0=PAST_LEN, name="pool_s")
            pool_p.append(np_p)
            pool_s.append(np_s)

        g_x = _row(norm_cross[i])
        x = _xattn_prompt(x, g_x, xa_wq_b, mk_p, mv_p, xa_wo_b, i)
        x = _xattn_sample(x, g_x, xa_wq_b, ck, cv, xa_wo_b, i)

        ffn2 = (_row(norm_ffn2[i]), *ffn2_w, i)
        if i < DEPTH - 1:
            x = _ffn(x, *ffn2)

    g_f = _row(final_norm)
    y_prompt = _ffn(x, *ffn2, n_rows=ROWS_P, final_g=g_f).reshape(BATCH, SEQ, D_MODEL)
    y_sample = _ffn(x, *ffn2, src_row_start=ROWS_P, n_rows=ROWS_S, final_g=g_f).reshape(DEC_BATCH, DEC_SEQ, D_MODEL)
    return (y_prompt, y_sample, jnp.stack(ssm_p), jnp.stack(conv_p), jnp.stack(pool_p),
            _from_chunk_rows(mk_p), _from_chunk_rows(mv_p),
            ssm_s.reshape(state_ssm.shape), jnp.stack(conv_s), jnp.stack(pool_s))
```

```python
import functools

import jax
import jax.numpy as jnp
from jax import lax
from jax.experimental import pallas as pl
from jax.experimental.pallas import tpu as pltpu

F32 = jnp.float32
BF16 = jnp.bfloat16

D_MODEL = 1024
BATCH = 8
SEQ = 2048
DEPTH = 4
DEC_BATCH = 128
DEC_SEQ = 8
PAST_LEN = 16384
D_FF = 2816
D_INNER = 2048
SSD_HEAD_DIM = 64
SSD_HEADS = 32
SSD_GROUPS = 4
HEADS_PER_GROUP = 8
D_STATE = 128
CONV_W = 4
GN = SSD_GROUPS * D_STATE
CONV_DIM = D_INNER + 2 * GN
SSD_CHUNK = 128
POOL_WINDOWS = (2, 4, 8, 16)
POOL_GW = 256
POOL_BUF = 15
N_MEM = 256
MEM_HEADS = 4
MEM_HEAD_DIM = 256
EPS = 1e-5

ROWS_P = BATCH * SEQ
ROWS_S = DEC_BATCH * DEC_SEQ
ROWS = ROWS_P + ROWS_S

LANES = 128
XBC_OFF = D_INNER
DT_OFF = D_INNER + CONV_DIM
IN_PAD_P = 5376
IN_PAD_S = DT_OFF + D_INNER
VMEM_LIMIT = 48 * 1024 * 1024
MXU_TILE = 256
FFN_CHUNK = MXU_TILE
MEM_CHUNKS = D_MODEL // LANES
SAMPLE_TB = 8


def _params(n_axes, vmem=VMEM_LIMIT):
    return pltpu.CompilerParams(dimension_semantics=("arbitrary",) * n_axes,
                                vmem_limit_bytes=vmem)


def _rms(x, g):
    r = lax.rsqrt(jnp.mean(x * x, axis=-1, keepdims=True) + EPS)
    return x * r * g


def _silu(x):
    return x * (1.0 / (1.0 + jnp.exp(-x)))


def _softplus(x):
    return jnp.maximum(x, 0.0) + jnp.log(1.0 + jnp.exp(-jnp.abs(x)))


def _dot(a, b):
    return jnp.dot(a, b, preferred_element_type=F32)


def _ffn_kernel(*refs, final, n_first):
    x_ref, g_ref, wg_ref, wu_ref, wd_ref = refs[:5]
    o_ref, h_ref = refs[-2:]
    x = x_ref[...]
    if n_first is not None:
        x = jnp.where(pl.program_id(0) < n_first, x, refs[-3][...])
    u = _rms(x, g_ref[...]).astype(BF16)
    for lo in range(0, D_FF, FFN_CHUNK):
        a = _dot(u, wg_ref[:, lo:lo + FFN_CHUNK])
        b = _dot(u, wu_ref[:, lo:lo + FFN_CHUNK])
        h_ref[:, lo:lo + FFN_CHUNK] = (_silu(a) * b).astype(BF16)
    y = x + 0.5 * _dot(h_ref[...], wd_ref[...])
    o_ref[...] = _rms(y, refs[5][...]) if final else y


def _resident(shape, layer=None):
    if layer is None:
        return pl.BlockSpec(shape, lambda *_: (0,) * len(shape), pipeline_mode=pl.Buffered(1))
    return pl.BlockSpec((None,) + tuple(shape), lambda *_: (layer,) + (0,) * len(shape),
                        pipeline_mode=pl.Buffered(1))


def _ffn(src, g, wg, wu, wd, layer, *, src_row_start=0, n_rows=None, src2=None, final_g=None, tm=1024):
    n_rows = src.shape[0] if n_rows is None else n_rows
    s0, n1 = src_row_start // tm, n_rows // tm
    in_specs = [pl.BlockSpec((tm, D_MODEL), lambda i: (s0 + jnp.minimum(i, n1 - 1), 0)), _resident((1, D_MODEL)),
                _resident((D_MODEL, D_FF), layer), _resident((D_MODEL, D_FF), layer),
                _resident((D_FF, D_MODEL), layer)]
    args = [src, g, wg, wu, wd]
    if final_g is not None:
        in_specs.append(_resident((1, D_MODEL)))
        args.append(final_g)
    n2 = 0
    if src2 is not None:
        n2 = src2.shape[0] // tm
        in_specs.append(pl.BlockSpec((tm, D_MODEL), lambda i: (jnp.maximum(i - n1, 0), 0)))
        args.append(src2)
    return pl.pallas_call(
        functools.partial(_ffn_kernel, final=final_g is not None, n_first=n1 if src2 is not None else None),
        grid=(n1 + n2,),
        in_specs=in_specs,
        out_specs=pl.BlockSpec((tm, D_MODEL), lambda i: (i, 0)),
        out_shape=jax.ShapeDtypeStruct(((n1 + n2) * tm, D_MODEL), F32),
        scratch_shapes=[pltpu.VMEM((tm, D_FF), BF16)],
        compiler_params=_params(1),
        name="ffn",
    )(*args)


def _rms_mm_kernel(x_ref, g_ref, w_ref, o_ref, u_ref):
    @pl.when(pl.program_id(1) == 0)
    def _():
        u_ref[...] = _rms(x_ref[...], g_ref[...]).astype(BF16)

    o_ref[...] = _dot(u_ref[...], w_ref[...])


def _rms_matmul(x, g, w, *, row_start, n_rows, tm, tn, name):
    n = w.shape[1]
    r0 = row_start // tm
    return pl.pallas_call(
        _rms_mm_kernel,
        grid=(n_rows // tm, n // tn),
        in_specs=[
            pl.BlockSpec((tm, D_MODEL), lambda i, j: (r0 + i, 0)),
            pl.BlockSpec((1, D_MODEL), lambda i, j: (0, 0)),
            pl.BlockSpec((D_MODEL, tn), lambda i, j: (0, j)),
        ],
        out_specs=pl.BlockSpec((tm, tn), lambda i, j: (i, j)),
        out_shape=jax.ShapeDtypeStruct((n_rows, n), F32),
        scratch_shapes=[pltpu.VMEM((tm, D_MODEL), BF16)],
        compiler_params=_params(2),
        name=name,
    )(x, g, w)


def _mem_kv_kernel(m_ref, g_ref, wk_ref, wv_ref, k_ref, v_ref, *, tm):
    u = _rms(m_ref[...], g_ref[...]).astype(BF16)
    for w_ref, o_ref in ((wk_ref, k_ref), (wv_ref, v_ref)):
        kv = _dot(u, w_ref[...])
        for j in range(MEM_CHUNKS):
            o_ref[pl.ds(j, tm, stride=MEM_CHUNKS), :] = kv[:, _chunk_to_col(j):_chunk_to_col(j) + LANES]


def _mem_kv(mem, g, wk, wv, *, tm=512):
    rows = mem.shape[0]
    w_spec = pl.BlockSpec((None, D_MODEL, D_MODEL), lambda l, r: (l, 0, 0))
    o_spec = pl.BlockSpec((None, tm * MEM_CHUNKS, LANES), lambda l, r: (l, r, 0))
    o_shape = jax.ShapeDtypeStruct((DEPTH, rows * MEM_CHUNKS, LANES), F32)
    return pl.pallas_call(
        functools.partial(_mem_kv_kernel, tm=tm),
        grid=(DEPTH, rows // tm),
        in_specs=[
            pl.BlockSpec((tm, D_MODEL), lambda l, r: (r, 0)),
            pl.BlockSpec((None, 1, D_MODEL), lambda l, r: (l, 0, 0)),
            w_spec, w_spec,
        ],
        out_specs=[o_spec, o_spec],
        out_shape=[o_shape, o_shape],
        compiler_params=_params(2),
        name="mem_kv",
    )(mem, g, wk, wv)


def _mm_res_kernel(x_ref, y_ref, w_ref, o_ref):
    o_ref[...] = x_ref[...] + _dot(y_ref[...].astype(BF16), w_ref[...])


def _matmul_residual(x, y, w, *, row_start, tm, name):
    n_rows, k = y.shape
    r0 = row_start // tm
    return pl.pallas_call(
        _mm_res_kernel,
        grid=(n_rows // tm,),
        in_specs=[
            pl.BlockSpec((tm, D_MODEL), lambda i: (r0 + i, 0)),
            pl.BlockSpec((tm, k), lambda i: (i, 0)),
            pl.BlockSpec((k, D_MODEL), lambda i: (0, 0)),
        ],
        out_specs=pl.BlockSpec((tm, D_MODEL), lambda i: (r0 + i, 0)),
        out_shape=jax.ShapeDtypeStruct(x.shape, F32),
        input_output_aliases={0: 0},
        compiler_params=_params(1),
        name=name,
    )(x, y, w)


CONV_LEAD = 8
CONV_TAIL0 = CONV_LEAD - (CONV_W - 1)


def _scan_chunk(zx, x_res, cw_ref, cb_ref, dtb_ref, alog_ref, dsk_ref, nw_ref, wout_ref,
                xpad_ref, ht_ref, ybuf_ref):
    q = SSD_CHUNK
    lead, tail0 = CONV_LEAD, CONV_TAIL0
    xpad_ref[lead:lead + q, :] = zx[:, XBC_OFF:DT_OFF]
    conv = cw_ref[0:1, :] * xpad_ref[tail0:tail0 + q, :]
    for k in range(1, CONV_W):
        conv = conv + cw_ref[k:k + 1, :] * xpad_ref[tail0 + k:tail0 + k + q, :]
    conv = cb_ref[...] + conv
    tail = xpad_ref[q + tail0:q + lead, :]
    xpad_ref[tail0:lead, :] = tail

    xbc = _silu(conv)
    xs = xbc[:, :D_INNER]
    bm = xbc[:, D_INNER:D_INNER + GN]
    cm = xbc[:, D_INNER + GN:]

    dt = _softplus(zx[:, DT_OFF:DT_OFF + LANES] + dtb_ref[...])
    a = -jnp.exp(alog_ref[...])
    da = dt * a
    rows = lax.broadcasted_iota(jnp.int32, (q, q), 0)
    cols = lax.broadcasted_iota(jnp.int32, (q, q), 1)
    causal = rows >= cols
    tri = jnp.where(causal, 1.0, 0.0).astype(F32)
    acs = jnp.dot(tri, da, precision=lax.Precision.HIGHEST, preferred_element_type=F32)
    acs_t = acs.T
    dt_t = dt.T
    src_t = acs_t - jnp.log(dt_t)
    eacs = jnp.exp(acs)
    last = acs_t[:, q - 1:q]
    w_t = jnp.exp(last - acs_t) * dt_t
    cd_t = jnp.exp(last)
    low_half = lax.broadcasted_iota(jnp.int32, (1, LANES), 1) < SSD_HEAD_DIM

    for g in range(SSD_GROUPS):
        bm_g = bm[:, g * D_STATE:(g + 1) * D_STATE]
        cm_g = cm[:, g * D_STATE:(g + 1) * D_STATE]
        bm_gt = bm_g.T
        cb_g = _dot(cm_g.astype(BF16), bm_gt.astype(BF16))
        for hh in range(0, HEADS_PER_GROUP, 2):
            h0 = g * HEADS_PER_GROUP + hh
            lo = (h0 // 2) * LANES
            x_pair = xs[:, lo:lo + LANES]
            ht_pair = ht_ref[:, lo:lo + LANES]
            x_pair_b = x_pair.astype(BF16)
            rhs = jnp.concatenate([x_pair_b, ht_pair.astype(BF16)], axis=0)
            ys, sts = [], []
            for h in (h0, h0 + 1):
                seg = acs[:, h:h + 1] - src_t[h:h + 1, :]
                m_h = cb_g * jnp.exp(jnp.where(causal, seg, -jnp.inf))
                e_h = eacs[:, h:h + 1] * cm_g
                lhs = jnp.concatenate([m_h.astype(BF16), e_h.astype(BF16)], axis=1)
                ys.append(_dot(lhs, rhs))
                b_h = bm_gt * w_t[h:h + 1, :]
                sts.append(_dot(b_h.astype(BF16), x_pair_b))
            y_pair = jnp.where(low_half, ys[0], ys[1]) + x_pair * dsk_ref[:, lo:lo + LANES]
            st_pair = jnp.where(low_half, sts[0], sts[1])
            cd_row = jnp.where(low_half, cd_t[h0:h0 + 1, :], cd_t[h0 + 1:h0 + 2, :])
            ht_ref[:, lo:lo + LANES] = ht_pair * cd_row + st_pair
            ybuf_ref[:, lo:lo + LANES] = y_pair

    y = ybuf_ref[...] * _silu(zx[:, :D_INNER])
    gw = D_INNER // SSD_GROUPS
    y_norm = []
    for g in range(SSD_GROUPS):
        yg = y[:, g * gw:(g + 1) * gw]
        yg = yg * lax.rsqrt(jnp.mean(yg * yg, axis=-1, keepdims=True) + EPS)
        y_norm.append((yg * nw_ref[:, g * gw:(g + 1) * gw]).astype(BF16))
    return x_res + _dot(jnp.concatenate(y_norm, axis=1), wout_ref[...]), tail


def _ssd_prompt_kernel(x_ref, g_ref, win_ref, cbuf_ref, st_ref, cw_ref, cb_ref, dtb_ref, alog_ref,
                       dsk_ref, nw_ref, wout_ref, o_ref, nconv_ref, nst_ref,
                       za_ref, zb_ref, xpad_ref, ht_ref, ybuf_ref, *, ns):
    s = pl.program_id(1)
    q = SSD_CHUNK
    xa_ref = x_ref.at[0:2 * q]
    xb_ref = x_ref.at[2 * q:3 * q]

    def project(x_rows, z_ref):
        z_ref[...] = _dot(_rms(x_rows, g_ref[...]).astype(BF16), win_ref[...])

    scan = functools.partial(_scan_chunk, cw_ref=cw_ref, cb_ref=cb_ref, dtb_ref=dtb_ref, alog_ref=alog_ref,
                             dsk_ref=dsk_ref, nw_ref=nw_ref, wout_ref=wout_ref,
                             xpad_ref=xpad_ref, ht_ref=ht_ref, ybuf_ref=ybuf_ref)

    @pl.when(s == 0)
    def _():
        xpad_ref[CONV_TAIL0:CONV_LEAD, :] = cbuf_ref[...]
        ht_ref[...] = st_ref[...].T
        project(xa_ref[0:q, :], za_ref)

    x1 = xa_ref[q:2 * q, :]
    project(x1, zb_ref)
    o_ref[0:q, :], _ = scan(za_ref, xa_ref[0:q, :])
    project(xb_ref[...], za_ref)
    o_ref[q:2 * q, :], tail = scan(zb_ref, x1)

    @pl.when(s == ns - 1)
    def _():
        nconv_ref[...] = tail
        nst_ref[...] = ht_ref[...].T


def _ssd_prompt(x, g, w_in, cbuf, st, cw, cb, dtb, alog, dsk, nw, w_out):
    nb = cbuf.shape[0]
    nc = SEQ // SSD_CHUNK
    ns = nc // 2
    q = SSD_CHUNK
    assert x.shape[0] >= nb * SEQ + q
    vec = lambda n: pl.BlockSpec((1, n), lambda b, c: (0, 0))
    return pl.pallas_call(
        functools.partial(_ssd_prompt_kernel, ns=ns),
        grid=(nb, ns),
        in_specs=[
            pl.BlockSpec((pl.Element(3 * q), pl.Element(D_MODEL)), lambda b, s: ((b * ns + s) * 2 * q, 0)),
            vec(D_MODEL),
            _resident((D_MODEL, IN_PAD_P)),
            pl.BlockSpec((None, CONV_W - 1, CONV_DIM), lambda b, c: (b, 0, 0)),
            pl.BlockSpec((None, D_INNER, D_STATE), lambda b, c: (b, 0, 0)),
            pl.BlockSpec((CONV_W, CONV_DIM), lambda b, c: (0, 0)),
            vec(CONV_DIM), vec(LANES), vec(LANES), vec(D_INNER), vec(D_INNER),
            _resident((D_INNER, D_MODEL)),
        ],
        out_specs=[
            pl.BlockSpec((2 * q, D_MODEL), lambda b, s: (b * ns + s, 0)),
            pl.BlockSpec((None, CONV_W - 1, CONV_DIM), lambda b, c: (b, 0, 0)),
            pl.BlockSpec((None, D_INNER, D_STATE), lambda b, c: (b, 0, 0)),
        ],
        out_shape=[
            jax.ShapeDtypeStruct(x.shape, F32),
            jax.ShapeDtypeStruct((nb, CONV_W - 1, CONV_DIM), F32),
            jax.ShapeDtypeStruct((nb, D_INNER, D_STATE), F32),
        ],
        scratch_shapes=[
            pltpu.VMEM((q, IN_PAD_P), F32),
            pltpu.VMEM((q, IN_PAD_P), F32),
            pltpu.VMEM((q + CONV_LEAD, CONV_DIM), F32),
            pltpu.VMEM((D_STATE, D_INNER), F32),
            pltpu.VMEM((q, D_INNER), F32),
        ],
        input_output_aliases={0: 0},
        compiler_params=_params(2),
        name="ssd_prompt",
    )(x, g, w_in, cbuf, st, cw, cb, dtb, alog, dsk, nw, w_out)


def _ssd_sample_kernel(zx_ref, cbuf_ref, st_ref, cw_ref, cb_ref, dtb_ref, alog_ref, dsk_ref, nw_ref, *rest, tb):
    y_ref, nconv_ref, nst_ref = rest[-3:]
    q = DEC_SEQ
    gw = D_INNER // SSD_GROUPS
    groups = [slice(g * D_STATE, (g + 1) * D_STATE) for g in range(SSD_GROUPS)]
    a = -jnp.exp(alog_ref[...])
    seq = lax.broadcasted_iota(jnp.int32, (tb, 1), 0)

    def tok(t, lo, hi):
        return zx_ref[t * tb:(t + 1) * tb, lo:hi]

    raw = [cbuf_ref[k] for k in range(CONV_W - 1)] + [tok(t, XBC_OFF, DT_OFF) for t in range(q)]
    for k in range(CONV_W - 1):
        nconv_ref[k] = raw[q + k]
    xs, bm, cm, acs, xdt = [], [], [], [], []
    for t in range(q):
        conv = cw_ref[0:1, :] * raw[t]
        for k in range(1, CONV_W):
            conv = conv + cw_ref[k:k + 1, :] * raw[t + k]
        xbc = _silu(cb_ref[...] + conv)
        xs.append(xbc[:, :D_INNER])
        bm.append(xbc[:, D_INNER:D_INNER + GN])
        cm.append(xbc[:, D_INNER + GN:])
        dt = _softplus(tok(t, DT_OFF, DT_OFF + D_INNER) + dtb_ref[...])
        acs.append(dt * a if t == 0 else acs[-1] + dt * a)
        xdt.append(xs[t] * dt)

    cm_rows = [jnp.concatenate([cm[t][:, sl] for t in range(q)], axis=0).astype(BF16) for sl in groups]
    y_off = {}
    for b in range(tb):
        h0 = st_ref[b].astype(BF16)
        for g in range(SSD_GROUPS):
            r = lax.dot_general(cm_rows[g], h0[g * gw:(g + 1) * gw, :], (((1,), (1,)), ((), ())),
                                preferred_element_type=F32)
            for t in range(q):
                prev = y_off.get((t, g), 0.0)
                y_off[t, g] = jnp.where(seq == b, r[t * tb:(t + 1) * tb, :], prev)

    for t in range(q):
        y = jnp.exp(acs[t]) * jnp.concatenate([y_off[t, g] for g in range(SSD_GROUPS)], axis=1)
        for k in range(t + 1):
            cb_tk = jnp.concatenate(
                [jnp.broadcast_to(jnp.sum(cm[t][:, sl] * bm[k][:, sl], axis=-1, keepdims=True), (tb, gw))
                 for sl in groups], axis=1)
            y = y + cb_tk * jnp.exp(acs[t] - acs[k]) * xdt[k]
        y = (y + xs[t] * dsk_ref[...]) * _silu(tok(t, 0, D_INNER))
        for g in range(SSD_GROUPS):
            yg = y[:, g * gw:(g + 1) * gw]
            yg = yg * lax.rsqrt(jnp.mean(yg * yg, axis=-1, keepdims=True) + EPS)
            y_ref[t * tb:(t + 1) * tb, g * gw:(g + 1) * gw] = yg * nw_ref[:, g * gw:(g + 1) * gw]

    last = acs[q - 1]
    cd = jnp.exp(last)
    padded = jnp.concatenate([xdt[t] * jnp.exp(last - acs[t]) for t in range(q)]
                             + [jnp.zeros((LANES - q * tb, D_INNER), F32)], axis=0)
    padded_t = padded.T.astype(BF16)
    zero_rows = jnp.zeros((LANES - q * tb, 2 * D_STATE), F32)
    for b0 in range(0, tb, 2):
        for g in range(SSD_GROUPS):
            rhs = jnp.concatenate(
                [jnp.concatenate([jnp.where(seq == b0, bm[t][:, groups[g]], 0.0),
                                  jnp.where(seq == b0 + 1, bm[t][:, groups[g]], 0.0)], axis=1)
                 for t in range(q)] + [zero_rows], axis=0).astype(BF16)
            st_pair = _dot(padded_t[g * gw:(g + 1) * gw, :], rhs)
            for half, b in enumerate((b0, b0 + 1)):
                for hh in range(HEADS_PER_GROUP):
                    lo = g * gw + hh * SSD_HEAD_DIM
                    cd_h = jnp.broadcast_to(cd[b:b + 1, lo:lo + 1], (SSD_HEAD_DIM, D_STATE))
                    nst_ref[b, lo:lo + SSD_HEAD_DIM, :] = (
                        st_ref[b, lo:lo + SSD_HEAD_DIM, :] * cd_h
                        + st_pair[hh * SSD_HEAD_DIM:(hh + 1) * SSD_HEAD_DIM, half * D_STATE:(half + 1) * D_STATE])


def _ssd_sample(zx, cbuf, st, layer, cw, cb, dtb, alog, dsk, nw, prev_states, *, tb=SAMPLE_TB):
    q = DEC_SEQ
    vec = lambda n: pl.BlockSpec((1, n), lambda i: (0, 0))
    n_layers = st.shape[0]
    in_specs = [
        pl.BlockSpec((tb * q, IN_PAD_S), lambda i: (i, 0)),
        pl.BlockSpec((CONV_W - 1, tb, CONV_DIM), lambda i: (0, i, 0)),
        pl.BlockSpec((None, tb, D_INNER, D_STATE), lambda i: (layer, i, 0, 0)),
        pl.BlockSpec((CONV_W, CONV_DIM), lambda i: (0, 0)),
        vec(CONV_DIM), vec(D_INNER), vec(D_INNER), vec(D_INNER), vec(D_INNER),
    ]
    args = [zx, cbuf, st, cw, cb, dtb, alog, dsk, nw]
    aliases = {}
    if prev_states is not None:
        in_specs.append(pl.BlockSpec(memory_space=pl.ANY))
        args.append(prev_states)
        aliases = {len(args) - 1: 2}
    return pl.pallas_call(
        functools.partial(_ssd_sample_kernel, tb=tb),
        grid=(DEC_BATCH // tb,),
        in_specs=in_specs,
        out_specs=[
            pl.BlockSpec((tb * q, D_INNER), lambda i: (i, 0)),
            pl.BlockSpec((CONV_W - 1, tb, CONV_DIM), lambda i: (0, i, 0)),
            pl.BlockSpec((None, tb, D_INNER, D_STATE), lambda i: (layer, i, 0, 0)),
        ],
        out_shape=[
            jax.ShapeDtypeStruct((ROWS_S, D_INNER), F32),
            jax.ShapeDtypeStruct((CONV_W - 1, DEC_BATCH, CONV_DIM), F32),
            jax.ShapeDtypeStruct((n_layers, DEC_BATCH, D_INNER, D_STATE), F32),
        ],
        input_output_aliases=aliases,
        compiler_params=_params(1),
        name="ssd_sample",
    )(*args)


POOL_LEAD = 32


def _pool_kernel(x_ref, g_ref, buf_ref, pw_ref, ps_ref, o_ref, np_ref, xx_ref, s2_ref, s4_ref, s8_ref,
                 *, tb, tl, nl, pos0):
    l = pl.program_id(1)
    x = x_ref[...]
    u = _rms(x, g_ref[...]).reshape(tb, tl, D_MODEL)
    lead = POOL_LEAD
    b0 = lead - POOL_BUF
    end = lead + tl
    gw = POOL_GW

    @pl.when(l == 0)
    def _():
        xx_ref[:, 0:b0, :] = jnp.zeros((tb, b0, D_MODEL), F32)
        xx_ref[:, b0:lead, :] = buf_ref[...]

    xx_ref[:, lead:end, :] = u
    s2_ref[:, 8:end, :] = xx_ref[:, 8:end, :] + xx_ref[:, 7:end - 1, :]
    s4_ref[:, 16:end, :] = s2_ref[:, 16:end, 2 * gw:] + s2_ref[:, 14:end - 2, 2 * gw:]
    s8_ref[:, 24:end, :] = s4_ref[:, 24:end, gw:] + s4_ref[:, 20:end - 4, gw:]
    sums = (s2_ref[:, lead:end, 0:gw],
            s2_ref[:, lead:end, gw:2 * gw] + s2_ref[:, lead - 2:end - 2, gw:2 * gw],
            s4_ref[:, lead:end, 0:gw] + s4_ref[:, lead - 4:end - 4, 0:gw],
            s8_ref[:, lead:end, :] + s8_ref[:, lead - 8:end - 8, :])

    pos = (pos0 + l * tl + lax.broadcasted_iota(jnp.int32, (1, tl, 1), 1)).astype(F32)
    outs = []
    for gi, w in enumerate(POOL_WINDOWS):
        lo = gi * gw
        cnt = jnp.minimum(pos + 1.0, float(w))
        mix = sums[gi] / cnt - u[:, :, lo:lo + gw]
        outs.append(_dot(mix.reshape(tb * tl, gw).astype(BF16), pw_ref[gi]))
    o_ref[...] = x + jnp.concatenate(outs, axis=-1) * ps_ref[...]

    tail = xx_ref[:, end - POOL_BUF:end, :]
    if nl > 1:
        xx_ref[:, b0:lead, :] = tail

    @pl.when(l == nl - 1)
    def _():
        np_ref[...] = tail


def _pool(x, g, buf_arr, buf_index, pw, ps, *, row_start, nb, seq, tb, tl, pos0, name):
    nl = seq // tl
    r0 = row_start // (tb * tl)
    x_spec = pl.BlockSpec((tb * tl, D_MODEL), lambda b, l: (r0 + b * nl + l, 0))
    return pl.pallas_call(
        functools.partial(_pool_kernel, tb=tb, tl=tl, nl=nl, pos0=pos0),
        grid=(nb // tb, nl),
        in_specs=[
            x_spec,
            pl.BlockSpec((1, D_MODEL), lambda b, l: (0, 0)),
            buf_index(tb),
            pl.BlockSpec((len(POOL_WINDOWS), POOL_GW, POOL_GW), lambda b, l: (0, 0, 0)),
            pl.BlockSpec((1, D_MODEL), lambda b, l: (0, 0)),
        ],
        out_specs=[x_spec, pl.BlockSpec((tb, POOL_BUF, D_MODEL), lambda b, l: (b, 0, 0))],
        out_shape=[jax.ShapeDtypeStruct(x.shape, F32),
                   jax.ShapeDtypeStruct((nb, POOL_BUF, D_MODEL), F32)],
        scratch_shapes=[pltpu.VMEM((tb, POOL_LEAD + tl, D_MODEL), F32),
                        pltpu.VMEM((tb, POOL_LEAD + tl, D_MODEL), F32),
                        pltpu.VMEM((tb, POOL_LEAD + tl, D_MODEL - 2 * POOL_GW), F32),
                        pltpu.VMEM((tb, POOL_LEAD + tl, POOL_GW), F32)],
        input_output_aliases={0: 0},
        compiler_params=_params(2),
        name=name,
    )(x, g, buf_arr, pw, ps)


def _head_kv(ref, h):
    return jnp.concatenate([ref[pl.ds(c * MEM_HEADS + h, N_MEM, stride=MEM_CHUNKS), :]
                            for c in range(MEM_HEAD_DIM // LANES)], axis=1).astype(BF16)


def _xattn_prompt_kernel(x_ref, g_ref, wq_ref, k_ref, v_ref, wo_ref, o_ref):
    x = x_ref[...]
    u = _rms(x, g_ref[...]).astype(BF16)
    q = (_dot(u, wq_ref[...]) * (MEM_HEAD_DIM ** -0.5)).astype(BF16)
    outs = []
    for h in range(MEM_HEADS):
        q_h = q[:, h * MEM_HEAD_DIM:(h + 1) * MEM_HEAD_DIM]
        s = lax.dot_general(q_h, _head_kv(k_ref, h), (((1,), (1,)), ((), ())), preferred_element_type=F32)
        e = jnp.exp(s - jnp.max(s, axis=-1, keepdims=True))
        den = jnp.sum(e, axis=-1, keepdims=True)
        outs.append(_dot(e.astype(BF16), _head_kv(v_ref, h)) / den)
    o = jnp.concatenate(outs, axis=-1)
    o_ref[...] = x + _dot(o.astype(BF16), wo_ref[...])


def _xattn_prompt(x, g, wq, mk, mv, wo, layer, *, tl=1024):
    nl = SEQ // tl
    x_spec = pl.BlockSpec((tl, D_MODEL), lambda b, l: (b * nl + l, 0))
    w_spec = _resident((D_MODEL, D_MODEL), layer)
    kv_spec = pl.BlockSpec((None, N_MEM * MEM_CHUNKS, LANES), lambda b, l: (layer, b, 0))
    return pl.pallas_call(
        _xattn_prompt_kernel,
        grid=(BATCH, nl),
        in_specs=[x_spec, pl.BlockSpec((1, D_MODEL), lambda b, l: (0, 0)), w_spec, kv_spec, kv_spec, w_spec],
        out_specs=x_spec,
        out_shape=jax.ShapeDtypeStruct(x.shape, F32),
        input_output_aliases={0: 0},
        compiler_params=_params(2),
        name="xattn_prompt",
    )(x, g, wq, mk, mv, wo)


def _chunk_rows(t):
    lead = t.shape[:-3]
    t = t.reshape(*lead, N_MEM, MEM_HEADS, MEM_HEAD_DIM // LANES, LANES)
    return jnp.swapaxes(t, -2, -3).reshape(*lead, N_MEM * MEM_CHUNKS, LANES)


def _from_chunk_rows(t):
    t = t.reshape(DEPTH, BATCH, N_MEM, MEM_HEAD_DIM // LANES, MEM_HEADS, LANES)
    return jnp.swapaxes(t, -2, -3).reshape(DEPTH, BATCH, N_MEM, MEM_HEADS, MEM_HEAD_DIM)


def _chunk_to_col(j):
    return ((j % MEM_HEADS) * (MEM_HEAD_DIM // LANES) + j // MEM_HEADS) * LANES


def _attn_sample_kernel(x_ref, g_ref, wq_ref, k_ref, v_ref, wo_ref, o_ref, obuf_ref, *, tb):
    n = DEC_SEQ
    rows = lax.broadcasted_iota(jnp.int32, (MEM_HEADS * n, D_MODEL), 0) // n
    chunk = lax.broadcasted_iota(jnp.int32, (MEM_HEADS * n, D_MODEL), 1) // LANES
    own = rows == chunk % MEM_HEADS

    def gather(ref, b):
        return jnp.concatenate([ref[b, pl.ds(j, N_MEM, stride=MEM_CHUNKS), :] for j in range(MEM_CHUNKS)],
                               axis=1).astype(BF16)

    x = x_ref[...]
    q = _dot(_rms(x, g_ref[...]).astype(BF16), wq_ref[...]) * (MEM_HEAD_DIM ** -0.5)
    hn = MEM_HEADS * n
    scores = []
    for b in range(tb):
        q_b = q[b * n:(b + 1) * n, :]
        q_c = jnp.concatenate([q_b[:, _chunk_to_col(j):_chunk_to_col(j) + LANES] for j in range(MEM_CHUNKS)], axis=1)
        q_blk = jnp.where(own, jnp.concatenate([q_c] * MEM_HEADS, axis=0), 0.0).astype(BF16)
        scores.append(lax.dot_general(q_blk, gather(k_ref, b), (((1,), (1,)), ((), ())),
                                      preferred_element_type=F32))
    s = jnp.concatenate(scores, axis=0)
    e = jnp.exp(s - jnp.max(s, axis=-1, keepdims=True))
    den = jnp.sum(e, axis=-1, keepdims=True)
    e = e.astype(BF16)
    for b in range(tb):
        o = _dot(e[b * hn:(b + 1) * hn, :], gather(v_ref, b)) / den[b * hn:(b + 1) * hn, :]
        for j in range(MEM_CHUNKS):
            h = j % MEM_HEADS
            obuf_ref[b * n:(b + 1) * n, _chunk_to_col(j):_chunk_to_col(j) + LANES] = (
                o[h * n:(h + 1) * n, j * LANES:(j + 1) * LANES])
    o_ref[...] = x + _dot(obuf_ref[...].astype(BF16), wo_ref[...])


def _xattn_sample(x, g, wq, ck, cv, wo, layer, *, tb=8):
    n = DEC_SEQ
    r0 = ROWS_P // (tb * n)
    kv_spec = pl.BlockSpec((None, tb, N_MEM * MEM_CHUNKS, LANES), lambda i: (layer, i, 0, 0))
    row_spec = pl.BlockSpec((tb * n, D_MODEL), lambda i: (r0 + i, 0))
    w_spec = _resident((D_MODEL, D_MODEL), layer)
    return pl.pallas_call(
        functools.partial(_attn_sample_kernel, tb=tb),
        grid=(DEC_BATCH // tb,),
        in_specs=[row_spec, _resident((1, D_MODEL)), w_spec, kv_spec, kv_spec, w_spec],
        out_specs=row_spec,
        out_shape=jax.ShapeDtypeStruct(x.shape, F32),
        scratch_shapes=[pltpu.VMEM((tb * n, D_MODEL), F32)],
        input_output_aliases={0: 0},
        compiler_params=_params(1),
        name="xattn_sample",
    )(x, g, wq, ck, cv, wo)


def _token_major(rows):
    c = rows.shape[-1]
    return jnp.swapaxes(rows.reshape(-1, SAMPLE_TB, DEC_SEQ, c), 1, 2).reshape(-1, c)


def _sequence_major(rows):
    c = rows.shape[-1]
    return jnp.swapaxes(rows.reshape(-1, DEC_SEQ, SAMPLE_TB, c), 1, 2).reshape(-1, c)


def _row(v):
    return v.reshape(1, -1)


def _per_channel(v):
    return jnp.repeat(v, SSD_HEAD_DIM).reshape(1, D_INNER)


def _pad_lanes(v):
    return jnp.pad(v, (0, LANES - v.shape[0])).reshape(1, LANES)


def kernel(x_prompt, x_sample, mem_prompt, cache_mem_k, cache_mem_v, state_ssm, state_conv, state_pool, norm_ffn1, ffn1_w_gate, ffn1_w_up, ffn1_w_down, norm_mix, ssd_in_w, ssd_conv_w, ssd_conv_b, ssd_dt_bias, ssd_a_log, ssd_d, ssd_norm_w, ssd_out_w, pool_w, pool_scale, norm_cross, norm_mem, xa_wq, xa_wk, xa_wv, xa_wo, norm_ffn2, ffn2_w_gate, ffn2_w_up, ffn2_w_down, final_norm):
    bf = lambda w: w.astype(BF16)
    ffn1_w = (bf(ffn1_w_gate), bf(ffn1_w_up), bf(ffn1_w_down))
    ffn2_w = (bf(ffn2_w_gate), bf(ffn2_w_up), bf(ffn2_w_down))
    xa_wq_b, xa_wo_b = bf(xa_wq), bf(xa_wo)
    mk_p, mv_p = _mem_kv(mem_prompt.reshape(BATCH * N_MEM, D_MODEL), norm_mem.reshape(DEPTH, 1, D_MODEL),
                         bf(xa_wk), bf(xa_wv))
    ck = _chunk_rows(cache_mem_k)
    cv = _chunk_rows(cache_mem_v)
    st_s = state_ssm.reshape(-1, DEC_BATCH, D_INNER, D_STATE)
    conv0 = jnp.zeros((BATCH, CONV_W - 1, CONV_DIM), F32)
    ssm0 = jnp.zeros((BATCH, D_INNER, D_STATE), F32)
    pool0 = jnp.zeros((BATCH, POOL_BUF, D_MODEL), F32)

    ssm_p, conv_p, pool_p, conv_s, pool_s = [], [], [], [], []
    ssm_s = None
    for i in range(DEPTH):
        j = i // 2
        ffn1 = (_row(norm_ffn1[i]), *ffn1_w, i)
        if i == 0:
            x = _ffn(x_prompt.reshape(ROWS_P, D_MODEL), *ffn1, src2=x_sample.reshape(ROWS_S, D_MODEL))
        else:
            x = _ffn(x, *ffn1)
        g_mix = _row(norm_mix[i])
        if i % 2 == 0:
            w_in = ssd_in_w[j]
            w_dt = w_in[:, DT_OFF:]
            w_p = bf(jnp.concatenate(
                [w_in[:, :DT_OFF], jnp.pad(w_dt, ((0, 0), (0, IN_PAD_P - DT_OFF - SSD_HEADS)))], axis=1))
            w_s = bf(jnp.concatenate([w_in[:, :DT_OFF], jnp.repeat(w_dt, SSD_HEAD_DIM, axis=1)], axis=1))
            cw, cb = ssd_conv_w[j], _row(ssd_conv_b[j])
            dsk, nw, w_out = _per_channel(ssd_d[j]), _row(ssd_norm_w[j]), bf(ssd_out_w[j])

            x, nc_p, ns_p = _ssd_prompt(x, g_mix, w_p, conv0, ssm0, cw, cb, _pad_lanes(ssd_dt_bias[j]),
                                        _pad_lanes(ssd_a_log[j]), dsk, nw, w_out)
            zx_s = _rms_matmul(_token_major(x[ROWS_P:]), g_mix, w_s, row_start=0, n_rows=ROWS_S, tm=1024,
                               tn=1792, name="in_proj_s")
            y_s, nc_s, ssm_s = _ssd_sample(zx_s, jnp.swapaxes(state_conv[j], 0, 1), st_s, j, cw, cb,
                                           _per_channel(ssd_dt_bias[j]), _per_channel(ssd_a_log[j]), dsk, nw,
                                           ssm_s)
            x = _matmul_residual(x, _sequence_major(y_s), w_out, row_start=ROWS_P, tm=1024, name="out_proj_s")
            nc_s = jnp.swapaxes(nc_s, 0, 1)
            conv_p.append(nc_p)
            conv_s.append(nc_s)
            ssm_p.append(ns_p.reshape(BATCH, SSD_HEADS, SSD_HEAD_DIM, D_STATE))
        else:
            pw, ps = bf(pool_w[j]), _row(pool_scale[j])
            x, np_p = _pool(x, g_mix, pool0, lambda tb: pl.BlockSpec((tb, POOL_BUF, D_MODEL), lambda b, l: (b, 0, 0)),
                            pw, ps, row_start=0, nb=BATCH, seq=SEQ, tb=1, tl=1024, pos0=0, name="pool_p")
            x, np_s = _pool(x, g_mix, state_pool,
                            lambda tb: pl.BlockSpec((None, tb, POOL_BUF, D_MODEL), lambda b, l: (j, b, 0, 0)),
                            pw, ps, row_start=ROWS_P, nb=DEC_BATCH, seq=DEC_SEQ, tb=8, tl=DEC_SEQ,
                            pos0=PAST_LEN, name="pool_s")
            pool_p.append(np_p)
            pool_s.append(np_s)

        g_x = _row(norm_cross[i])
        x = _xattn_prompt(x, g_x, xa_wq_b, mk_p, mv_p, xa_wo_b, i)
        x = _xattn_sample(x, g_x, xa_wq_b, ck, cv, xa_wo_b, i)

        ffn2 = (_row(norm_ffn2[i]), *ffn2_w, i)
        if i < DEPTH - 1:
            x = _ffn(x, *ffn2)

    g_f = _row(final_norm)
    y_prompt = _ffn(x, *ffn2, n_rows=ROWS_P, final_g=g_f).reshape(BATCH, SEQ, D_MODEL)
    y_sample = _ffn(x, *ffn2, src_row_start=ROWS_P, n_rows=ROWS_S, final_g=g_f).reshape(DEC_BATCH, DEC_SEQ, D_MODEL)
    return (y_prompt, y_sample, jnp.stack(ssm_p), jnp.stack(conv_p), jnp.stack(pool_p),
            _from_chunk_rows(mk_p), _from_chunk_rows(mv_p),
            ssm_s.reshape(state_ssm.shape), jnp.stack(conv_s), jnp.stack(pool_s))
```

```python
import functools

import jax
import jax.numpy as jnp
from jax import lax
from jax.experimental import pallas as pl
from jax.experimental.pallas import tpu as pltpu

F32 = jnp.float32
BF16 = jnp.bfloat16

D_MODEL = 1024
BATCH = 8
SEQ = 2048
DEPTH = 4
DEC_BATCH = 128
DEC_SEQ = 8
PAST_LEN = 16384
D_FF = 2816
D_INNER = 2048
SSD_HEAD_DIM = 64
SSD_HEADS = 32
SSD_GROUPS = 4
HEADS_PER_GROUP = 8
D_STATE = 128
CONV_W = 4
GN = SSD_GROUPS * D_STATE
CONV_DIM = D_INNER + 2 * GN
SSD_CHUNK = 128
POOL_WINDOWS = (2, 4, 8, 16)
POOL_GW = 256
POOL_BUF = 15
N_MEM = 256
MEM_HEADS = 4
MEM_HEAD_DIM = 256
EPS = 1e-5

ROWS_P = BATCH * SEQ
ROWS_S = DEC_BATCH * DEC_SEQ
ROWS = ROWS_P + ROWS_S

LANES = 128
XBC_OFF = D_INNER
DT_OFF = D_INNER + CONV_DIM
IN_DIM = DT_OFF + SSD_HEADS
Z_WIDTH = DT_OFF + LANES
VMEM_LIMIT = 48 * 1024 * 1024
MXU_TILE = 256
FFN_CHUNK = MXU_TILE
MEM_CHUNKS = D_MODEL // LANES
SAMPLE_TB = 8


def _params(n_axes, vmem=VMEM_LIMIT):
    return pltpu.CompilerParams(dimension_semantics=("arbitrary",) * n_axes,
                                vmem_limit_bytes=vmem)


def _rms(x, g):
    r = lax.rsqrt(jnp.mean(x * x, axis=-1, keepdims=True) + EPS)
    return x * r * g


def _silu(x):
    return x * (1.0 / (1.0 + jnp.exp(-x)))


def _softplus(x):
    return jnp.maximum(x, 0.0) + jnp.log(1.0 + jnp.exp(-jnp.abs(x)))


def _dot(a, b):
    return jnp.dot(a, b, preferred_element_type=F32)


def _ffn_kernel(*refs, final, n_first):
    x_ref, g_ref, wg_ref, wu_ref, wd_ref = refs[:5]
    o_ref, h_ref = refs[-2:]
    x = x_ref[...]
    if n_first is not None:
        x = jnp.where(pl.program_id(0) < n_first, x, refs[-3][...])
    u = _rms(x, g_ref[...]).astype(BF16)
    for lo in range(0, D_FF, FFN_CHUNK):
        a = _dot(u, wg_ref[:, lo:lo + FFN_CHUNK])
        b = _dot(u, wu_ref[:, lo:lo + FFN_CHUNK])
        h_ref[:, lo:lo + FFN_CHUNK] = (_silu(a) * b).astype(BF16)
    y = x + 0.5 * _dot(h_ref[...], wd_ref[...])
    o_ref[...] = _rms(y, refs[5][...]) if final else y


def _resident(shape, layer=None):
    if layer is None:
        return pl.BlockSpec(shape, lambda *_: (0,) * len(shape), pipeline_mode=pl.Buffered(1))
    return pl.BlockSpec((None,) + tuple(shape), lambda *_: (layer,) + (0,) * len(shape),
                        pipeline_mode=pl.Buffered(1))


def _ffn(src, g, wg, wu, wd, layer, *, src_row_start=0, n_rows=None, src2=None, final_g=None, tm=1024):
    n_rows = src.shape[0] if n_rows is None else n_rows
    s0, n1 = src_row_start // tm, n_rows // tm
    in_specs = [pl.BlockSpec((tm, D_MODEL), lambda i: (s0 + jnp.minimum(i, n1 - 1), 0)), _resident((1, D_MODEL)),
                _resident((D_MODEL, D_FF), layer), _resident((D_MODEL, D_FF), layer),
                _resident((D_FF, D_MODEL), layer)]
    args = [src, g, wg, wu, wd]
    if final_g is not None:
        in_specs.append(_resident((1, D_MODEL)))
        args.append(final_g)
    n2 = 0
    if src2 is not None:
        n2 = src2.shape[0] // tm
        in_specs.append(pl.BlockSpec((tm, D_MODEL), lambda i: (jnp.maximum(i - n1, 0), 0)))
        args.append(src2)
    return pl.pallas_call(
        functools.partial(_ffn_kernel, final=final_g is not None, n_first=n1 if src2 is not None else None),
        grid=(n1 + n2,),
        in_specs=in_specs,
        out_specs=pl.BlockSpec((tm, D_MODEL), lambda i: (i, 0)),
        out_shape=jax.ShapeDtypeStruct(((n1 + n2) * tm, D_MODEL), F32),
        scratch_shapes=[pltpu.VMEM((tm, D_FF), BF16)],
        compiler_params=_params(1),
        name="ffn",
    )(*args)


def _rms_mm_kernel(x_ref, g_ref, w_ref, o_ref, u_ref):
    @pl.when(pl.program_id(1) == 0)
    def _():
        u_ref[...] = _rms(x_ref[...], g_ref[...]).astype(BF16)

    o_ref[...] = _dot(u_ref[...], w_ref[...])


def _rms_matmul(x, g, w, *, row_start, n_rows, tm, tn, name, layer=None, n_cols=None):
    n = w.shape[-1] if n_cols is None else n_cols
    r0 = row_start // tm
    if layer is None:
        w_spec = pl.BlockSpec((D_MODEL, tn), lambda i, j: (0, j))
    else:
        w_spec = pl.BlockSpec((None, D_MODEL, tn), lambda i, j: (layer, 0, j))
    return pl.pallas_call(
        _rms_mm_kernel,
        grid=(n_rows // tm, n // tn),
        in_specs=[
            pl.BlockSpec((tm, D_MODEL), lambda i, j: (r0 + i, 0)),
            pl.BlockSpec((1, D_MODEL), lambda i, j: (0, 0)),
            w_spec,
        ],
        out_specs=pl.BlockSpec((tm, tn), lambda i, j: (i, j)),
        out_shape=jax.ShapeDtypeStruct((n_rows, n), F32),
        scratch_shapes=[pltpu.VMEM((tm, D_MODEL), BF16)],
        compiler_params=_params(2),
        name=name,
    )(x, g, w)


def _mem_kv_kernel(m_ref, g_ref, wk_ref, wv_ref, k_ref, v_ref, *, tm):
    u = _rms(m_ref[...], g_ref[...]).astype(BF16)
    for w_ref, o_ref in ((wk_ref, k_ref), (wv_ref, v_ref)):
        kv = _dot(u, w_ref[...])
        for j in range(MEM_CHUNKS):
            o_ref[pl.ds(j, tm, stride=MEM_CHUNKS), :] = kv[:, _chunk_to_col(j):_chunk_to_col(j) + LANES]


def _mem_kv(mem, g, wk, wv, *, tm=512):
    rows = mem.shape[0]
    w_spec = pl.BlockSpec((None, D_MODEL, D_MODEL), lambda l, r: (l, 0, 0))
    o_spec = pl.BlockSpec((None, tm * MEM_CHUNKS, LANES), lambda l, r: (l, r, 0))
    o_shape = jax.ShapeDtypeStruct((DEPTH, rows * MEM_CHUNKS, LANES), F32)
    return pl.pallas_call(
        functools.partial(_mem_kv_kernel, tm=tm),
        grid=(DEPTH, rows // tm),
        in_specs=[
            pl.BlockSpec((tm, D_MODEL), lambda l, r: (r, 0)),
            pl.BlockSpec((None, 1, D_MODEL), lambda l, r: (l, 0, 0)),
            w_spec, w_spec,
        ],
        out_specs=[o_spec, o_spec],
        out_shape=[o_shape, o_shape],
        compiler_params=_params(2),
        name="mem_kv",
    )(mem, g, wk, wv)


def _mm_res_kernel(x_ref, y_ref, w_ref, o_ref):
    o_ref[...] = x_ref[...] + _dot(y_ref[...].astype(BF16), w_ref[...])


def _matmul_residual(x, y, w, *, row_start, tm, name):
    n_rows, k = y.shape
    r0 = row_start // tm
    return pl.pallas_call(
        _mm_res_kernel,
        grid=(n_rows // tm,),
        in_specs=[
            pl.BlockSpec((tm, D_MODEL), lambda i: (r0 + i, 0)),
            pl.BlockSpec((tm, k), lambda i: (i, 0)),
            pl.BlockSpec((k, D_MODEL), lambda i: (0, 0)),
        ],
        out_specs=pl.BlockSpec((tm, D_MODEL), lambda i: (r0 + i, 0)),
        out_shape=jax.ShapeDtypeStruct(x.shape, F32),
        input_output_aliases={0: 0},
        compiler_params=_params(1),
        name=name,
    )(x, y, w)


CONV_LEAD = 8
CONV_TAIL0 = CONV_LEAD - (CONV_W - 1)


def _scan_chunk(zx, x_res, cw_ref, cb_ref, dtb_ref, alog_ref, dsk_ref, nw_ref, wout_ref,
                xpad_ref, ht_ref, ybuf_ref):
    q = SSD_CHUNK
    lead, tail0 = CONV_LEAD, CONV_TAIL0
    xpad_ref[lead:lead + q, :] = zx[:, XBC_OFF:DT_OFF]
    conv = cw_ref[0:1, :] * xpad_ref[tail0:tail0 + q, :]
    for k in range(1, CONV_W):
        conv = conv + cw_ref[k:k + 1, :] * xpad_ref[tail0 + k:tail0 + k + q, :]
    conv = cb_ref[...] + conv
    tail = xpad_ref[q + tail0:q + lead, :]
    xpad_ref[tail0:lead, :] = tail

    xbc = _silu(conv)
    xs = xbc[:, :D_INNER]
    bm = xbc[:, D_INNER:D_INNER + GN]
    cm = xbc[:, D_INNER + GN:]

    dt = _softplus(zx[:, DT_OFF:DT_OFF + LANES] + dtb_ref[...])
    a = -jnp.exp(alog_ref[...])
    da = dt * a
    rows = lax.broadcasted_iota(jnp.int32, (q, q), 0)
    cols = lax.broadcasted_iota(jnp.int32, (q, q), 1)
    causal = rows >= cols
    tri = jnp.where(causal, 1.0, 0.0).astype(F32)
    acs = jnp.dot(tri, da, precision=lax.Precision.HIGHEST, preferred_element_type=F32)
    acs_t = acs.T
    dt_t = dt.T
    src_t = acs_t - jnp.log(dt_t)
    eacs = jnp.exp(acs)
    last = acs_t[:, q - 1:q]
    w_t = jnp.exp(last - acs_t) * dt_t
    cd_t = jnp.exp(last)
    low_half = lax.broadcasted_iota(jnp.int32, (1, LANES), 1) < SSD_HEAD_DIM

    for g in range(SSD_GROUPS):
        bm_g = bm[:, g * D_STATE:(g + 1) * D_STATE]
        cm_g = cm[:, g * D_STATE:(g + 1) * D_STATE]
        bm_gt = bm_g.T
        cb_g = _dot(cm_g.astype(BF16), bm_gt.astype(BF16))
        for hh in range(0, HEADS_PER_GROUP, 2):
            h0 = g * HEADS_PER_GROUP + hh
            lo = (h0 // 2) * LANES
            x_pair = xs[:, lo:lo + LANES]
            ht_pair = ht_ref[:, lo:lo + LANES]
            x_pair_b = x_pair.astype(BF16)
            rhs = jnp.concatenate([x_pair_b, ht_pair.astype(BF16)], axis=0)
            ys, sts = [], []
            for h in (h0, h0 + 1):
                seg = acs[:, h:h + 1] - src_t[h:h + 1, :]
                m_h = cb_g * jnp.exp(jnp.where(causal, seg, -jnp.inf))
                e_h = eacs[:, h:h + 1] * cm_g
                lhs = jnp.concatenate([m_h.astype(BF16), e_h.astype(BF16)], axis=1)
                ys.append(_dot(lhs, rhs))
                b_h = bm_gt * w_t[h:h + 1, :]
                sts.append(_dot(b_h.astype(BF16), x_pair_b))
            y_pair = jnp.where(low_half, ys[0], ys[1]) + x_pair * dsk_ref[:, lo:lo + LANES]
            st_pair = jnp.where(low_half, sts[0], sts[1])
            cd_row = jnp.where(low_half, cd_t[h0:h0 + 1, :], cd_t[h0 + 1:h0 + 2, :])
            ht_ref[:, lo:lo + LANES] = ht_pair * cd_row + st_pair
            ybuf_ref[:, lo:lo + LANES] = y_pair

    y = ybuf_ref[...] * _silu(zx[:, :D_INNER])
    gw = D_INNER // SSD_GROUPS
    y_norm = []
    for g in range(SSD_GROUPS):
        yg = y[:, g * gw:(g + 1) * gw]
        yg = yg * lax.rsqrt(jnp.mean(yg * yg, axis=-1, keepdims=True) + EPS)
        y_norm.append((yg * nw_ref[:, g * gw:(g + 1) * gw]).astype(BF16))
    return x_res + _dot(jnp.concatenate(y_norm, axis=1), wout_ref[...]), tail


def _ssd_prompt_kernel(x_ref, g_ref, win_ref, wdt_ref, cbuf_ref, st_ref, cw_ref, cb_ref, dtb_ref, alog_ref,
                       dsk_ref, nw_ref, wout_ref, o_ref, nconv_ref, nst_ref,
                       za_ref, zb_ref, xpad_ref, ht_ref, ybuf_ref, *, ns):
    s = pl.program_id(1)
    q = SSD_CHUNK
    xa_ref = x_ref.at[0:2 * q]
    xb_ref = x_ref.at[2 * q:3 * q]

    def project(x_rows, z_ref):
        u = _rms(x_rows, g_ref[...]).astype(BF16)
        z_ref[:, :DT_OFF] = _dot(u, win_ref[:, :DT_OFF])
        z_ref[:, DT_OFF:] = _dot(u, wdt_ref[...])

    scan = functools.partial(_scan_chunk, cw_ref=cw_ref, cb_ref=cb_ref, dtb_ref=dtb_ref, alog_ref=alog_ref,
                             dsk_ref=dsk_ref, nw_ref=nw_ref, wout_ref=wout_ref,
                             xpad_ref=xpad_ref, ht_ref=ht_ref, ybuf_ref=ybuf_ref)

    @pl.when(s == 0)
    def _():
        xpad_ref[CONV_TAIL0:CONV_LEAD, :] = cbuf_ref[...]
        ht_ref[...] = st_ref[...].T
        project(xa_ref[0:q, :], za_ref)

    x1 = xa_ref[q:2 * q, :]
    project(x1, zb_ref)
    o_ref[0:q, :], _ = scan(za_ref, xa_ref[0:q, :])
    project(xb_ref[...], za_ref)
    o_ref[q:2 * q, :], tail = scan(zb_ref, x1)

    @pl.when(s == ns - 1)
    def _():
        nconv_ref[...] = tail
        nst_ref[...] = ht_ref[...].T


def _ssd_prompt(x, g, w_in, w_dt, layer, cbuf, st, cw, cb, dtb, alog, dsk, nw, w_out):
    nb = cbuf.shape[0]
    nc = SEQ // SSD_CHUNK
    ns = nc // 2
    q = SSD_CHUNK
    assert x.shape[0] >= nb * SEQ + q
    vec = lambda n: pl.BlockSpec((1, n), lambda b, c: (0, 0))
    return pl.pallas_call(
        functools.partial(_ssd_prompt_kernel, ns=ns),
        grid=(nb, ns),
        in_specs=[
            pl.BlockSpec((pl.Element(3 * q), pl.Element(D_MODEL)), lambda b, s: ((b * ns + s) * 2 * q, 0)),
            vec(D_MODEL),
            _resident((D_MODEL, IN_DIM), layer),
            _resident((D_MODEL, LANES), layer),
            pl.BlockSpec((None, CONV_W - 1, CONV_DIM), lambda b, c: (b, 0, 0)),
            pl.BlockSpec((None, D_INNER, D_STATE), lambda b, c: (b, 0, 0)),
            pl.BlockSpec((CONV_W, CONV_DIM), lambda b, c: (0, 0)),
            vec(CONV_DIM), vec(LANES), vec(LANES), vec(D_INNER), vec(D_INNER),
            _resident((D_INNER, D_MODEL)),
        ],
        out_specs=[
            pl.BlockSpec((2 * q, D_MODEL), lambda b, s: (b * ns + s, 0)),
            pl.BlockSpec((None, CONV_W - 1, CONV_DIM), lambda b, c: (b, 0, 0)),
            pl.BlockSpec((None, D_INNER, D_STATE), lambda b, c: (b, 0, 0)),
        ],
        out_shape=[
            jax.ShapeDtypeStruct(x.shape, F32),
            jax.ShapeDtypeStruct((nb, CONV_W - 1, CONV_DIM), F32),
            jax.ShapeDtypeStruct((nb, D_INNER, D_STATE), F32),
        ],
        scratch_shapes=[
            pltpu.VMEM((q, Z_WIDTH), F32),
            pltpu.VMEM((q, Z_WIDTH), F32),
            pltpu.VMEM((q + CONV_LEAD, CONV_DIM), F32),
            pltpu.VMEM((D_STATE, D_INNER), F32),
            pltpu.VMEM((q, D_INNER), F32),
        ],
        input_output_aliases={0: 0},
        compiler_params=_params(2),
        name="ssd_prompt",
    )(x, g, w_in, w_dt, cbuf, st, cw, cb, dtb, alog, dsk, nw, w_out)


def _ssd_sample_kernel(zx_ref, dtx_ref, cbuf_ref, st_ref, cw_ref, cb_ref, dtb_ref, alog_ref, dsk_ref, nw_ref,
                       *rest, tb):
    y_ref, nconv_ref, nst_ref = rest[-3:]
    q = DEC_SEQ
    gw = D_INNER // SSD_GROUPS
    groups = [slice(g * D_STATE, (g + 1) * D_STATE) for g in range(SSD_GROUPS)]
    a = -jnp.exp(alog_ref[...])
    seq = lax.broadcasted_iota(jnp.int32, (tb, 1), 0)

    def tok(t, lo, hi):
        return zx_ref[t * tb:(t + 1) * tb, lo:hi]

    raw = [cbuf_ref[k] for k in range(CONV_W - 1)] + [tok(t, XBC_OFF, DT_OFF) for t in range(q)]
    for k in range(CONV_W - 1):
        nconv_ref[k] = raw[q + k]
    xs, bm, cm, acs, xdt = [], [], [], [], []
    for t in range(q):
        conv = cw_ref[0:1, :] * raw[t]
        for k in range(1, CONV_W):
            conv = conv + cw_ref[k:k + 1, :] * raw[t + k]
        xbc = _silu(cb_ref[...] + conv)
        xs.append(xbc[:, :D_INNER])
        bm.append(xbc[:, D_INNER:D_INNER + GN])
        cm.append(xbc[:, D_INNER + GN:])
        dt = _softplus(dtx_ref[t * tb:(t + 1) * tb, :] + dtb_ref[...])
        acs.append(dt * a if t == 0 else acs[-1] + dt * a)
        xdt.append(xs[t] * dt)

    cm_rows = [jnp.concatenate([cm[t][:, sl] for t in range(q)], axis=0).astype(BF16) for sl in groups]
    y_off = {}
    for b in range(tb):
        h0 = st_ref[b].astype(BF16)
        for g in range(SSD_GROUPS):
            r = lax.dot_general(cm_rows[g], h0[g * gw:(g + 1) * gw, :], (((1,), (1,)), ((), ())),
                                preferred_element_type=F32)
            for t in range(q):
                prev = y_off.get((t, g), 0.0)
                y_off[t, g] = jnp.where(seq == b, r[t * tb:(t + 1) * tb, :], prev)

    for t in range(q):
        y = jnp.exp(acs[t]) * jnp.concatenate([y_off[t, g] for g in range(SSD_GROUPS)], axis=1)
        for k in range(t + 1):
            cb_tk = jnp.concatenate(
                [jnp.broadcast_to(jnp.sum(cm[t][:, sl] * bm[k][:, sl], axis=-1, keepdims=True), (tb, gw))
                 for sl in groups], axis=1)
            y = y + cb_tk * jnp.exp(acs[t] - acs[k]) * xdt[k]
        y = (y + xs[t] * dsk_ref[...]) * _silu(tok(t, 0, D_INNER))
        for g in range(SSD_GROUPS):
            yg = y[:, g * gw:(g + 1) * gw]
            yg = yg * lax.rsqrt(jnp.mean(yg * yg, axis=-1, keepdims=True) + EPS)
            y_ref[t * tb:(t + 1) * tb, g * gw:(g + 1) * gw] = yg * nw_ref[:, g * gw:(g + 1) * gw]

    last = acs[q - 1]
    cd = jnp.exp(last)
    padded = jnp.concatenate([xdt[t] * jnp.exp(last - acs[t]) for t in range(q)]
                             + [jnp.zeros((LANES - q * tb, D_INNER), F32)], axis=0)
    padded_t = padded.T.astype(BF16)
    zero_rows = jnp.zeros((LANES - q * tb, 2 * D_STATE), F32)
    for b0 in range(0, tb, 2):
        for g in range(SSD_GROUPS):
            rhs = jnp.concatenate(
                [jnp.concatenate([jnp.where(seq == b0, bm[t][:, groups[g]], 0.0),
                                  jnp.where(seq == b0 + 1, bm[t][:, groups[g]], 0.0)], axis=1)
                 for t in range(q)] + [zero_rows], axis=0).astype(BF16)
            st_pair = _dot(padded_t[g * gw:(g + 1) * gw, :], rhs)
            for half, b in enumerate((b0, b0 + 1)):
                for hh in range(HEADS_PER_GROUP):
                    lo = g * gw + hh * SSD_HEAD_DIM
                    cd_h = jnp.broadcast_to(cd[b:b + 1, lo:lo + 1], (SSD_HEAD_DIM, D_STATE))
                    nst_ref[b, lo:lo + SSD_HEAD_DIM, :] = (
                        st_ref[b, lo:lo + SSD_HEAD_DIM, :] * cd_h
                        + st_pair[hh * SSD_HEAD_DIM:(hh + 1) * SSD_HEAD_DIM, half * D_STATE:(half + 1) * D_STATE])


def _ssd_sample(zx, dtx, cbuf, st, layer, cw, cb, dtb, alog, dsk, nw, prev_states, *, tb=SAMPLE_TB):
    q = DEC_SEQ
    vec = lambda n: pl.BlockSpec((1, n), lambda i: (0, 0))
    n_layers = st.shape[0]
    in_specs = [
        pl.BlockSpec((tb * q, DT_OFF), lambda i: (i, 0)),
        pl.BlockSpec((tb * q, D_INNER), lambda i: (i, 0)),
        pl.BlockSpec((CONV_W - 1, tb, CONV_DIM), lambda i: (0, i, 0)),
        pl.BlockSpec((None, tb, D_INNER, D_STATE), lambda i: (layer, i, 0, 0)),
        pl.BlockSpec((CONV_W, CONV_DIM), lambda i: (0, 0)),
        vec(CONV_DIM), vec(D_INNER), vec(D_INNER), vec(D_INNER), vec(D_INNER),
    ]
    args = [zx, dtx, cbuf, st, cw, cb, dtb, alog, dsk, nw]
    aliases = {}
    if prev_states is not None:
        in_specs.append(pl.BlockSpec(memory_space=pl.ANY))
        args.append(prev_states)
        aliases = {len(args) - 1: 2}
    return pl.pallas_call(
        functools.partial(_ssd_sample_kernel, tb=tb),
        grid=(DEC_BATCH // tb,),
        in_specs=in_specs,
        out_specs=[
            pl.BlockSpec((tb * q, D_INNER), lambda i: (i, 0)),
            pl.BlockSpec((CONV_W - 1, tb, CONV_DIM), lambda i: (0, i, 0)),
            pl.BlockSpec((None, tb, D_INNER, D_STATE), lambda i: (layer, i, 0, 0)),
        ],
        out_shape=[
            jax.ShapeDtypeStruct((ROWS_S, D_INNER), F32),
            jax.ShapeDtypeStruct((CONV_W - 1, DEC_BATCH, CONV_DIM), F32),
            jax.ShapeDtypeStruct((n_layers, DEC_BATCH, D_INNER, D_STATE), F32),
        ],
        input_output_aliases=aliases,
        compiler_params=_params(1),
        name="ssd_sample",
    )(*args)


POOL_LEAD = 32


def _pool_kernel(x_ref, g_ref, buf_ref, pw_ref, ps_ref, o_ref, np_ref, xx_ref, s2_ref, s4_ref, s8_ref,
                 *, tb, tl, nl, pos0):
    l = pl.program_id(1)
    x = x_ref[...]
    u = _rms(x, g_ref[...]).reshape(tb, tl, D_MODEL)
    lead = POOL_LEAD
    b0 = lead - POOL_BUF
    end = lead + tl
    gw = POOL_GW

    @pl.when(l == 0)
    def _():
        xx_ref[:, 0:b0, :] = jnp.zeros((tb, b0, D_MODEL), F32)
        xx_ref[:, b0:lead, :] = buf_ref[...]

    xx_ref[:, lead:end, :] = u
    s2_ref[:, 8:end, :] = xx_ref[:, 8:end, :] + xx_ref[:, 7:end - 1, :]
    s4_ref[:, 16:end, :] = s2_ref[:, 16:end, 2 * gw:] + s2_ref[:, 14:end - 2, 2 * gw:]
    s8_ref[:, 24:end, :] = s4_ref[:, 24:end, gw:] + s4_ref[:, 20:end - 4, gw:]
    sums = (s2_ref[:, lead:end, 0:gw],
            s2_ref[:, lead:end, gw:2 * gw] + s2_ref[:, lead - 2:end - 2, gw:2 * gw],
            s4_ref[:, lead:end, 0:gw] + s4_ref[:, lead - 4:end - 4, 0:gw],
            s8_ref[:, lead:end, :] + s8_ref[:, lead - 8:end - 8, :])

    pos = (pos0 + l * tl + lax.broadcasted_iota(jnp.int32, (1, tl, 1), 1)).astype(F32)
    outs = []
    for gi, w in enumerate(POOL_WINDOWS):
        lo = gi * gw
        cnt = jnp.minimum(pos + 1.0, float(w))
        mix = sums[gi] / cnt - u[:, :, lo:lo + gw]
        outs.append(_dot(mix.reshape(tb * tl, gw).astype(BF16), pw_ref[gi]))
    o_ref[...] = x + jnp.concatenate(outs, axis=-1) * ps_ref[...]

    tail = xx_ref[:, end - POOL_BUF:end, :]
    if nl > 1:
        xx_ref[:, b0:lead, :] = tail

    @pl.when(l == nl - 1)
    def _():
        np_ref[...] = tail


def _pool(x, g, buf_arr, buf_index, pw, ps, *, row_start, nb, seq, tb, tl, pos0, name):
    nl = seq // tl
    r0 = row_start // (tb * tl)
    x_spec = pl.BlockSpec((tb * tl, D_MODEL), lambda b, l: (r0 + b * nl + l, 0))
    return pl.pallas_call(
        functools.partial(_pool_kernel, tb=tb, tl=tl, nl=nl, pos0=pos0),
        grid=(nb // tb, nl),
        in_specs=[
            x_spec,
            pl.BlockSpec((1, D_MODEL), lambda b, l: (0, 0)),
            buf_index(tb),
            pl.BlockSpec((len(POOL_WINDOWS), POOL_GW, POOL_GW), lambda b, l: (0, 0, 0)),
            pl.BlockSpec((1, D_MODEL), lambda b, l: (0, 0)),
        ],
        out_specs=[x_spec, pl.BlockSpec((tb, POOL_BUF, D_MODEL), lambda b, l: (b, 0, 0))],
        out_shape=[jax.ShapeDtypeStruct(x.shape, F32),
                   jax.ShapeDtypeStruct((nb, POOL_BUF, D_MODEL), F32)],
        scratch_shapes=[pltpu.VMEM((tb, POOL_LEAD + tl, D_MODEL), F32),
                        pltpu.VMEM((tb, POOL_LEAD + tl, D_MODEL), F32),
                        pltpu.VMEM((tb, POOL_LEAD + tl, D_MODEL - 2 * POOL_GW), F32),
                        pltpu.VMEM((tb, POOL_LEAD + tl, POOL_GW), F32)],
        input_output_aliases={0: 0},
        compiler_params=_params(2),
        name=name,
    )(x, g, buf_arr, pw, ps)


def _head_kv(ref, h):
    return jnp.concatenate([ref[pl.ds(c * MEM_HEADS + h, N_MEM, stride=MEM_CHUNKS), :]
                            for c in range(MEM_HEAD_DIM // LANES)], axis=1).astype(BF16)


def _xattn_prompt_kernel(x_ref, g_ref, wq_ref, k_ref, v_ref, wo_ref, o_ref):
    x = x_ref[...]
    u = _rms(x, g_ref[...]).astype(BF16)
    q = (_dot(u, wq_ref[...]) * (MEM_HEAD_DIM ** -0.5)).astype(BF16)
    outs = []
    for h in range(MEM_HEADS):
        q_h = q[:, h * MEM_HEAD_DIM:(h + 1) * MEM_HEAD_DIM]
        s = lax.dot_general(q_h, _head_kv(k_ref, h), (((1,), (1,)), ((), ())), preferred_element_type=F32)
        e = jnp.exp(s - jnp.max(s, axis=-1, keepdims=True))
        den = jnp.sum(e, axis=-1, keepdims=True)
        outs.append(_dot(e.astype(BF16), _head_kv(v_ref, h)) / den)
    o = jnp.concatenate(outs, axis=-1)
    o_ref[...] = x + _dot(o.astype(BF16), wo_ref[...])


def _xattn_prompt(x, g, wq, mk, mv, wo, layer, *, tl=1024):
    nl = SEQ // tl
    x_spec = pl.BlockSpec((tl, D_MODEL), lambda b, l: (b * nl + l, 0))
    w_spec = _resident((D_MODEL, D_MODEL), layer)
    kv_spec = pl.BlockSpec((None, N_MEM * MEM_CHUNKS, LANES), lambda b, l: (layer, b, 0))
    return pl.pallas_call(
        _xattn_prompt_kernel,
        grid=(BATCH, nl),
        in_specs=[x_spec, pl.BlockSpec((1, D_MODEL), lambda b, l: (0, 0)), w_spec, kv_spec, kv_spec, w_spec],
        out_specs=x_spec,
        out_shape=jax.ShapeDtypeStruct(x.shape, F32),
        input_output_aliases={0: 0},
        compiler_params=_params(2),
        name="xattn_prompt",
    )(x, g, wq, mk, mv, wo)


def _chunk_rows(t):
    lead = t.shape[:-3]
    t = t.reshape(*lead, N_MEM, MEM_HEADS, MEM_HEAD_DIM // LANES, LANES)
    return jnp.swapaxes(t, -2, -3).reshape(*lead, N_MEM * MEM_CHUNKS, LANES)


def _from_chunk_rows(t):
    t = t.reshape(DEPTH, BATCH, N_MEM, MEM_HEAD_DIM // LANES, MEM_HEADS, LANES)
    return jnp.swapaxes(t, -2, -3).reshape(DEPTH, BATCH, N_MEM, MEM_HEADS, MEM_HEAD_DIM)


def _chunk_to_col(j):
    return ((j % MEM_HEADS) * (MEM_HEAD_DIM // LANES) + j // MEM_HEADS) * LANES


def _attn_sample_kernel(x_ref, g_ref, wq_ref, k_ref, v_ref, wo_ref, o_ref, obuf_ref, *, tb):
    n = DEC_SEQ
    rows = lax.broadcasted_iota(jnp.int32, (MEM_HEADS * n, D_MODEL), 0) // n
    chunk = lax.broadcasted_iota(jnp.int32, (MEM_HEADS * n, D_MODEL), 1) // LANES
    own = rows == chunk % MEM_HEADS

    def gather(ref, b):
        return jnp.concatenate([ref[b, pl.ds(j, N_MEM, stride=MEM_CHUNKS), :] for j in range(MEM_CHUNKS)],
                               axis=1).astype(BF16)

    x = x_ref[...]
    q = _dot(_rms(x, g_ref[...]).astype(BF16), wq_ref[...]) * (MEM_HEAD_DIM ** -0.5)
    hn = MEM_HEADS * n
    scores = []
    for b in range(tb):
        q_b = q[b * n:(b + 1) * n, :]
        q_c = jnp.concatenate([q_b[:, _chunk_to_col(j):_chunk_to_col(j) + LANES] for j in range(MEM_CHUNKS)], axis=1)
        q_blk = jnp.where(own, jnp.concatenate([q_c] * MEM_HEADS, axis=0), 0.0).astype(BF16)
        scores.append(lax.dot_general(q_blk, gather(k_ref, b), (((1,), (1,)), ((), ())),
                                      preferred_element_type=F32))
    s = jnp.concatenate(scores, axis=0)
    e = jnp.exp(s - jnp.max(s, axis=-1, keepdims=True))
    den = jnp.sum(e, axis=-1, keepdims=True)
    e = e.astype(BF16)
    for b in range(tb):
        o = _dot(e[b * hn:(b + 1) * hn, :], gather(v_ref, b)) / den[b * hn:(b + 1) * hn, :]
        for j in range(MEM_CHUNKS):
            h = j % MEM_HEADS
            obuf_ref[b * n:(b + 1) * n, _chunk_to_col(j):_chunk_to_col(j) + LANES] = (
                o[h * n:(h + 1) * n, j * LANES:(j + 1) * LANES])
    o_ref[...] = x + _dot(obuf_ref[...].astype(BF16), wo_ref[...])


def _xattn_sample(x, g, wq, ck, cv, wo, layer, *, tb=8):
    n = DEC_SEQ
    r0 = ROWS_P // (tb * n)
    kv_spec = pl.BlockSpec((None, tb, N_MEM * MEM_CHUNKS, LANES), lambda i: (layer, i, 0, 0))
    row_spec = pl.BlockSpec((tb * n, D_MODEL), lambda i: (r0 + i, 0))
    w_spec = _resident((D_MODEL, D_MODEL), layer)
    return pl.pallas_call(
        functools.partial(_attn_sample_kernel, tb=tb),
        grid=(DEC_BATCH // tb,),
        in_specs=[row_spec, _resident((1, D_MODEL)), w_spec, kv_spec, kv_spec, w_spec],
        out_specs=row_spec,
        out_shape=jax.ShapeDtypeStruct(x.shape, F32),
        scratch_shapes=[pltpu.VMEM((tb * n, D_MODEL), F32)],
        input_output_aliases={0: 0},
        compiler_params=_params(1),
        name="xattn_sample",
    )(x, g, wq, ck, cv, wo)


def _token_major(rows):
    c = rows.shape[-1]
    return jnp.swapaxes(rows.reshape(-1, SAMPLE_TB, DEC_SEQ, c), 1, 2).reshape(-1, c)


def _sequence_major(rows):
    c = rows.shape[-1]
    return jnp.swapaxes(rows.reshape(-1, DEC_SEQ, SAMPLE_TB, c), 1, 2).reshape(-1, c)


def _row(v):
    return v.reshape(1, -1)


def _per_channel(v):
    return jnp.repeat(v, SSD_HEAD_DIM).reshape(1, D_INNER)


def _pad_lanes(v):
    return jnp.pad(v, (0, LANES - v.shape[0])).reshape(1, LANES)


def kernel(x_prompt, x_sample, mem_prompt, cache_mem_k, cache_mem_v, state_ssm, state_conv, state_pool, norm_ffn1, ffn1_w_gate, ffn1_w_up, ffn1_w_down, norm_mix, ssd_in_w, ssd_conv_w, ssd_conv_b, ssd_dt_bias, ssd_a_log, ssd_d, ssd_norm_w, ssd_out_w, pool_w, pool_scale, norm_cross, norm_mem, xa_wq, xa_wk, xa_wv, xa_wo, norm_ffn2, ffn2_w_gate, ffn2_w_up, ffn2_w_down, final_norm):
    bf = lambda w: w.astype(BF16)
    ffn1_w = (bf(ffn1_w_gate), bf(ffn1_w_up), bf(ffn1_w_down))
    ffn2_w = (bf(ffn2_w_gate), bf(ffn2_w_up), bf(ffn2_w_down))
    xa_wq_b, xa_wo_b = bf(xa_wq), bf(xa_wo)
    in_w_b = bf(ssd_in_w)
    in_wdt = ssd_in_w[:, :, DT_OFF:]
    in_wdt_pad = bf(jnp.pad(in_wdt, ((0, 0), (0, 0), (0, LANES - SSD_HEADS))))
    in_wdt_rep = bf(jnp.repeat(in_wdt, SSD_HEAD_DIM, axis=2))
    mk_p, mv_p = _mem_kv(mem_prompt.reshape(BATCH * N_MEM, D_MODEL), norm_mem.reshape(DEPTH, 1, D_MODEL),
                         bf(xa_wk), bf(xa_wv))
    ck = _chunk_rows(cache_mem_k)
    cv = _chunk_rows(cache_mem_v)
    st_s = state_ssm.reshape(-1, DEC_BATCH, D_INNER, D_STATE)
    conv0 = jnp.zeros((BATCH, CONV_W - 1, CONV_DIM), F32)
    ssm0 = jnp.zeros((BATCH, D_INNER, D_STATE), F32)
    pool0 = jnp.zeros((BATCH, POOL_BUF, D_MODEL), F32)

    ssm_p, conv_p, pool_p, conv_s, pool_s = [], [], [], [], []
    ssm_s = None
    for i in range(DEPTH):
        j = i // 2
        ffn1 = (_row(norm_ffn1[i]), *ffn1_w, i)
        if i == 0:
            x = _ffn(x_prompt.reshape(ROWS_P, D_MODEL), *ffn1, src2=x_sample.reshape(ROWS_S, D_MODEL))
        else:
            x = _ffn(x, *ffn1)
        g_mix = _row(norm_mix[i])
        if i % 2 == 0:
            cw, cb = ssd_conv_w[j], _row(ssd_conv_b[j])
            dsk, nw, w_out = _per_channel(ssd_d[j]), _row(ssd_norm_w[j]), bf(ssd_out_w[j])

            x, nc_p, ns_p = _ssd_prompt(x, g_mix, in_w_b, in_wdt_pad, j, conv0, ssm0, cw, cb,
                                        _pad_lanes(ssd_dt_bias[j]), _pad_lanes(ssd_a_log[j]), dsk, nw, w_out)
            x_s = _token_major(x[ROWS_P:])
            zx_s = _rms_matmul(x_s, g_mix, in_w_b, row_start=0, n_rows=ROWS_S, tm=1024, tn=DT_OFF // 4,
                               name="in_proj_s", layer=j, n_cols=DT_OFF)
            dtx_s = _rms_matmul(x_s, g_mix, in_wdt_rep[j], row_start=0, n_rows=ROWS_S, tm=1024, tn=D_INNER,
                                name="dt_proj_s")
            y_s, nc_s, ssm_s = _ssd_sample(zx_s, dtx_s, jnp.swapaxes(state_conv[j], 0, 1), st_s, j, cw, cb,
                                           _per_channel(ssd_dt_bias[j]), _per_channel(ssd_a_log[j]), dsk, nw,
                                           ssm_s)
            x = _matmul_residual(x, _sequence_major(y_s), w_out, row_start=ROWS_P, tm=1024, name="out_proj_s")
            nc_s = jnp.swapaxes(nc_s, 0, 1)
            conv_p.append(nc_p)
            conv_s.append(nc_s)
            ssm_p.append(ns_p.reshape(BATCH, SSD_HEADS, SSD_HEAD_DIM, D_STATE))
        else:
            pw, ps = bf(pool_w[j]), _row(pool_scale[j])
            x, np_p = _pool(x, g_mix, pool0, lambda tb: pl.BlockSpec((tb, POOL_BUF, D_MODEL), lambda b, l: (b, 0, 0)),
                            pw, ps, row_start=0, nb=BATCH, seq=SEQ, tb=1, tl=1024, pos0=0, name="pool_p")
            x, np_s = _pool(x, g_mix, state_pool,
                            lambda tb: pl.BlockSpec((None, tb, POOL_BUF, D_MODEL), lambda b, l: (j, b, 0, 0)),
                            pw, ps, row_start=ROWS_P, nb=DEC_BATCH, seq=DEC_SEQ, tb=8, tl=DEC_SEQ,
                            pos0=PAST_LEN, name="pool_s")
            pool_p.append(np_p)
            pool_s.append(np_s)

        g_x = _row(norm_cross[i])
        x = _xattn_prompt(x, g_x, xa_wq_b, mk_p, mv_p, xa_wo_b, i)
        x = _xattn_sample(x, g_x, xa_wq_b, ck, cv, xa_wo_b, i)

        ffn2 = (_row(norm_ffn2[i]), *ffn2_w, i)
        if i < DEPTH - 1:
            x = _ffn(x, *ffn2)

    g_f = _row(final_norm)
    y_prompt = _ffn(x, *ffn2, n_rows=ROWS_P, final_g=g_f).reshape(BATCH, SEQ, D_MODEL)
    y_sample = _ffn(x, *ffn2, src_row_start=ROWS_P, n_rows=ROWS_S, final_g=g_f).reshape(DEC_BATCH, DEC_SEQ, D_MODEL)
    return (y_prompt, y_sample, jnp.stack(ssm_p), jnp.stack(conv_p), jnp.stack(pool_p),
            _from_chunk_rows(mk_p), _from_chunk_rows(mv_p),
            ssm_s.reshape(state_ssm.shape), jnp.stack(conv_s), jnp.stack(pool_s))
```

```python
import functools

import jax
import jax.numpy as jnp
from jax import lax
from jax.experimental import pallas as pl
from jax.experimental.pallas import tpu as pltpu

F32 = jnp.float32
BF16 = jnp.bfloat16

D_MODEL = 1024
BATCH = 8
SEQ = 2048
DEPTH = 4
DEC_BATCH = 128
DEC_SEQ = 8
PAST_LEN = 16384
D_FF = 2816
D_INNER = 2048
SSD_HEAD_DIM = 64
SSD_HEADS = 32
SSD_GROUPS = 4
HEADS_PER_GROUP = 8
D_STATE = 128
CONV_W = 4
GN = SSD_GROUPS * D_STATE
CONV_DIM = D_INNER + 2 * GN
SSD_CHUNK = 128
POOL_WINDOWS = (2, 4, 8, 16)
POOL_GW = 256
POOL_BUF = 15
N_MEM = 256
MEM_HEADS = 4
MEM_HEAD_DIM = 256
EPS = 1e-5

ROWS_P = BATCH * SEQ
ROWS_S = DEC_BATCH * DEC_SEQ
ROWS = ROWS_P + ROWS_S

LANES = 128
XBC_OFF = D_INNER
DT_OFF = D_INNER + CONV_DIM
IN_DIM = DT_OFF + SSD_HEADS
Z_WIDTH = DT_OFF + LANES
VMEM_LIMIT = 48 * 1024 * 1024
MXU_TILE = 256
FFN_CHUNK = MXU_TILE
MEM_CHUNKS = D_MODEL // LANES
SAMPLE_TB = 8


def _params(n_axes, vmem=VMEM_LIMIT):
    return pltpu.CompilerParams(dimension_semantics=("arbitrary",) * n_axes,
                                vmem_limit_bytes=vmem)


def _rms(x, g):
    r = lax.rsqrt(jnp.mean(x * x, axis=-1, keepdims=True) + EPS)
    return x * r * g


def _silu(x):
    return x * (1.0 / (1.0 + jnp.exp(-x)))


def _softplus(x):
    return jnp.maximum(x, 0.0) + jnp.log(1.0 + jnp.exp(-jnp.abs(x)))


def _dot(a, b):
    return jnp.dot(a, b, preferred_element_type=F32)


def _ffn_kernel(*refs, final, n_first):
    x_ref, g_ref, wg_ref, wu_ref, wd_ref = refs[:5]
    o_ref, h_ref = refs[-2:]
    x = x_ref[...]
    if n_first is not None:
        x = jnp.where(pl.program_id(0) < n_first, x, refs[-3][...])
    u = _rms(x, g_ref[...]).astype(BF16)
    for lo in range(0, D_FF, FFN_CHUNK):
        a = _dot(u, wg_ref[:, lo:lo + FFN_CHUNK])
        b = _dot(u, wu_ref[:, lo:lo + FFN_CHUNK])
        h_ref[:, lo:lo + FFN_CHUNK] = (_silu(a) * b).astype(BF16)
    y = x + 0.5 * _dot(h_ref[...], wd_ref[...])
    o_ref[...] = _rms(y, refs[5][...]) if final else y


def _resident(shape, layer=None):
    if layer is None:
        return pl.BlockSpec(shape, lambda *_: (0,) * len(shape), pipeline_mode=pl.Buffered(1))
    return pl.BlockSpec((None,) + tuple(shape), lambda *_: (layer,) + (0,) * len(shape),
                        pipeline_mode=pl.Buffered(1))


def _ffn(src, g, wg, wu, wd, layer, *, src_row_start=0, n_rows=None, src2=None, final_g=None, tm=1024):
    n_rows = src.shape[0] if n_rows is None else n_rows
    s0, n1 = src_row_start // tm, n_rows // tm
    in_specs = [pl.BlockSpec((tm, D_MODEL), lambda i: (s0 + jnp.minimum(i, n1 - 1), 0)), _resident((1, D_MODEL)),
                _resident((D_MODEL, D_FF), layer), _resident((D_MODEL, D_FF), layer),
                _resident((D_FF, D_MODEL), layer)]
    args = [src, g, wg, wu, wd]
    if final_g is not None:
        in_specs.append(_resident((1, D_MODEL)))
        args.append(final_g)
    n2 = 0
    if src2 is not None:
        n2 = src2.shape[0] // tm
        in_specs.append(pl.BlockSpec((tm, D_MODEL), lambda i: (jnp.maximum(i - n1, 0), 0)))
        args.append(src2)
    return pl.pallas_call(
        functools.partial(_ffn_kernel, final=final_g is not None, n_first=n1 if src2 is not None else None),
        grid=(n1 + n2,),
        in_specs=in_specs,
        out_specs=pl.BlockSpec((tm, D_MODEL), lambda i: (i, 0)),
        out_shape=jax.ShapeDtypeStruct(((n1 + n2) * tm, D_MODEL), F32),
        scratch_shapes=[pltpu.VMEM((tm, D_FF), BF16)],
        compiler_params=_params(1),
        name="ffn",
    )(*args)


def _rms_mm_kernel(x_ref, g_ref, w_ref, o_ref, u_ref):
    @pl.when(pl.program_id(1) == 0)
    def _():
        u_ref[...] = _rms(x_ref[...], g_ref[...]).astype(BF16)

    o_ref[...] = _dot(u_ref[...], w_ref[...])


def _rms_matmul(x, g, w, *, row_start, n_rows, tm, tn, name, layer=None, n_cols=None):
    n = w.shape[-1] if n_cols is None else n_cols
    r0 = row_start // tm
    if layer is None:
        w_spec = pl.BlockSpec((D_MODEL, tn), lambda i, j: (0, j))
    else:
        w_spec = pl.BlockSpec((None, D_MODEL, tn), lambda i, j: (layer, 0, j))
    return pl.pallas_call(
        _rms_mm_kernel,
        grid=(n_rows // tm, n // tn),
        in_specs=[
            pl.BlockSpec((tm, D_MODEL), lambda i, j: (r0 + i, 0)),
            pl.BlockSpec((1, D_MODEL), lambda i, j: (0, 0)),
            w_spec,
        ],
        out_specs=pl.BlockSpec((tm, tn), lambda i, j: (i, j)),
        out_shape=jax.ShapeDtypeStruct((n_rows, n), F32),
        scratch_shapes=[pltpu.VMEM((tm, D_MODEL), BF16)],
        compiler_params=_params(2),
        name=name,
    )(x, g, w)


def _mem_kv_kernel(m_ref, g_ref, wk_ref, wv_ref, k_ref, v_ref, *, tm):
    u = _rms(m_ref[...], g_ref[...]).astype(BF16)
    for w_ref, o_ref in ((wk_ref, k_ref), (wv_ref, v_ref)):
        kv = _dot(u, w_ref[...])
        for j in range(MEM_CHUNKS):
            o_ref[pl.ds(j, tm, stride=MEM_CHUNKS), :] = kv[:, _chunk_to_col(j):_chunk_to_col(j) + LANES]


def _mem_kv(mem, g, wk, wv, *, tm=512):
    rows = mem.shape[0]
    w_spec = pl.BlockSpec((None, D_MODEL, D_MODEL), lambda l, r: (l, 0, 0))
    o_spec = pl.BlockSpec((None, tm * MEM_CHUNKS, LANES), lambda l, r: (l, r, 0))
    o_shape = jax.ShapeDtypeStruct((DEPTH, rows * MEM_CHUNKS, LANES), F32)
    return pl.pallas_call(
        functools.partial(_mem_kv_kernel, tm=tm),
        grid=(DEPTH, rows // tm),
        in_specs=[
            pl.BlockSpec((tm, D_MODEL), lambda l, r: (r, 0)),
            pl.BlockSpec((None, 1, D_MODEL), lambda l, r: (l, 0, 0)),
            w_spec, w_spec,
        ],
        out_specs=[o_spec, o_spec],
        out_shape=[o_shape, o_shape],
        compiler_params=_params(2),
        name="mem_kv",
    )(mem, g, wk, wv)


def _mm_res_kernel(x_ref, y_ref, w_ref, o_ref):
    o_ref[...] = x_ref[...] + _dot(y_ref[...].astype(BF16), w_ref[...])


def _matmul_residual(x, y, w, *, row_start, tm, name):
    n_rows, k = y.shape
    r0 = row_start // tm
    return pl.pallas_call(
        _mm_res_kernel,
        grid=(n_rows // tm,),
        in_specs=[
            pl.BlockSpec((tm, D_MODEL), lambda i: (r0 + i, 0)),
            pl.BlockSpec((tm, k), lambda i: (i, 0)),
            pl.BlockSpec((k, D_MODEL), lambda i: (0, 0)),
        ],
        out_specs=pl.BlockSpec((tm, D_MODEL), lambda i: (r0 + i, 0)),
        out_shape=jax.ShapeDtypeStruct(x.shape, F32),
        input_output_aliases={0: 0},
        compiler_params=_params(1),
        name=name,
    )(x, y, w)


CONV_LEAD = 8
CONV_TAIL0 = CONV_LEAD - (CONV_W - 1)


def _scan_chunk(zx, x_res, cw_ref, cb_ref, dtb_ref, alog_ref, dsk_ref, nw_ref, wout_ref,
                xpad_ref, xbc_ref, ht_ref, ybuf_ref):
    q = SSD_CHUNK
    lead, tail0 = CONV_LEAD, CONV_TAIL0
    xpad_ref[lead:lead + q, :] = zx[:, XBC_OFF:DT_OFF]
    conv = cw_ref[0:1, :] * xpad_ref[tail0:tail0 + q, :]
    for k in range(1, CONV_W):
        conv = conv + cw_ref[k:k + 1, :] * xpad_ref[tail0 + k:tail0 + k + q, :]
    conv = cb_ref[...] + conv
    tail = xpad_ref[q + tail0:q + lead, :]
    xpad_ref[tail0:lead, :] = tail

    xbc_ref[...] = _silu(conv)

    dt = _softplus(zx[:, DT_OFF:DT_OFF + LANES] + dtb_ref[...])
    a = -jnp.exp(alog_ref[...])
    da = dt * a
    rows = lax.broadcasted_iota(jnp.int32, (q, q), 0)
    cols = lax.broadcasted_iota(jnp.int32, (q, q), 1)
    causal = rows >= cols
    tri = jnp.where(causal, 1.0, 0.0).astype(F32)
    acs = jnp.dot(tri, da, precision=lax.Precision.HIGHEST, preferred_element_type=F32)
    acs_t = acs.T
    dt_t = dt.T
    src_t = acs_t - jnp.log(dt_t)
    eacs = jnp.exp(acs)
    last = acs_t[:, q - 1:q]
    w_t = jnp.exp(last - acs_t) * dt_t
    cd_t = jnp.exp(last)
    low_half = lax.broadcasted_iota(jnp.int32, (1, LANES), 1) < SSD_HEAD_DIM

    for g in range(SSD_GROUPS):
        bm_g = xbc_ref[:, D_INNER + g * D_STATE:D_INNER + (g + 1) * D_STATE]
        cm_g = xbc_ref[:, D_INNER + GN + g * D_STATE:D_INNER + GN + (g + 1) * D_STATE]
        bm_gt = bm_g.T
        cb_g = _dot(cm_g.astype(BF16), bm_gt.astype(BF16))
        for hh in range(0, HEADS_PER_GROUP, 2):
            h0 = g * HEADS_PER_GROUP + hh
            lo = (h0 // 2) * LANES
            x_pair = xbc_ref[:, lo:lo + LANES]
            ht_pair = ht_ref[:, lo:lo + LANES]
            x_pair_b = x_pair.astype(BF16)
            rhs = jnp.concatenate([x_pair_b, ht_pair.astype(BF16)], axis=0)
            ys, sts = [], []
            for h in (h0, h0 + 1):
                seg = acs[:, h:h + 1] - src_t[h:h + 1, :]
                m_h = cb_g * jnp.exp(jnp.where(causal, seg, -jnp.inf))
                e_h = eacs[:, h:h + 1] * cm_g
                lhs = jnp.concatenate([m_h.astype(BF16), e_h.astype(BF16)], axis=1)
                ys.append(_dot(lhs, rhs))
                b_h = bm_gt * w_t[h:h + 1, :]
                sts.append(_dot(b_h.astype(BF16), x_pair_b))
            y_pair = jnp.where(low_half, ys[0], ys[1]) + x_pair * dsk_ref[:, lo:lo + LANES]
            st_pair = jnp.where(low_half, sts[0], sts[1])
            cd_row = jnp.where(low_half, cd_t[h0:h0 + 1, :], cd_t[h0 + 1:h0 + 2, :])
            ht_ref[:, lo:lo + LANES] = ht_pair * cd_row + st_pair
            ybuf_ref[:, lo:lo + LANES] = y_pair

    y = ybuf_ref[...] * _silu(zx[:, :D_INNER])
    gw = D_INNER // SSD_GROUPS
    y_norm = []
    for g in range(SSD_GROUPS):
        yg = y[:, g * gw:(g + 1) * gw]
        yg = yg * lax.rsqrt(jnp.mean(yg * yg, axis=-1, keepdims=True) + EPS)
        y_norm.append((yg * nw_ref[:, g * gw:(g + 1) * gw]).astype(BF16))
    return x_res + _dot(jnp.concatenate(y_norm, axis=1), wout_ref[...]), tail


def _ssd_prompt_kernel(x_ref, g_ref, win_ref, wdt_ref, cbuf_ref, st_ref, cw_ref, cb_ref, dtb_ref, alog_ref,
                       dsk_ref, nw_ref, wout_ref, o_ref, nconv_ref, nst_ref,
                       za_ref, zb_ref, xpad_ref, xbc_ref, ht_ref, ybuf_ref, *, ns):
    s = pl.program_id(1)
    q = SSD_CHUNK
    xa_ref = x_ref.at[0:2 * q]
    xb_ref = x_ref.at[2 * q:3 * q]

    def project(x_rows, z_ref):
        u = _rms(x_rows, g_ref[...]).astype(BF16)
        z_ref[:, :DT_OFF] = _dot(u, win_ref[:, :DT_OFF])
        z_ref[:, DT_OFF:] = _dot(u, wdt_ref[...])

    scan = functools.partial(_scan_chunk, cw_ref=cw_ref, cb_ref=cb_ref, dtb_ref=dtb_ref, alog_ref=alog_ref,
                             dsk_ref=dsk_ref, nw_ref=nw_ref, wout_ref=wout_ref,
                             xpad_ref=xpad_ref, xbc_ref=xbc_ref, ht_ref=ht_ref, ybuf_ref=ybuf_ref)

    @pl.when(s == 0)
    def _():
        xpad_ref[CONV_TAIL0:CONV_LEAD, :] = cbuf_ref[...]
        ht_ref[...] = st_ref[...].T
        project(xa_ref[0:q, :], za_ref)

    x1 = xa_ref[q:2 * q, :]
    project(x1, zb_ref)
    o_ref[0:q, :], _ = scan(za_ref, xa_ref[0:q, :])
    project(xb_ref[...], za_ref)
    o_ref[q:2 * q, :], tail = scan(zb_ref, x1)

    @pl.when(s == ns - 1)
    def _():
        nconv_ref[...] = tail
        nst_ref[...] = ht_ref[...].T


def _ssd_prompt(x, g, w_in, w_dt, layer, cbuf, st, cw, cb, dtb, alog, dsk, nw, w_out):
    nb = cbuf.shape[0]
    nc = SEQ // SSD_CHUNK
    ns = nc // 2
    q = SSD_CHUNK
    assert x.shape[0] >= nb * SEQ + q
    vec = lambda n: pl.BlockSpec((1, n), lambda b, c: (0, 0))
    return pl.pallas_call(
        functools.partial(_ssd_prompt_kernel, ns=ns),
        grid=(nb, ns),
        in_specs=[
            pl.BlockSpec((pl.Element(3 * q), pl.Element(D_MODEL)), lambda b, s: ((b * ns + s) * 2 * q, 0)),
            vec(D_MODEL),
            _resident((D_MODEL, IN_DIM), layer),
            _resident((D_MODEL, LANES), layer),
            pl.BlockSpec((None, CONV_W - 1, CONV_DIM), lambda b, c: (b, 0, 0)),
            pl.BlockSpec((None, D_INNER, D_STATE), lambda b, c: (b, 0, 0)),
            pl.BlockSpec((CONV_W, CONV_DIM), lambda b, c: (0, 0)),
            vec(CONV_DIM), vec(LANES), vec(LANES), vec(D_INNER), vec(D_INNER),
            _resident((D_INNER, D_MODEL)),
        ],
        out_specs=[
            pl.BlockSpec((2 * q, D_MODEL), lambda b, s: (b * ns + s, 0)),
            pl.BlockSpec((None, CONV_W - 1, CONV_DIM), lambda b, c: (b, 0, 0)),
            pl.BlockSpec((None, D_INNER, D_STATE), lambda b, c: (b, 0, 0)),
        ],
        out_shape=[
            jax.ShapeDtypeStruct(x.shape, F32),
            jax.ShapeDtypeStruct((nb, CONV_W - 1, CONV_DIM), F32),
            jax.ShapeDtypeStruct((nb, D_INNER, D_STATE), F32),
        ],
        scratch_shapes=[
            pltpu.VMEM((q, Z_WIDTH), F32),
            pltpu.VMEM((q, Z_WIDTH), F32),
            pltpu.VMEM((q + CONV_LEAD, CONV_DIM), F32),
            pltpu.VMEM((q, CONV_DIM), F32),
            pltpu.VMEM((D_STATE, D_INNER), F32),
            pltpu.VMEM((q, D_INNER), F32),
        ],
        input_output_aliases={0: 0},
        compiler_params=_params(2),
        name="ssd_prompt",
    )(x, g, w_in, w_dt, cbuf, st, cw, cb, dtb, alog, dsk, nw, w_out)


def _ssd_sample_kernel(zx_ref, dtx_ref, cbuf_ref, st_ref, cw_ref, cb_ref, dtb_ref, alog_ref, dsk_ref, nw_ref,
                       *rest, tb):
    y_ref, nconv_ref, nst_ref = rest[-3:]
    q = DEC_SEQ
    gw = D_INNER // SSD_GROUPS
    groups = [slice(g * D_STATE, (g + 1) * D_STATE) for g in range(SSD_GROUPS)]
    a = -jnp.exp(alog_ref[...])
    seq = lax.broadcasted_iota(jnp.int32, (tb, 1), 0)

    def tok(t, lo, hi):
        return zx_ref[t * tb:(t + 1) * tb, lo:hi]

    raw = [cbuf_ref[k] for k in range(CONV_W - 1)] + [tok(t, XBC_OFF, DT_OFF) for t in range(q)]
    for k in range(CONV_W - 1):
        nconv_ref[k] = raw[q + k]
    xs, bm, cm, acs, xdt = [], [], [], [], []
    for t in range(q):
        conv = cw_ref[0:1, :] * raw[t]
        for k in range(1, CONV_W):
            conv = conv + cw_ref[k:k + 1, :] * raw[t + k]
        xbc = _silu(cb_ref[...] + conv)
        xs.append(xbc[:, :D_INNER])
        bm.append(xbc[:, D_INNER:D_INNER + GN])
        cm.append(xbc[:, D_INNER + GN:])
        dt = _softplus(dtx_ref[t * tb:(t + 1) * tb, :] + dtb_ref[...])
        acs.append(dt * a if t == 0 else acs[-1] + dt * a)
        xdt.append(xs[t] * dt)

    cm_rows = [jnp.concatenate([cm[t][:, sl] for t in range(q)], axis=0).astype(BF16) for sl in groups]
    y_off = {}
    for b in range(tb):
        h0 = st_ref[b].astype(BF16)
        for g in range(SSD_GROUPS):
            r = lax.dot_general(cm_rows[g], h0[g * gw:(g + 1) * gw, :], (((1,), (1,)), ((), ())),
                                preferred_element_type=F32)
            for t in range(q):
                prev = y_off.get((t, g), 0.0)
                y_off[t, g] = jnp.where(seq == b, r[t * tb:(t + 1) * tb, :], prev)

    for t in range(q):
        y = jnp.exp(acs[t]) * jnp.concatenate([y_off[t, g] for g in range(SSD_GROUPS)], axis=1)
        for k in range(t + 1):
            cb_tk = jnp.concatenate(
                [jnp.broadcast_to(jnp.sum(cm[t][:, sl] * bm[k][:, sl], axis=-1, keepdims=True), (tb, gw))
                 for sl in groups], axis=1)
            y = y + cb_tk * jnp.exp(acs[t] - acs[k]) * xdt[k]
        y = (y + xs[t] * dsk_ref[...]) * _silu(tok(t, 0, D_INNER))
        for g in range(SSD_GROUPS):
            yg = y[:, g * gw:(g + 1) * gw]
            yg = yg * lax.rsqrt(jnp.mean(yg * yg, axis=-1, keepdims=True) + EPS)
            y_ref[t * tb:(t + 1) * tb, g * gw:(g + 1) * gw] = yg * nw_ref[:, g * gw:(g + 1) * gw]

    last = acs[q - 1]
    cd = jnp.exp(last)
    padded = jnp.concatenate([xdt[t] * jnp.exp(last - acs[t]) for t in range(q)]
                             + [jnp.zeros((LANES - q * tb, D_INNER), F32)], axis=0)
    padded_t = padded.T.astype(BF16)
    zero_rows = jnp.zeros((LANES - q * tb, 2 * D_STATE), F32)
    for b0 in range(0, tb, 2):
        for g in range(SSD_GROUPS):
            rhs = jnp.concatenate(
                [jnp.concatenate([jnp.where(seq == b0, bm[t][:, groups[g]], 0.0),
                                  jnp.where(seq == b0 + 1, bm[t][:, groups[g]], 0.0)], axis=1)
                 for t in range(q)] + [zero_rows], axis=0).astype(BF16)
            st_pair = _dot(padded_t[g * gw:(g + 1) * gw, :], rhs)
            for half, b in enumerate((b0, b0 + 1)):
                for hh in range(HEADS_PER_GROUP):
                    lo = g * gw + hh * SSD_HEAD_DIM
                    cd_h = jnp.broadcast_to(cd[b:b + 1, lo:lo + 1], (SSD_HEAD_DIM, D_STATE))
                    nst_ref[b, lo:lo + SSD_HEAD_DIM, :] = (
                        st_ref[b, lo:lo + SSD_HEAD_DIM, :] * cd_h
                        + st_pair[hh * SSD_HEAD_DIM:(hh + 1) * SSD_HEAD_DIM, half * D_STATE:(half + 1) * D_STATE])


def _ssd_sample(zx, dtx, cbuf, st, layer, cw, cb, dtb, alog, dsk, nw, prev_states, *, tb=SAMPLE_TB):
    q = DEC_SEQ
    vec = lambda n: pl.BlockSpec((1, n), lambda i: (0, 0))
    n_layers = st.shape[0]
    in_specs = [
        pl.BlockSpec((tb * q, DT_OFF), lambda i: (i, 0)),
        pl.BlockSpec((tb * q, D_INNER), lambda i: (i, 0)),
        pl.BlockSpec((CONV_W - 1, tb, CONV_DIM), lambda i: (0, i, 0)),
        pl.BlockSpec((None, tb, D_INNER, D_STATE), lambda i: (layer, i, 0, 0)),
        pl.BlockSpec((CONV_W, CONV_DIM), lambda i: (0, 0)),
        vec(CONV_DIM), vec(D_INNER), vec(D_INNER), vec(D_INNER), vec(D_INNER),
    ]
    args = [zx, dtx, cbuf, st, cw, cb, dtb, alog, dsk, nw]
    aliases = {}
    if prev_states is not None:
        in_specs.append(pl.BlockSpec(memory_space=pl.ANY))
        args.append(prev_states)
        aliases = {len(args) - 1: 2}
    return pl.pallas_call(
        functools.partial(_ssd_sample_kernel, tb=tb),
        grid=(DEC_BATCH // tb,),
        in_specs=in_specs,
        out_specs=[
            pl.BlockSpec((tb * q, D_INNER), lambda i: (i, 0)),
            pl.BlockSpec((CONV_W - 1, tb, CONV_DIM), lambda i: (0, i, 0)),
            pl.BlockSpec((None, tb, D_INNER, D_STATE), lambda i: (layer, i, 0, 0)),
        ],
        out_shape=[
            jax.ShapeDtypeStruct((ROWS_S, D_INNER), F32),
            jax.ShapeDtypeStruct((CONV_W - 1, DEC_BATCH, CONV_DIM), F32),
            jax.ShapeDtypeStruct((n_layers, DEC_BATCH, D_INNER, D_STATE), F32),
        ],
        input_output_aliases=aliases,
        compiler_params=_params(1),
        name="ssd_sample",
    )(*args)


POOL_LEAD = 32


def _pool_kernel(x_ref, g_ref, buf_ref, pw_ref, ps_ref, o_ref, np_ref, xx_ref, s2_ref, s4_ref, s8_ref,
                 *, tb, tl, nl, pos0):
    l = pl.program_id(1)
    x = x_ref[...]
    u = _rms(x, g_ref[...]).reshape(tb, tl, D_MODEL)
    lead = POOL_LEAD
    b0 = lead - POOL_BUF
    end = lead + tl
    gw = POOL_GW

    @pl.when(l == 0)
    def _():
        xx_ref[:, 0:b0, :] = jnp.zeros((tb, b0, D_MODEL), F32)
        xx_ref[:, b0:lead, :] = buf_ref[...]

    xx_ref[:, lead:end, :] = u
    s2_ref[:, 8:end, :] = xx_ref[:, 8:end, :] + xx_ref[:, 7:end - 1, :]
    s4_ref[:, 16:end, :] = s2_ref[:, 16:end, 2 * gw:] + s2_ref[:, 14:end - 2, 2 * gw:]
    s8_ref[:, 24:end, :] = s4_ref[:, 24:end, gw:] + s4_ref[:, 20:end - 4, gw:]
    sums = (s2_ref[:, lead:end, 0:gw],
            s2_ref[:, lead:end, gw:2 * gw] + s2_ref[:, lead - 2:end - 2, gw:2 * gw],
            s4_ref[:, lead:end, 0:gw] + s4_ref[:, lead - 4:end - 4, 0:gw],
            s8_ref[:, lead:end, :] + s8_ref[:, lead - 8:end - 8, :])

    pos = (pos0 + l * tl + lax.broadcasted_iota(jnp.int32, (1, tl, 1), 1)).astype(F32)
    outs = []
    for gi, w in enumerate(POOL_WINDOWS):
        lo = gi * gw
        cnt = jnp.minimum(pos + 1.0, float(w))
        mix = sums[gi] / cnt - u[:, :, lo:lo + gw]
        outs.append(_dot(mix.reshape(tb * tl, gw).astype(BF16), pw_ref[gi]))
    o_ref[...] = x + jnp.concatenate(outs, axis=-1) * ps_ref[...]

    tail = xx_ref[:, end - POOL_BUF:end, :]
    if nl > 1:
        xx_ref[:, b0:lead, :] = tail

    @pl.when(l == nl - 1)
    def _():
        np_ref[...] = tail


def _pool(x, g, buf_arr, buf_index, pw, ps, *, row_start, nb, seq, tb, tl, pos0, name):
    nl = seq // tl
    r0 = row_start // (tb * tl)
    x_spec = pl.BlockSpec((tb * tl, D_MODEL), lambda b, l: (r0 + b * nl + l, 0))
    return pl.pallas_call(
        functools.partial(_pool_kernel, tb=tb, tl=tl, nl=nl, pos0=pos0),
        grid=(nb // tb, nl),
        in_specs=[
            x_spec,
            pl.BlockSpec((1, D_MODEL), lambda b, l: (0, 0)),
            buf_index(tb),
            pl.BlockSpec((len(POOL_WINDOWS), POOL_GW, POOL_GW), lambda b, l: (0, 0, 0)),
            pl.BlockSpec((1, D_MODEL), lambda b, l: (0, 0)),
        ],
        out_specs=[x_spec, pl.BlockSpec((tb, POOL_BUF, D_MODEL), lambda b, l: (b, 0, 0))],
        out_shape=[jax.ShapeDtypeStruct(x.shape, F32),
                   jax.ShapeDtypeStruct((nb, POOL_BUF, D_MODEL), F32)],
        scratch_shapes=[pltpu.VMEM((tb, POOL_LEAD + tl, D_MODEL), F32),
                        pltpu.VMEM((tb, POOL_LEAD + tl, D_MODEL), F32),
                        pltpu.VMEM((tb, POOL_LEAD + tl, D_MODEL - 2 * POOL_GW), F32),
                        pltpu.VMEM((tb, POOL_LEAD + tl, POOL_GW), F32)],
        input_output_aliases={0: 0},
        compiler_params=_params(2),
        name=name,
    )(x, g, buf_arr, pw, ps)


def _head_kv(ref, h):
    return jnp.concatenate([ref[pl.ds(c * MEM_HEADS + h, N_MEM, stride=MEM_CHUNKS), :]
                            for c in range(MEM_HEAD_DIM // LANES)], axis=1).astype(BF16)


def _xattn_prompt_kernel(x_ref, g_ref, wq_ref, k_ref, v_ref, wo_ref, o_ref):
    x = x_ref[...]
    u = _rms(x, g_ref[...]).astype(BF16)
    q = (_dot(u, wq_ref[...]) * (MEM_HEAD_DIM ** -0.5)).astype(BF16)
    outs = []
    for h in range(MEM_HEADS):
        q_h = q[:, h * MEM_HEAD_DIM:(h + 1) * MEM_HEAD_DIM]
        s = lax.dot_general(q_h, _head_kv(k_ref, h), (((1,), (1,)), ((), ())), preferred_element_type=F32)
        e = jnp.exp(s - jnp.max(s, axis=-1, keepdims=True))
        den = jnp.sum(e, axis=-1, keepdims=True)
        outs.append(_dot(e.astype(BF16), _head_kv(v_ref, h)) / den)
    o = jnp.concatenate(outs, axis=-1)
    o_ref[...] = x + _dot(o.astype(BF16), wo_ref[...])


def _xattn_prompt(x, g, wq, mk, mv, wo, layer, *, tl=1024):
    nl = SEQ // tl
    x_spec = pl.BlockSpec((tl, D_MODEL), lambda b, l: (b * nl + l, 0))
    w_spec = _resident((D_MODEL, D_MODEL), layer)
    kv_spec = pl.BlockSpec((None, N_MEM * MEM_CHUNKS, LANES), lambda b, l: (layer, b, 0))
    return pl.pallas_call(
        _xattn_prompt_kernel,
        grid=(BATCH, nl),
        in_specs=[x_spec, pl.BlockSpec((1, D_MODEL), lambda b, l: (0, 0)), w_spec, kv_spec, kv_spec, w_spec],
        out_specs=x_spec,
        out_shape=jax.ShapeDtypeStruct(x.shape, F32),
        input_output_aliases={0: 0},
        compiler_params=_params(2),
        name="xattn_prompt",
    )(x, g, wq, mk, mv, wo)


def _chunk_rows(t):
    lead = t.shape[:-3]
    t = t.reshape(*lead, N_MEM, MEM_HEADS, MEM_HEAD_DIM // LANES, LANES)
    return jnp.swapaxes(t, -2, -3).reshape(*lead, N_MEM * MEM_CHUNKS, LANES)


def _from_chunk_rows(t):
    t = t.reshape(DEPTH, BATCH, N_MEM, MEM_HEAD_DIM // LANES, MEM_HEADS, LANES)
    return jnp.swapaxes(t, -2, -3).reshape(DEPTH, BATCH, N_MEM, MEM_HEADS, MEM_HEAD_DIM)


def _chunk_to_col(j):
    return ((j % MEM_HEADS) * (MEM_HEAD_DIM // LANES) + j // MEM_HEADS) * LANES


def _attn_sample_kernel(x_ref, g_ref, wq_ref, k_ref, v_ref, wo_ref, o_ref, obuf_ref, *, tb):
    n = DEC_SEQ
    rows = lax.broadcasted_iota(jnp.int32, (MEM_HEADS * n, D_MODEL), 0) // n
    chunk = lax.broadcasted_iota(jnp.int32, (MEM_HEADS * n, D_MODEL), 1) // LANES
    own = rows == chunk % MEM_HEADS

    def gather(ref, b):
        return jnp.concatenate([ref[b, pl.ds(j, N_MEM, stride=MEM_CHUNKS), :] for j in range(MEM_CHUNKS)],
                               axis=1).astype(BF16)

    x = x_ref[...]
    q = _dot(_rms(x, g_ref[...]).astype(BF16), wq_ref[...]) * (MEM_HEAD_DIM ** -0.5)
    hn = MEM_HEADS * n
    scores = []
    for b in range(tb):
        q_b = q[b * n:(b + 1) * n, :]
        q_c = jnp.concatenate([q_b[:, _chunk_to_col(j):_chunk_to_col(j) + LANES] for j in range(MEM_CHUNKS)], axis=1)
        q_blk = jnp.where(own, jnp.concatenate([q_c] * MEM_HEADS, axis=0), 0.0).astype(BF16)
        scores.append(lax.dot_general(q_blk, gather(k_ref, b), (((1,), (1,)), ((), ())),
                                      preferred_element_type=F32))
    s = jnp.concatenate(scores, axis=0)
    e = jnp.exp(s - jnp.max(s, axis=-1, keepdims=True))
    den = jnp.sum(e, axis=-1, keepdims=True)
    e = e.astype(BF16)
    for b in range(tb):
        o = _dot(e[b * hn:(b + 1) * hn, :], gather(v_ref, b)) / den[b * hn:(b + 1) * hn, :]
        for j in range(MEM_CHUNKS):
            h = j % MEM_HEADS
            obuf_ref[b * n:(b + 1) * n, _chunk_to_col(j):_chunk_to_col(j) + LANES] = (
                o[h * n:(h + 1) * n, j * LANES:(j + 1) * LANES])
    o_ref[...] = x + _dot(obuf_ref[...].astype(BF16), wo_ref[...])


def _xattn_sample(x, g, wq, ck, cv, wo, layer, *, tb=8):
    n = DEC_SEQ
    r0 = ROWS_P // (tb * n)
    kv_spec = pl.BlockSpec((None, tb, N_MEM * MEM_CHUNKS, LANES), lambda i: (layer, i, 0, 0))
    row_spec = pl.BlockSpec((tb * n, D_MODEL), lambda i: (r0 + i, 0))
    w_spec = _resident((D_MODEL, D_MODEL), layer)
    return pl.pallas_call(
        functools.partial(_attn_sample_kernel, tb=tb),
        grid=(DEC_BATCH // tb,),
        in_specs=[row_spec, _resident((1, D_MODEL)), w_spec, kv_spec, kv_spec, w_spec],
        out_specs=row_spec,
        out_shape=jax.ShapeDtypeStruct(x.shape, F32),
        scratch_shapes=[pltpu.VMEM((tb * n, D_MODEL), F32)],
        input_output_aliases={0: 0},
        compiler_params=_params(1),
        name="xattn_sample",
    )(x, g, wq, ck, cv, wo)


def _token_major(rows):
    c = rows.shape[-1]
    return jnp.swapaxes(rows.reshape(-1, SAMPLE_TB, DEC_SEQ, c), 1, 2).reshape(-1, c)


def _sequence_major(rows):
    c = rows.shape[-1]
    return jnp.swapaxes(rows.reshape(-1, DEC_SEQ, SAMPLE_TB, c), 1, 2).reshape(-1, c)


def _row(v):
    return v.reshape(1, -1)


def _per_channel(v):
    return jnp.repeat(v, SSD_HEAD_DIM).reshape(1, D_INNER)


def _pad_lanes(v):
    return jnp.pad(v, (0, LANES - v.shape[0])).reshape(1, LANES)


def kernel(x_prompt, x_sample, mem_prompt, cache_mem_k, cache_mem_v, state_ssm, state_conv, state_pool, norm_ffn1, ffn1_w_gate, ffn1_w_up, ffn1_w_down, norm_mix, ssd_in_w, ssd_conv_w, ssd_conv_b, ssd_dt_bias, ssd_a_log, ssd_d, ssd_norm_w, ssd_out_w, pool_w, pool_scale, norm_cross, norm_mem, xa_wq, xa_wk, xa_wv, xa_wo, norm_ffn2, ffn2_w_gate, ffn2_w_up, ffn2_w_down, final_norm):
    bf = lambda w: w.astype(BF16)
    ffn1_w = (bf(ffn1_w_gate), bf(ffn1_w_up), bf(ffn1_w_down))
    ffn2_w = (bf(ffn2_w_gate), bf(ffn2_w_up), bf(ffn2_w_down))
    xa_wq_b, xa_wo_b = bf(xa_wq), bf(xa_wo)
    in_w_b = bf(ssd_in_w)
    in_wdt = ssd_in_w[:, :, DT_OFF:]
    in_wdt_pad = bf(jnp.pad(in_wdt, ((0, 0), (0, 0), (0, LANES - SSD_HEADS))))
    in_wdt_rep = bf(jnp.repeat(in_wdt, SSD_HEAD_DIM, axis=2))
    mk_p, mv_p = _mem_kv(mem_prompt.reshape(BATCH * N_MEM, D_MODEL), norm_mem.reshape(DEPTH, 1, D_MODEL),
                         bf(xa_wk), bf(xa_wv))
    ck = _chunk_rows(cache_mem_k)
    cv = _chunk_rows(cache_mem_v)
    st_s = state_ssm.reshape(-1, DEC_BATCH, D_INNER, D_STATE)
    conv0 = jnp.zeros((BATCH, CONV_W - 1, CONV_DIM), F32)
    ssm0 = jnp.zeros((BATCH, D_INNER, D_STATE), F32)
    pool0 = jnp.zeros((BATCH, POOL_BUF, D_MODEL), F32)

    ssm_p, conv_p, pool_p, conv_s, pool_s = [], [], [], [], []
    ssm_s = None
    for i in range(DEPTH):
        j = i // 2
        ffn1 = (_row(norm_ffn1[i]), *ffn1_w, i)
        if i == 0:
            x = _ffn(x_prompt.reshape(ROWS_P, D_MODEL), *ffn1, src2=x_sample.reshape(ROWS_S, D_MODEL))
        else:
            x = _ffn(x, *ffn1)
        g_mix = _row(norm_mix[i])
        if i % 2 == 0:
            cw, cb = ssd_conv_w[j], _row(ssd_conv_b[j])
            dsk, nw, w_out = _per_channel(ssd_d[j]), _row(ssd_norm_w[j]), bf(ssd_out_w[j])

            x, nc_p, ns_p = _ssd_prompt(x, g_mix, in_w_b, in_wdt_pad, j, conv0, ssm0, cw, cb,
                                        _pad_lanes(ssd_dt_bias[j]), _pad_lanes(ssd_a_log[j]), dsk, nw, w_out)
            x_s = _token_major(x[ROWS_P:])
            zx_s = _rms_matmul(x_s, g_mix, in_w_b, row_start=0, n_rows=ROWS_S, tm=1024, tn=DT_OFF // 4,
                               name="in_proj_s", layer=j, n_cols=DT_OFF)
            dtx_s = _rms_matmul(x_s, g_mix, in_wdt_rep[j], row_start=0, n_rows=ROWS_S, tm=1024, tn=D_INNER,
                                name="dt_proj_s")
            y_s, nc_s, ssm_s = _ssd_sample(zx_s, dtx_s, jnp.swapaxes(state_conv[j], 0, 1), st_s, j, cw, cb,
                                           _per_channel(ssd_dt_bias[j]), _per_channel(ssd_a_log[j]), dsk, nw,
                                           ssm_s)
            x = _matmul_residual(x, _sequence_major(y_s), w_out, row_start=ROWS_P, tm=1024, name="out_proj_s")
            nc_s = jnp.swapaxes(nc_s, 0, 1)
            conv_p.append(nc_p)
            conv_s.append(nc_s)
            ssm_p.append(ns_p.reshape(BATCH, SSD_HEADS, SSD_HEAD_DIM, D_STATE))
        else:
            pw, ps = bf(pool_w[j]), _row(pool_scale[j])
            x, np_p = _pool(x, g_mix, pool0, lambda tb: pl.BlockSpec((tb, POOL_BUF, D_MODEL), lambda b, l: (b, 0, 0)),
                            pw, ps, row_start=0, nb=BATCH, seq=SEQ, tb=1, tl=1024, pos0=0, name="pool_p")
            x, np_s = _pool(x, g_mix, state_pool,
                            lambda tb: pl.BlockSpec((None, tb, POOL_BUF, D_MODEL), lambda b, l: (j, b, 0, 0)),
                            pw, ps, row_start=ROWS_P, nb=DEC_BATCH, seq=DEC_SEQ, tb=8, tl=DEC_SEQ,
                            pos0=PAST_LEN, name="pool_s")
            pool_p.append(np_p)
            pool_s.append(np_s)

        g_x = _row(norm_cross[i])
        x = _xattn_prompt(x, g_x, xa_wq_b, mk_p, mv_p, xa_wo_b, i)
        x = _xattn_sample(x, g_x, xa_wq_b, ck, cv, xa_wo_b, i)

        ffn2 = (_row(norm_ffn2[i]), *ffn2_w, i)
        if i < DEPTH - 1:
            x = _ffn(x, *ffn2)

    g_f = _row(final_norm)
    y_prompt = _ffn(x, *ffn2, n_rows=ROWS_P, final_g=g_f).reshape(BATCH, SEQ, D_MODEL)
    y_sample = _ffn(x, *ffn2, src_row_start=ROWS_P, n_rows=ROWS_S, final_g=g_f).reshape(DEC_BATCH, DEC_SEQ, D_MODEL)
    return (y_prompt, y_sample, jnp.stack(ssm_p), jnp.stack(conv_p), jnp.stack(pool_p),
            _from_chunk_rows(mk_p), _from_chunk_rows(mv_p),
            ssm_s.reshape(state_ssm.shape), jnp.stack(conv_s), jnp.stack(pool_s))
```

```python
import functools

import jax
import jax.numpy as jnp
from jax import lax
from jax.experimental import pallas as pl
from jax.experimental.pallas import tpu as pltpu

F32 = jnp.float32
BF16 = jnp.bfloat16

D_MODEL = 1024
BATCH = 8
SEQ = 2048
DEPTH = 4
DEC_BATCH = 128
DEC_SEQ = 8
PAST_LEN = 16384
D_FF = 2816
D_INNER = 2048
SSD_HEAD_DIM = 64
SSD_HEADS = 32
SSD_GROUPS = 4
HEADS_PER_GROUP = 8
D_STATE = 128
CONV_W = 4
GN = SSD_GROUPS * D_STATE
CONV_DIM = D_INNER + 2 * GN
SSD_CHUNK = 128
POOL_WINDOWS = (2, 4, 8, 16)
POOL_GW = 256
POOL_BUF = 15
N_MEM = 256
MEM_HEADS = 4
MEM_HEAD_DIM = 256
EPS = 1e-5

ROWS_P = BATCH * SEQ
ROWS_S = DEC_BATCH * DEC_SEQ
ROWS = ROWS_P + ROWS_S

LANES = 128
XBC_OFF = D_INNER
DT_OFF = D_INNER + CONV_DIM
IN_DIM = DT_OFF + SSD_HEADS
Z_WIDTH = DT_OFF + LANES
VMEM_LIMIT = 48 * 1024 * 1024
MXU_TILE = 256
FFN_CHUNK = MXU_TILE
MEM_CHUNKS = D_MODEL // LANES
SAMPLE_TB = 8


def _params(n_axes, vmem=VMEM_LIMIT):
    return pltpu.CompilerParams(dimension_semantics=("arbitrary",) * n_axes,
                                vmem_limit_bytes=vmem)


def _rms(x, g):
    r = lax.rsqrt(jnp.mean(x * x, axis=-1, keepdims=True) + EPS)
    return x * r * g


def _silu(x):
    return x * (1.0 / (1.0 + jnp.exp(-x)))


def _softplus(x):
    return jnp.maximum(x, 0.0) + jnp.log(1.0 + jnp.exp(-jnp.abs(x)))


def _dot(a, b):
    return jnp.dot(a, b, preferred_element_type=F32)


def _ffn_kernel(*refs, final, n_first):
    x_ref, g_ref, wg_ref, wu_ref, wd_ref = refs[:5]
    o_ref, h_ref = refs[-2:]
    x = x_ref[...]
    if n_first is not None:
        x = jnp.where(pl.program_id(0) < n_first, x, refs[-3][...])
    u = _rms(x, g_ref[...]).astype(BF16)
    for lo in range(0, D_FF, FFN_CHUNK):
        a = _dot(u, wg_ref[:, lo:lo + FFN_CHUNK])
        b = _dot(u, wu_ref[:, lo:lo + FFN_CHUNK])
        h_ref[:, lo:lo + FFN_CHUNK] = (_silu(a) * b).astype(BF16)
    y = x + 0.5 * _dot(h_ref[...], wd_ref[...])
    o_ref[...] = _rms(y, refs[5][...]) if final else y


def _resident(shape, layer=None):
    if layer is None:
        return pl.BlockSpec(shape, lambda *_: (0,) * len(shape), pipeline_mode=pl.Buffered(1))
    return pl.BlockSpec((None,) + tuple(shape), lambda *_: (layer,) + (0,) * len(shape),
                        pipeline_mode=pl.Buffered(1))


def _ffn(src, g, wg, wu, wd, layer, *, src_row_start=0, n_rows=None, src2=None, final_g=None, tm=1024):
    n_rows = src.shape[0] if n_rows is None else n_rows
    s0, n1 = src_row_start // tm, n_rows // tm
    in_specs = [pl.BlockSpec((tm, D_MODEL), lambda i: (s0 + jnp.minimum(i, n1 - 1), 0)), _resident((1, D_MODEL)),
                _resident((D_MODEL, D_FF), layer), _resident((D_MODEL, D_FF), layer),
                _resident((D_FF, D_MODEL), layer)]
    args = [src, g, wg, wu, wd]
    if final_g is not None:
        in_specs.append(_resident((1, D_MODEL)))
        args.append(final_g)
    n2 = 0
    if src2 is not None:
        n2 = src2.shape[0] // tm
        in_specs.append(pl.BlockSpec((tm, D_MODEL), lambda i: (jnp.maximum(i - n1, 0), 0)))
        args.append(src2)
    return pl.pallas_call(
        functools.partial(_ffn_kernel, final=final_g is not None, n_first=n1 if src2 is not None else None),
        grid=(n1 + n2,),
        in_specs=in_specs,
        out_specs=pl.BlockSpec((tm, D_MODEL), lambda i: (i, 0)),
        out_shape=jax.ShapeDtypeStruct(((n1 + n2) * tm, D_MODEL), F32),
        scratch_shapes=[pltpu.VMEM((tm, D_FF), BF16)],
        compiler_params=_params(1),
        name="ffn",
    )(*args)


def _rms_mm_kernel(x_ref, g_ref, w_ref, o_ref, u_ref):
    @pl.when(pl.program_id(1) == 0)
    def _():
        u_ref[...] = _rms(x_ref[...], g_ref[...]).astype(BF16)

    o_ref[...] = _dot(u_ref[...], w_ref[...])


def _rms_matmul(x, g, w, *, row_start, n_rows, tm, tn, name, layer=None, n_cols=None):
    n = w.shape[-1] if n_cols is None else n_cols
    r0 = row_start // tm
    if layer is None:
        w_spec = pl.BlockSpec((D_MODEL, tn), lambda i, j: (0, j))
    else:
        w_spec = pl.BlockSpec((None, D_MODEL, tn), lambda i, j: (layer, 0, j))
    return pl.pallas_call(
        _rms_mm_kernel,
        grid=(n_rows // tm, n // tn),
        in_specs=[
            pl.BlockSpec((tm, D_MODEL), lambda i, j: (r0 + i, 0)),
            pl.BlockSpec((1, D_MODEL), lambda i, j: (0, 0)),
            w_spec,
        ],
        out_specs=pl.BlockSpec((tm, tn), lambda i, j: (i, j)),
        out_shape=jax.ShapeDtypeStruct((n_rows, n), F32),
        scratch_shapes=[pltpu.VMEM((tm, D_MODEL), BF16)],
        compiler_params=_params(2),
        name=name,
    )(x, g, w)


def _mem_kv_kernel(m_ref, g_ref, wk_ref, wv_ref, k_ref, v_ref, *, tm):
    u = _rms(m_ref[...], g_ref[...]).astype(BF16)
    for w_ref, o_ref in ((wk_ref, k_ref), (wv_ref, v_ref)):
        kv = _dot(u, w_ref[...])
        for j in range(MEM_CHUNKS):
            o_ref[pl.ds(j, tm, stride=MEM_CHUNKS), :] = kv[:, _chunk_to_col(j):_chunk_to_col(j) + LANES]


def _mem_kv(mem, g, wk, wv, *, tm=512):
    rows = mem.shape[0]
    w_spec = pl.BlockSpec((None, D_MODEL, D_MODEL), lambda l, r: (l, 0, 0))
    o_spec = pl.BlockSpec((None, tm * MEM_CHUNKS, LANES), lambda l, r: (l, r, 0))
    o_shape = jax.ShapeDtypeStruct((DEPTH, rows * MEM_CHUNKS, LANES), F32)
    return pl.pallas_call(
        functools.partial(_mem_kv_kernel, tm=tm),
        grid=(DEPTH, rows // tm),
        in_specs=[
            pl.BlockSpec((tm, D_MODEL), lambda l, r: (r, 0)),
            pl.BlockSpec((None, 1, D_MODEL), lambda l, r: (l, 0, 0)),
            w_spec, w_spec,
        ],
        out_specs=[o_spec, o_spec],
        out_shape=[o_shape, o_shape],
        compiler_params=_params(2),
        name="mem_kv",
    )(mem, g, wk, wv)


def _mm_res_kernel(x_ref, y_ref, w_ref, o_ref):
    o_ref[...] = x_ref[...] + _dot(y_ref[...].astype(BF16), w_ref[...])


def _matmul_residual(x, y, w, *, row_start, tm, name):
    n_rows, k = y.shape
    r0 = row_start // tm
    return pl.pallas_call(
        _mm_res_kernel,
        grid=(n_rows // tm,),
        in_specs=[
            pl.BlockSpec((tm, D_MODEL), lambda i: (r0 + i, 0)),
            pl.BlockSpec((tm, k), lambda i: (i, 0)),
            pl.BlockSpec((k, D_MODEL), lambda i: (0, 0)),
        ],
        out_specs=pl.BlockSpec((tm, D_MODEL), lambda i: (r0 + i, 0)),
        out_shape=jax.ShapeDtypeStruct(x.shape, F32),
        input_output_aliases={0: 0},
        compiler_params=_params(1),
        name=name,
    )(x, y, w)


CONV_LEAD = 8
CONV_TAIL0 = CONV_LEAD - (CONV_W - 1)


def _scan_chunk(zx, x_res, cw_ref, cb_ref, dtb_ref, alog_ref, dsk_ref, nw_ref, wout_ref,
                xpad_ref, xbc_ref, tab_ref, ht_ref, ybuf_ref):
    q = SSD_CHUNK
    lead, tail0 = CONV_LEAD, CONV_TAIL0
    xpad_ref[lead:lead + q, :] = zx[:, XBC_OFF:DT_OFF]
    conv = cw_ref[0:1, :] * xpad_ref[tail0:tail0 + q, :]
    for k in range(1, CONV_W):
        conv = conv + cw_ref[k:k + 1, :] * xpad_ref[tail0 + k:tail0 + k + q, :]
    conv = cb_ref[...] + conv
    tail = xpad_ref[q + tail0:q + lead, :]
    xpad_ref[tail0:lead, :] = tail

    xbc_ref[...] = _silu(conv)

    dt = _softplus(zx[:, DT_OFF:DT_OFF + LANES] + dtb_ref[...])
    a = -jnp.exp(alog_ref[...])
    da = dt * a
    rows = lax.broadcasted_iota(jnp.int32, (q, q), 0)
    cols = lax.broadcasted_iota(jnp.int32, (q, q), 1)
    causal = rows >= cols
    tri = jnp.where(causal, 1.0, 0.0).astype(F32)
    acs = jnp.dot(tri, da, precision=lax.Precision.HIGHEST, preferred_element_type=F32)
    acs_t = acs.T
    dt_t = dt.T
    src_t = acs_t - jnp.log(dt_t)
    eacs = jnp.exp(acs)
    last = acs_t[:, q - 1:q]
    w_t = jnp.exp(last - acs_t) * dt_t
    cd_t = jnp.exp(last)
    tab_ref[0], tab_ref[1], tab_ref[2], tab_ref[3] = acs, eacs, src_t, w_t
    low_half = lax.broadcasted_iota(jnp.int32, (1, LANES), 1) < SSD_HEAD_DIM

    for g in range(SSD_GROUPS):
        bm_g = xbc_ref[:, D_INNER + g * D_STATE:D_INNER + (g + 1) * D_STATE]
        cm_g = xbc_ref[:, D_INNER + GN + g * D_STATE:D_INNER + GN + (g + 1) * D_STATE]
        bm_gt = bm_g.T
        cb_g = _dot(cm_g.astype(BF16), bm_gt.astype(BF16))
        for hh in range(0, HEADS_PER_GROUP, 2):
            h0 = g * HEADS_PER_GROUP + hh
            lo = (h0 // 2) * LANES
            x_pair = xbc_ref[:, lo:lo + LANES]
            ht_pair = ht_ref[:, lo:lo + LANES]
            x_pair_b = x_pair.astype(BF16)
            rhs = jnp.concatenate([x_pair_b, ht_pair.astype(BF16)], axis=0)
            ys, sts = [], []
            for h in (h0, h0 + 1):
                seg = tab_ref[0, :, h:h + 1] - tab_ref[2, h:h + 1, :]
                m_h = cb_g * jnp.exp(jnp.where(causal, seg, -jnp.inf))
                e_h = tab_ref[1, :, h:h + 1] * cm_g
                lhs = jnp.concatenate([m_h.astype(BF16), e_h.astype(BF16)], axis=1)
                ys.append(_dot(lhs, rhs))
                b_h = bm_gt * tab_ref[3, h:h + 1, :]
                sts.append(_dot(b_h.astype(BF16), x_pair_b))
            y_pair = jnp.where(low_half, ys[0], ys[1]) + x_pair * dsk_ref[:, lo:lo + LANES]
            st_pair = jnp.where(low_half, sts[0], sts[1])
            cd_row = jnp.where(low_half, cd_t[h0:h0 + 1, :], cd_t[h0 + 1:h0 + 2, :])
            ht_ref[:, lo:lo + LANES] = ht_pair * cd_row + st_pair
            ybuf_ref[:, lo:lo + LANES] = y_pair

    y = ybuf_ref[...] * _silu(zx[:, :D_INNER])
    gw = D_INNER // SSD_GROUPS
    y_norm = []
    for g in range(SSD_GROUPS):
        yg = y[:, g * gw:(g + 1) * gw]
        yg = yg * lax.rsqrt(jnp.mean(yg * yg, axis=-1, keepdims=True) + EPS)
        y_norm.append((yg * nw_ref[:, g * gw:(g + 1) * gw]).astype(BF16))
    return x_res + _dot(jnp.concatenate(y_norm, axis=1), wout_ref[...]), tail


def _ssd_prompt_kernel(x_ref, g_ref, win_ref, wdt_ref, cbuf_ref, st_ref, cw_ref, cb_ref, dtb_ref, alog_ref,
                       dsk_ref, nw_ref, wout_ref, o_ref, nconv_ref, nst_ref,
                       za_ref, zb_ref, xpad_ref, xbc_ref, tab_ref, ht_ref, ybuf_ref, *, ns):
    s = pl.program_id(1)
    q = SSD_CHUNK
    xa_ref = x_ref.at[0:2 * q]
    xb_ref = x_ref.at[2 * q:3 * q]

    def project(x_rows, z_ref):
        u = _rms(x_rows, g_ref[...]).astype(BF16)
        z_ref[:, :DT_OFF] = _dot(u, win_ref[:, :DT_OFF])
        z_ref[:, DT_OFF:] = _dot(u, wdt_ref[...])

    scan = functools.partial(_scan_chunk, cw_ref=cw_ref, cb_ref=cb_ref, dtb_ref=dtb_ref, alog_ref=alog_ref,
                             dsk_ref=dsk_ref, nw_ref=nw_ref, wout_ref=wout_ref,
                             xpad_ref=xpad_ref, xbc_ref=xbc_ref, tab_ref=tab_ref, ht_ref=ht_ref,
                             ybuf_ref=ybuf_ref)

    @pl.when(s == 0)
    def _():
        xpad_ref[CONV_TAIL0:CONV_LEAD, :] = cbuf_ref[...]
        ht_ref[...] = st_ref[...].T
        project(xa_ref[0:q, :], za_ref)

    x1 = xa_ref[q:2 * q, :]
    project(x1, zb_ref)
    o_ref[0:q, :], _ = scan(za_ref, xa_ref[0:q, :])
    project(xb_ref[...], za_ref)
    o_ref[q:2 * q, :], tail = scan(zb_ref, x1)

    @pl.when(s == ns - 1)
    def _():
        nconv_ref[...] = tail
        nst_ref[...] = ht_ref[...].T


def _ssd_prompt(x, g, w_in, w_dt, layer, cbuf, st, cw, cb, dtb, alog, dsk, nw, w_out):
    nb = cbuf.shape[0]
    nc = SEQ // SSD_CHUNK
    ns = nc // 2
    q = SSD_CHUNK
    assert x.shape[0] >= nb * SEQ + q
    vec = lambda n: pl.BlockSpec((1, n), lambda b, c: (0, 0))
    return pl.pallas_call(
        functools.partial(_ssd_prompt_kernel, ns=ns),
        grid=(nb, ns),
        in_specs=[
            pl.BlockSpec((pl.Element(3 * q), pl.Element(D_MODEL)), lambda b, s: ((b * ns + s) * 2 * q, 0)),
            vec(D_MODEL),
            _resident((D_MODEL, IN_DIM), layer),
            _resident((D_MODEL, LANES), layer),
            pl.BlockSpec((None, CONV_W - 1, CONV_DIM), lambda b, c: (b, 0, 0)),
            pl.BlockSpec((None, D_INNER, D_STATE), lambda b, c: (b, 0, 0)),
            pl.BlockSpec((CONV_W, CONV_DIM), lambda b, c: (0, 0)),
            vec(CONV_DIM), vec(LANES), vec(LANES), vec(D_INNER), vec(D_INNER),
            _resident((D_INNER, D_MODEL)),
        ],
        out_specs=[
            pl.BlockSpec((2 * q, D_MODEL), lambda b, s: (b * ns + s, 0)),
            pl.BlockSpec((None, CONV_W - 1, CONV_DIM), lambda b, c: (b, 0, 0)),
            pl.BlockSpec((None, D_INNER, D_STATE), lambda b, c: (b, 0, 0)),
        ],
        out_shape=[
            jax.ShapeDtypeStruct(x.shape, F32),
            jax.ShapeDtypeStruct((nb, CONV_W - 1, CONV_DIM), F32),
            jax.ShapeDtypeStruct((nb, D_INNER, D_STATE), F32),
        ],
        scratch_shapes=[
            pltpu.VMEM((q, Z_WIDTH), F32),
            pltpu.VMEM((q, Z_WIDTH), F32),
            pltpu.VMEM((q + CONV_LEAD, CONV_DIM), F32),
            pltpu.VMEM((q, CONV_DIM), F32),
            pltpu.VMEM((4, q, LANES), F32),
            pltpu.VMEM((D_STATE, D_INNER), F32),
            pltpu.VMEM((q, D_INNER), F32),
        ],
        input_output_aliases={0: 0},
        compiler_params=_params(2),
        name="ssd_prompt",
    )(x, g, w_in, w_dt, cbuf, st, cw, cb, dtb, alog, dsk, nw, w_out)


def _ssd_sample_kernel(zx_ref, dtx_ref, cbuf_ref, st_ref, cw_ref, cb_ref, dtb_ref, alog_ref, dsk_ref, nw_ref,
                       *rest, tb):
    y_ref, nconv_ref, nst_ref = rest[-3:]
    q = DEC_SEQ
    gw = D_INNER // SSD_GROUPS
    groups = [slice(g * D_STATE, (g + 1) * D_STATE) for g in range(SSD_GROUPS)]
    a = -jnp.exp(alog_ref[...])
    seq = lax.broadcasted_iota(jnp.int32, (tb, 1), 0)

    def tok(t, lo, hi):
        return zx_ref[t * tb:(t + 1) * tb, lo:hi]

    raw = [cbuf_ref[k] for k in range(CONV_W - 1)] + [tok(t, XBC_OFF, DT_OFF) for t in range(q)]
    for k in range(CONV_W - 1):
        nconv_ref[k] = raw[q + k]
    xs, bm, cm, acs, xdt = [], [], [], [], []
    for t in range(q):
        conv = cw_ref[0:1, :] * raw[t]
        for k in range(1, CONV_W):
            conv = conv + cw_ref[k:k + 1, :] * raw[t + k]
        xbc = _silu(cb_ref[...] + conv)
        xs.append(xbc[:, :D_INNER])
        bm.append(xbc[:, D_INNER:D_INNER + GN])
        cm.append(xbc[:, D_INNER + GN:])
        dt = _softplus(dtx_ref[t * tb:(t + 1) * tb, :] + dtb_ref[...])
        acs.append(dt * a if t == 0 else acs[-1] + dt * a)
        xdt.append(xs[t] * dt)

    cm_rows = [jnp.concatenate([cm[t][:, sl] for t in range(q)], axis=0).astype(BF16) for sl in groups]
    y_off = {}
    for b in range(tb):
        h0 = st_ref[b].astype(BF16)
        for g in range(SSD_GROUPS):
            r = lax.dot_general(cm_rows[g], h0[g * gw:(g + 1) * gw, :], (((1,), (1,)), ((), ())),
                                preferred_element_type=F32)
            for t in range(q):
                prev = y_off.get((t, g), 0.0)
                y_off[t, g] = jnp.where(seq == b, r[t * tb:(t + 1) * tb, :], prev)

    for t in range(q):
        y = jnp.exp(acs[t]) * jnp.concatenate([y_off[t, g] for g in range(SSD_GROUPS)], axis=1)
        for k in range(t + 1):
            cb_tk = jnp.concatenate(
                [jnp.broadcast_to(jnp.sum(cm[t][:, sl] * bm[k][:, sl], axis=-1, keepdims=True), (tb, gw))
                 for sl in groups], axis=1)
            y = y + cb_tk * jnp.exp(acs[t] - acs[k]) * xdt[k]
        y = (y + xs[t] * dsk_ref[...]) * _silu(tok(t, 0, D_INNER))
        for g in range(SSD_GROUPS):
            yg = y[:, g * gw:(g + 1) * gw]
            yg = yg * lax.rsqrt(jnp.mean(yg * yg, axis=-1, keepdims=True) + EPS)
            y_ref[t * tb:(t + 1) * tb, g * gw:(g + 1) * gw] = yg * nw_ref[:, g * gw:(g + 1) * gw]

    last = acs[q - 1]
    cd = jnp.exp(last)
    padded = jnp.concatenate([xdt[t] * jnp.exp(last - acs[t]) for t in range(q)]
                             + [jnp.zeros((LANES - q * tb, D_INNER), F32)], axis=0)
    padded_t = padded.T.astype(BF16)
    zero_rows = jnp.zeros((LANES - q * tb, 2 * D_STATE), F32)
    for b0 in range(0, tb, 2):
        for g in range(SSD_GROUPS):
            rhs = jnp.concatenate(
                [jnp.concatenate([jnp.where(seq == b0, bm[t][:, groups[g]], 0.0),
                                  jnp.where(seq == b0 + 1, bm[t][:, groups[g]], 0.0)], axis=1)
                 for t in range(q)] + [zero_rows], axis=0).astype(BF16)
            st_pair = _dot(padded_t[g * gw:(g + 1) * gw, :], rhs)
            for half, b in enumerate((b0, b0 + 1)):
                for hh in range(HEADS_PER_GROUP):
                    lo = g * gw + hh * SSD_HEAD_DIM
                    cd_h = jnp.broadcast_to(cd[b:b + 1, lo:lo + 1], (SSD_HEAD_DIM, D_STATE))
                    nst_ref[b, lo:lo + SSD_HEAD_DIM, :] = (
                        st_ref[b, lo:lo + SSD_HEAD_DIM, :] * cd_h
                        + st_pair[hh * SSD_HEAD_DIM:(hh + 1) * SSD_HEAD_DIM, half * D_STATE:(half + 1) * D_STATE])


def _ssd_sample(zx, dtx, cbuf, st, layer, cw, cb, dtb, alog, dsk, nw, prev_states, *, tb=SAMPLE_TB):
    q = DEC_SEQ
    vec = lambda n: pl.BlockSpec((1, n), lambda i: (0, 0))
    n_layers = st.shape[0]
    in_specs = [
        pl.BlockSpec((tb * q, DT_OFF), lambda i: (i, 0)),
        pl.BlockSpec((tb * q, D_INNER), lambda i: (i, 0)),
        pl.BlockSpec((CONV_W - 1, tb, CONV_DIM), lambda i: (0, i, 0)),
        pl.BlockSpec((None, tb, D_INNER, D_STATE), lambda i: (layer, i, 0, 0)),
        pl.BlockSpec((CONV_W, CONV_DIM), lambda i: (0, 0)),
        vec(CONV_DIM), vec(D_INNER), vec(D_INNER), vec(D_INNER), vec(D_INNER),
    ]
    args = [zx, dtx, cbuf, st, cw, cb, dtb, alog, dsk, nw]
    aliases = {}
    if prev_states is not None:
        in_specs.append(pl.BlockSpec(memory_space=pl.ANY))
        args.append(prev_states)
        aliases = {len(args) - 1: 2}
    return pl.pallas_call(
        functools.partial(_ssd_sample_kernel, tb=tb),
        grid=(DEC_BATCH // tb,),
        in_specs=in_specs,
        out_specs=[
            pl.BlockSpec((tb * q, D_INNER), lambda i: (i, 0)),
            pl.BlockSpec((CONV_W - 1, tb, CONV_DIM), lambda i: (0, i, 0)),
            pl.BlockSpec((None, tb, D_INNER, D_STATE), lambda i: (layer, i, 0, 0)),
        ],
        out_shape=[
            jax.ShapeDtypeStruct((ROWS_S, D_INNER), F32),
            jax.ShapeDtypeStruct((CONV_W - 1, DEC_BATCH, CONV_DIM), F32),
            jax.ShapeDtypeStruct((n_layers, DEC_BATCH, D_INNER, D_STATE), F32),
        ],
        input_output_aliases=aliases,
        compiler_params=_params(1),
        name="ssd_sample",
    )(*args)


POOL_LEAD = 32


def _pool_kernel(x_ref, g_ref, buf_ref, pw_ref, ps_ref, o_ref, np_ref, xx_ref, s2_ref, s4_ref, s8_ref,
                 *, tb, tl, nl, pos0):
    l = pl.program_id(1)
    x = x_ref[...]
    u = _rms(x, g_ref[...]).reshape(tb, tl, D_MODEL)
    lead = POOL_LEAD
    b0 = lead - POOL_BUF
    end = lead + tl
    gw = POOL_GW

    @pl.when(l == 0)
    def _():
        xx_ref[:, 0:b0, :] = jnp.zeros((tb, b0, D_MODEL), F32)
        xx_ref[:, b0:lead, :] = buf_ref[...]

    xx_ref[:, lead:end, :] = u
    s2_ref[:, 8:end, :] = xx_ref[:, 8:end, :] + xx_ref[:, 7:end - 1, :]
    s4_ref[:, 16:end, :] = s2_ref[:, 16:end, 2 * gw:] + s2_ref[:, 14:end - 2, 2 * gw:]
    s8_ref[:, 24:end, :] = s4_ref[:, 24:end, gw:] + s4_ref[:, 20:end - 4, gw:]
    sums = (s2_ref[:, lead:end, 0:gw],
            s2_ref[:, lead:end, gw:2 * gw] + s2_ref[:, lead - 2:end - 2, gw:2 * gw],
            s4_ref[:, lead:end, 0:gw] + s4_ref[:, lead - 4:end - 4, 0:gw],
            s8_ref[:, lead:end, :] + s8_ref[:, lead - 8:end - 8, :])

    pos = (pos0 + l * tl + lax.broadcasted_iota(jnp.int32, (1, tl, 1), 1)).astype(F32)
    outs = []
    for gi, w in enumerate(POOL_WINDOWS):
        lo = gi * gw
        cnt = jnp.minimum(pos + 1.0, float(w))
        mix = sums[gi] / cnt - u[:, :, lo:lo + gw]
        outs.append(_dot(mix.reshape(tb * tl, gw).astype(BF16), pw_ref[gi]))
    o_ref[...] = x + jnp.concatenate(outs, axis=-1) * ps_ref[...]

    tail = xx_ref[:, end - POOL_BUF:end, :]
    if nl > 1:
        xx_ref[:, b0:lead, :] = tail

    @pl.when(l == nl - 1)
    def _():
        np_ref[...] = tail


def _pool(x, g, buf_arr, buf_index, pw, ps, *, row_start, nb, seq, tb, tl, pos0, name):
    nl = seq // tl
    r0 = row_start // (tb * tl)
    x_spec = pl.BlockSpec((tb * tl, D_MODEL), lambda b, l: (r0 + b * nl + l, 0))
    return pl.pallas_call(
        functools.partial(_pool_kernel, tb=tb, tl=tl, nl=nl, pos0=pos0),
        grid=(nb // tb, nl),
        in_specs=[
            x_spec,
            pl.BlockSpec((1, D_MODEL), lambda b, l: (0, 0)),
            buf_index(tb),
            pl.BlockSpec((len(POOL_WINDOWS), POOL_GW, POOL_GW), lambda b, l: (0, 0, 0)),
            pl.BlockSpec((1, D_MODEL), lambda b, l: (0, 0)),
        ],
        out_specs=[x_spec, pl.BlockSpec((tb, POOL_BUF, D_MODEL), lambda b, l: (b, 0, 0))],
        out_shape=[jax.ShapeDtypeStruct(x.shape, F32),
                   jax.ShapeDtypeStruct((nb, POOL_BUF, D_MODEL), F32)],
        scratch_shapes=[pltpu.VMEM((tb, POOL_LEAD + tl, D_MODEL), F32),
                        pltpu.VMEM((tb, POOL_LEAD + tl, D_MODEL), F32),
                        pltpu.VMEM((tb, POOL_LEAD + tl, D_MODEL - 2 * POOL_GW), F32),
                        pltpu.VMEM((tb, POOL_LEAD + tl, POOL_GW), F32)],
        input_output_aliases={0: 0},
        compiler_params=_params(2),
        name=name,
    )(x, g, buf_arr, pw, ps)


def _head_kv(ref, h):
    return jnp.concatenate([ref[pl.ds(c * MEM_HEADS + h, N_MEM, stride=MEM_CHUNKS), :]
                            for c in range(MEM_HEAD_DIM // LANES)], axis=1).astype(BF16)


def _xattn_prompt_kernel(x_ref, g_ref, wq_ref, k_ref, v_ref, wo_ref, o_ref):
    x = x_ref[...]
    u = _rms(x, g_ref[...]).astype(BF16)
    q = (_dot(u, wq_ref[...]) * (MEM_HEAD_DIM ** -0.5)).astype(BF16)
    outs = []
    for h in range(MEM_HEADS):
        q_h = q[:, h * MEM_HEAD_DIM:(h + 1) * MEM_HEAD_DIM]
        s = lax.dot_general(q_h, _head_kv(k_ref, h), (((1,), (1,)), ((), ())), preferred_element_type=F32)
        e = jnp.exp(s - jnp.max(s, axis=-1, keepdims=True))
        den = jnp.sum(e, axis=-1, keepdims=True)
        outs.append(_dot(e.astype(BF16), _head_kv(v_ref, h)) / den)
    o = jnp.concatenate(outs, axis=-1)
    o_ref[...] = x + _dot(o.astype(BF16), wo_ref[...])


def _xattn_prompt(x, g, wq, mk, mv, wo, layer, *, tl=1024):
    nl = SEQ // tl
    x_spec = pl.BlockSpec((tl, D_MODEL), lambda b, l: (b * nl + l, 0))
    w_spec = _resident((D_MODEL, D_MODEL), layer)
    kv_spec = pl.BlockSpec((None, N_MEM * MEM_CHUNKS, LANES), lambda b, l: (layer, b, 0))
    return pl.pallas_call(
        _xattn_prompt_kernel,
        grid=(BATCH, nl),
        in_specs=[x_spec, pl.BlockSpec((1, D_MODEL), lambda b, l: (0, 0)), w_spec, kv_spec, kv_spec, w_spec],
        out_specs=x_spec,
        out_shape=jax.ShapeDtypeStruct(x.shape, F32),
        input_output_aliases={0: 0},
        compiler_params=_params(2),
        name="xattn_prompt",
    )(x, g, wq, mk, mv, wo)


def _chunk_rows(t):
    lead = t.shape[:-3]
    t = t.reshape(*lead, N_MEM, MEM_HEADS, MEM_HEAD_DIM // LANES, LANES)
    return jnp.swapaxes(t, -2, -3).reshape(*lead, N_MEM * MEM_CHUNKS, LANES)


def _from_chunk_rows(t):
    t = t.reshape(DEPTH, BATCH, N_MEM, MEM_HEAD_DIM // LANES, MEM_HEADS, LANES)
    return jnp.swapaxes(t, -2, -3).reshape(DEPTH, BATCH, N_MEM, MEM_HEADS, MEM_HEAD_DIM)


def _chunk_to_col(j):
    return ((j % MEM_HEADS) * (MEM_HEAD_DIM // LANES) + j // MEM_HEADS) * LANES


def _attn_sample_kernel(x_ref, g_ref, wq_ref, k_ref, v_ref, wo_ref, o_ref, obuf_ref, *, tb):
    n = DEC_SEQ
    rows = lax.broadcasted_iota(jnp.int32, (MEM_HEADS * n, D_MODEL), 0) // n
    chunk = lax.broadcasted_iota(jnp.int32, (MEM_HEADS * n, D_MODEL), 1) // LANES
    own = rows == chunk % MEM_HEADS

    def gather(ref, b):
        return jnp.concatenate([ref[b, pl.ds(j, N_MEM, stride=MEM_CHUNKS), :] for j in range(MEM_CHUNKS)],
                               axis=1).astype(BF16)

    x = x_ref[...]
    q = _dot(_rms(x, g_ref[...]).astype(BF16), wq_ref[...]) * (MEM_HEAD_DIM ** -0.5)
    hn = MEM_HEADS * n
    scores = []
    for b in range(tb):
        q_b = q[b * n:(b + 1) * n, :]
        q_c = jnp.concatenate([q_b[:, _chunk_to_col(j):_chunk_to_col(j) + LANES] for j in range(MEM_CHUNKS)], axis=1)
        q_blk = jnp.where(own, jnp.concatenate([q_c] * MEM_HEADS, axis=0), 0.0).astype(BF16)
        scores.append(lax.dot_general(q_blk, gather(k_ref, b), (((1,), (1,)), ((), ())),
                                      preferred_element_type=F32))
    s = jnp.concatenate(scores, axis=0)
    e = jnp.exp(s - jnp.max(s, axis=-1, keepdims=True))
    den = jnp.sum(e, axis=-1, keepdims=True)
    e = e.astype(BF16)
    for b in range(tb):
        o = _dot(e[b * hn:(b + 1) * hn, :], gather(v_ref, b)) / den[b * hn:(b + 1) * hn, :]
        for j in range(MEM_CHUNKS):
            h = j % MEM_HEADS
            obuf_ref[b * n:(b + 1) * n, _chunk_to_col(j):_chunk_to_col(j) + LANES] = (
                o[h * n:(h + 1) * n, j * LANES:(j + 1) * LANES])
    o_ref[...] = x + _dot(obuf_ref[...].astype(BF16), wo_ref[...])


def _xattn_sample(x, g, wq, ck, cv, wo, layer, *, tb=8):
    n = DEC_SEQ
    r0 = ROWS_P // (tb * n)
    kv_spec = pl.BlockSpec((None, tb, N_MEM * MEM_CHUNKS, LANES), lambda i: (layer, i, 0, 0))
    row_spec = pl.BlockSpec((tb * n, D_MODEL), lambda i: (r0 + i, 0))
    w_spec = _resident((D_MODEL, D_MODEL), layer)
    return pl.pallas_call(
        functools.partial(_attn_sample_kernel, tb=tb),
        grid=(DEC_BATCH // tb,),
        in_specs=[row_spec, _resident((1, D_MODEL)), w_spec, kv_spec, kv_spec, w_spec],
        out_specs=row_spec,
        out_shape=jax.ShapeDtypeStruct(x.shape, F32),
        scratch_shapes=[pltpu.VMEM((tb * n, D_MODEL), F32)],
        input_output_aliases={0: 0},
        compiler_params=_params(1),
        name="xattn_sample",
    )(x, g, wq, ck, cv, wo)


def _token_major(rows):
    c = rows.shape[-1]
    return jnp.swapaxes(rows.reshape(-1, SAMPLE_TB, DEC_SEQ, c), 1, 2).reshape(-1, c)


def _sequence_major(rows):
    c = rows.shape[-1]
    return jnp.swapaxes(rows.reshape(-1, DEC_SEQ, SAMPLE_TB, c), 1, 2).reshape(-1, c)


def _row(v):
    return v.reshape(1, -1)


def _per_channel(v):
    return jnp.repeat(v, SSD_HEAD_DIM).reshape(1, D_INNER)


def _pad_lanes(v):
    return jnp.pad(v, (0, LANES - v.shape[0])).reshape(1, LANES)


def kernel(x_prompt, x_sample, mem_prompt, cache_mem_k, cache_mem_v, state_ssm, state_conv, state_pool, norm_ffn1, ffn1_w_gate, ffn1_w_up, ffn1_w_down, norm_mix, ssd_in_w, ssd_conv_w, ssd_conv_b, ssd_dt_bias, ssd_a_log, ssd_d, ssd_norm_w, ssd_out_w, pool_w, pool_scale, norm_cross, norm_mem, xa_wq, xa_wk, xa_wv, xa_wo, norm_ffn2, ffn2_w_gate, ffn2_w_up, ffn2_w_down, final_norm):
    bf = lambda w: w.astype(BF16)
    ffn1_w = (bf(ffn1_w_gate), bf(ffn1_w_up), bf(ffn1_w_down))
    ffn2_w = (bf(ffn2_w_gate), bf(ffn2_w_up), bf(ffn2_w_down))
    xa_wq_b, xa_wo_b = bf(xa_wq), bf(xa_wo)
    in_w_b = bf(ssd_in_w)
    in_wdt = ssd_in_w[:, :, DT_OFF:]
    in_wdt_pad = bf(jnp.pad(in_wdt, ((0, 0), (0, 0), (0, LANES - SSD_HEADS))))
    in_wdt_rep = bf(jnp.repeat(in_wdt, SSD_HEAD_DIM, axis=2))
    mk_p, mv_p = _mem_kv(mem_prompt.reshape(BATCH * N_MEM, D_MODEL), norm_mem.reshape(DEPTH, 1, D_MODEL),
                         bf(xa_wk), bf(xa_wv))
    ck = _chunk_rows(cache_mem_k)
    cv = _chunk_rows(cache_mem_v)
    st_s = state_ssm.reshape(-1, DEC_BATCH, D_INNER, D_STATE)
    conv0 = jnp.zeros((BATCH, CONV_W - 1, CONV_DIM), F32)
    ssm0 = jnp.zeros((BATCH, D_INNER, D_STATE), F32)
    pool0 = jnp.zeros((BATCH, POOL_BUF, D_MODEL), F32)

    ssm_p, conv_p, pool_p, conv_s, pool_s = [], [], [], [], []
    ssm_s = None
    for i in range(DEPTH):
        j = i // 2
        ffn1 = (_row(norm_ffn1[i]), *ffn1_w, i)
        if i == 0:
            x = _ffn(x_prompt.reshape(ROWS_P, D_MODEL), *ffn1, src2=x_sample.reshape(ROWS_S, D_MODEL))
        else:
            x = _ffn(x, *ffn1)
        g_mix = _row(norm_mix[i])
        if i % 2 == 0:
            cw, cb = ssd_conv_w[j], _row(ssd_conv_b[j])
            dsk, nw, w_out = _per_channel(ssd_d[j]), _row(ssd_norm_w[j]), bf(ssd_out_w[j])

            x, nc_p, ns_p = _ssd_prompt(x, g_mix, in_w_b, in_wdt_pad, j, conv0, ssm0, cw, cb,
                                        _pad_lanes(ssd_dt_bias[j]), _pad_lanes(ssd_a_log[j]), dsk, nw, w_out)
            x_s = _token_major(x[ROWS_P:])
            zx_s = _rms_matmul(x_s, g_mix, in_w_b, row_start=0, n_rows=ROWS_S, tm=1024, tn=DT_OFF // 4,
                               name="in_proj_s", layer=j, n_cols=DT_OFF)
            dtx_s = _rms_matmul(x_s, g_mix, in_wdt_rep[j], row_start=0, n_rows=ROWS_S, tm=1024, tn=D_INNER,
                                name="dt_proj_s")
            y_s, nc_s, ssm_s = _ssd_sample(zx_s, dtx_s, jnp.swapaxes(state_conv[j], 0, 1), st_s, j, cw, cb,
                                           _per_channel(ssd_dt_bias[j]), _per_channel(ssd_a_log[j]), dsk, nw,
                                           ssm_s)
            x = _matmul_residual(x, _sequence_major(y_s), w_out, row_start=ROWS_P, tm=1024, name="out_proj_s")
            nc_s = jnp.swapaxes(nc_s, 0, 1)
            conv_p.append(nc_p)
            conv_s.append(nc_s)
            ssm_p.append(ns_p.reshape(BATCH, SSD_HEADS, SSD_HEAD_DIM, D_STATE))
        else:
            pw, ps = bf(pool_w[j]), _row(pool_scale[j])
            x, np_p = _pool(x, g_mix, pool0, lambda tb: pl.BlockSpec((tb, POOL_BUF, D_MODEL), lambda b, l: (b, 0, 0)),
                            pw, ps, row_start=0, nb=BATCH, seq=SEQ, tb=1, tl=1024, pos0=0, name="pool_p")
            x, np_s = _pool(x, g_mix, state_pool,
                            lambda tb: pl.BlockSpec((None, tb, POOL_BUF, D_MODEL), lambda b, l: (j, b, 0, 0)),
                            pw, ps, row_start=ROWS_P, nb=DEC_BATCH, seq=DEC_SEQ, tb=8, tl=DEC_SEQ,
                            pos0=PAST_LEN, name="pool_s")
            pool_p.append(np_p)
            pool_s.append(np_s)

        g_x = _row(norm_cross[i])
        x = _xattn_prompt(x, g_x, xa_wq_b, mk_p, mv_p, xa_wo_b, i)
        x = _xattn_sample(x, g_x, xa_wq_b, ck, cv, xa_wo_b, i)

        ffn2 = (_row(norm_ffn2[i]), *ffn2_w, i)
        if i < DEPTH - 1:
            x = _ffn(x, *ffn2)

    g_f = _row(final_norm)
    y_prompt = _ffn(x, *ffn2, n_rows=ROWS_P, final_g=g_f).reshape(BATCH, SEQ, D_MODEL)
    y_sample = _ffn(x, *ffn2, src_row_start=ROWS_P, n_rows=ROWS_S, final_g=g_f).reshape(DEC_BATCH, DEC_SEQ, D_MODEL)
    return (y_prompt, y_sample, jnp.stack(ssm_p), jnp.stack(conv_p), jnp.stack(pool_p),
            _from_chunk_rows(mk_p), _from_chunk_rows(mv_p),
            ssm_s.reshape(state_ssm.shape), jnp.stack(conv_s), jnp.stack(pool_s))
```
